```python
import math
import jax, jax.numpy as jnp
from jax import lax
import numpy as np

D_MODEL = 1024
BATCH = 16
SEQ = 2048
DEPTH = 1
DEC_BATCH = 32
DEC_SEQ = 32
PAST_LEN = 4096

CHUNK = 64
MIX_WIDTH = D_MODEL
GDN_WIDTH = MIX_WIDTH // 2
GDN_DK = 128
GDN_DV = 128
GDN_HEADS = GDN_WIDTH // GDN_DV
GDN_QK_WIDTH = GDN_HEADS * GDN_DK
GDN_CONV_DIM = 2 * GDN_QK_WIDTH + GDN_WIDTH
CONV_WIDTH = 4
SWA_WIDTH = MIX_WIDTH - GDN_WIDTH
SWA_HEAD_DIM = 64
SWA_Q_HEADS = SWA_WIDTH // SWA_HEAD_DIM
SWA_KV_HEADS = 2
SWA_GROUP = SWA_Q_HEADS // SWA_KV_HEADS
SWA_KV_WIDTH = SWA_KV_HEADS * SWA_HEAD_DIM
WINDOW = 128
ROPE_THETA = 500000.0
ROT_DIM = SWA_HEAD_DIM // 4
IN_DIM = GDN_CONV_DIM + GDN_WIDTH + 2 * GDN_HEADS + SWA_WIDTH + 2 * SWA_KV_WIDTH
N_EXPERTS = 256
TOP_K = 8
N_GROUPS = 8
TOPK_GROUPS = 4
EXPERT_FF = 256
SHARED_FF = 256
ROUTED_SCALE = 2.5
MOE_BLOCK = 128
ALPHA = (2 * DEPTH) ** 0.25
BETA_INIT = (8 * DEPTH) ** -0.25
LN_EPS = 1e-5
NORM_EPS = 1e-6

kernel_name = 'hybrid_gdn_swa_moe_stream_step'


def layer_norm(x, g, b):
    xf = x.astype(jnp.float32)
    mu = xf.mean(-1, keepdims=True)
    var = jnp.square(xf - mu).mean(-1, keepdims=True)
    return ((xf - mu) * lax.rsqrt(var + LN_EPS)).astype(x.dtype) * g + b


def l2_normalize(t):
    tf = t.astype(jnp.float32)
    return tf * lax.rsqrt(jnp.sum(tf * tf, -1, keepdims=True) + NORM_EPS)


def gated_rms_norm(o, z, w):
    of = o.astype(jnp.float32)
    on = of * lax.rsqrt(jnp.mean(of * of, -1, keepdims=True) + NORM_EPS)
    return on.astype(z.dtype) * w * jax.nn.silu(z)


def partial_rope(x, pos):
    half = ROT_DIM // 2
    inv = ROPE_THETA ** (-jnp.arange(0, ROT_DIM, 2, dtype=jnp.float32) / ROT_DIM)
    ang = pos[:, None] * inv[None, :]
    cos = jnp.cos(ang)[None, :, None, :].astype(x.dtype)
    sin = jnp.sin(ang)[None, :, None, :].astype(x.dtype)
    x1 = x[..., :half]
    x2 = x[..., half:ROT_DIM]
    return jnp.concatenate([x1 * cos - x2 * sin, x2 * cos + x1 * sin, x[..., ROT_DIM:]], -1)


def causal_depthwise_conv(x, hist, w):
    xp = jnp.concatenate([hist.astype(x.dtype), x], axis=1)
    out = lax.conv_general_dilated(xp, w[:, None, :], (1,), 'VALID',
                                   dimension_numbers=('NWC', 'WIO', 'NWC'),
                                   feature_group_count=x.shape[-1])
    return out, xp[:, -(CONV_WIDTH - 1):]


def gdn_core(q, k, v, g, beta, s0, chunk_len):
    B, T, H, DK = q.shape
    DV = v.shape[-1]
    L = chunk_len
    NC = T // L
    f32 = jnp.float32

    def blocks(t):
        t = t.astype(f32).reshape((B, NC, L, H) + t.shape[3:])
        return jnp.moveaxis(t, 3, 2)

    q, k, v, g, beta = (blocks(t) for t in (q, k, v, g, beta))
    G = jnp.cumsum(g, axis=-1)
    li = jnp.arange(L)
    incl = li[:, None] >= li[None, :]
    strict = li[:, None] > li[None, :]
    diff = G[..., :, None] - G[..., None, :]
    dec = jnp.where(incl, jnp.exp(jnp.where(incl, diff, 0.0)), 0.0)
    kk = jnp.einsum('bchid,bchjd->bchij', k, k)
    ia = jnp.eye(L, dtype=f32) + jnp.where(strict, dec, 0.0) * kk * beta[..., :, None]
    w = lax.linalg.triangular_solve(ia, beta[..., None] * v, left_side=True, lower=True)
    y = lax.linalg.triangular_solve(ia, (beta * jnp.exp(G))[..., None] * k,
                                    left_side=True, lower=True)
    aqk = jnp.einsum('bchid,bchjd->bchij', q, k) * dec
    q_dec = jnp.exp(G)[..., None] * q
    k_dec = jnp.exp(G[..., -1:] - G)[..., None] * k
    g_last = jnp.exp(G[..., -1])

    def step(s, xs):
        w_c, y_c, a_c, q_c, k_c, g_c = xs
        u = w_c - jnp.einsum('bhlk,bhkv->bhlv', y_c, s)
        o = jnp.einsum('bhlk,bhkv->bhlv', q_c, s) + jnp.einsum('bhij,bhjv->bhiv', a_c, u)
        s = g_c[..., None, None] * s + jnp.einsum('bhlk,bhlv->bhkv', k_c, u)
        return s, o

    xs = tuple(jnp.moveaxis(t, 1, 0) for t in (w, y, aqk, q_dec, k_dec, g_last))
    s_fin, o = lax.scan(step, s0.astype(f32), xs)
    o = jnp.moveaxis(jnp.moveaxis(o, 0, 1), 2, 3).reshape(B, T, H, DV)
    return o, s_fin


def swa_attention(q, k, v, k_hist, v_hist, sinks, pos0, chunk_len):
    B, T, HQ, HD = q.shape
    L = chunk_len
    NC = T // L
    span = WINDOW + L
    k_ext = jnp.concatenate([k_hist.astype(k.dtype), k], axis=1)
    v_ext = jnp.concatenate([v_hist.astype(v.dtype), v], axis=1)
    idx = jnp.arange(NC)[:, None] * L + jnp.arange(span)[None, :]
    kb = k_ext[:, idx]
    vb = v_ext[:, idx]
    valid = (pos0 - WINDOW + idx) >= 0
    qb = q.reshape(B, NC, L, SWA_KV_HEADS, SWA_GROUP, HD)
    s = jnp.einsum('bclkgd,bcskd->bckgls', qb, kb).astype(jnp.float32) * HD ** -0.5
    s = jnp.where(valid[None, :, None, None, None, :], s, -jnp.inf)
    sink = sinks.astype(jnp.float32).reshape(SWA_KV_HEADS, SWA_GROUP)[None, None, :, :, None, None]
    m = jnp.maximum(s.max(-1, keepdims=True), sink)
    p = jnp.exp(s - m)
    p = p / (p.sum(-1, keepdims=True) + jnp.exp(sink - m))
    o = jnp.einsum('bckgls,bcskd->bclkgd', p.astype(v.dtype), vb).reshape(B, T, HQ * HD)
    return o, k_ext[:, -WINDOW:], v_ext[:, -WINDOW:]


def routed_experts(x2d, top_e, top_w, w_gate_e, w_up_e, w_down_e):
    N, D = x2d.shape
    NK = N * TOP_K
    NB = -(-NK // MOE_BLOCK) + N_EXPERTS
    flat_e = top_e.reshape(-1)
    flat_w = top_w.reshape(-1)
    flat_tok = jnp.arange(NK, dtype=jnp.int32) // TOP_K
    order = jnp.argsort(flat_e)
    se = flat_e[order]
    stok = flat_tok[order]
    sw = flat_w[order]
    counts = jnp.bincount(flat_e, length=N_EXPERTS)
    padded = (counts + MOE_BLOCK - 1) // MOE_BLOCK * MOE_BLOCK
    start = jnp.cumsum(counts) - counts
    pstart = jnp.cumsum(padded) - padded
    dest = pstart[se] + jnp.arange(NK, dtype=jnp.int32) - start[se]
    buf_tok = jnp.zeros((NB * MOE_BLOCK,), jnp.int32).at[dest].set(stok)
    buf_w = jnp.zeros((NB * MOE_BLOCK,), x2d.dtype).at[dest].set(sw)
    block_e = jnp.searchsorted(jnp.cumsum(padded), jnp.arange(NB) * MOE_BLOCK, side='right')
    block_e = jnp.clip(block_e, 0, N_EXPERTS - 1)

    def expert_block(args):
        e, tok, wt = args
        xb = x2d[tok]
        hb = jax.nn.silu(xb @ w_gate_e[e]) * (xb @ w_up_e[e])
        return (hb * wt[:, None]) @ w_down_e[e]

    yb = lax.map(expert_block, (block_e, buf_tok.reshape(NB, MOE_BLOCK),
                                buf_w.reshape(NB, MOE_BLOCK)))
    return jax.ops.segment_sum(yb.reshape(NB * MOE_BLOCK, D), buf_tok, num_segments=N)


def moe_ffn(x2d, w_router, router_bias, w_gate_e, w_up_e, w_down_e,
            w_shared_gate, w_shared_up, w_shared_down):
    N = x2d.shape[0]
    scores = jax.nn.sigmoid((x2d @ w_router).astype(jnp.float32))
    biased = scores + router_bias.astype(jnp.float32)
    grp = biased.reshape(N, N_GROUPS, N_EXPERTS // N_GROUPS)
    grp_score = lax.top_k(grp, 2)[0].sum(-1)
    top_g = lax.top_k(grp_score, TOPK_GROUPS)[1]
    gmask = jnp.any(top_g[..., None] == jnp.arange(N_GROUPS), axis=1)
    emask = jnp.repeat(gmask, N_EXPERTS // N_GROUPS, axis=1)
    top_e = lax.top_k(jnp.where(emask, biased, -jnp.inf), TOP_K)[1]
    top_w = jnp.take_along_axis(scores, top_e, axis=1)
    top_w = top_w / top_w.sum(-1, keepdims=True) * ROUTED_SCALE
    routed = routed_experts(x2d, top_e, top_w.astype(x2d.dtype), w_gate_e, w_up_e, w_down_e)
    shared = (jax.nn.silu(x2d @ w_shared_gate) * (x2d @ w_shared_up)) @ w_shared_down
    return routed + shared


def trunk_layer(x, pos0, chunk_len, conv_hist, gdn_s0, k_hist, v_hist,
                w_in, conv_w, a_log, dt_bias, gdn_norm_w, attn_sinks, w_o, ln1_g, ln1_b,
                w_router, router_bias, w_gate_e, w_up_e, w_down_e,
                w_shared_gate, w_shared_up, w_shared_down, ln2_g, ln2_b):
    B, T, D = x.shape
    f32 = jnp.float32
    h = jnp.einsum('btd,de->bte', x, w_in)
    o1 = GDN_CONV_DIM
    o2 = o1 + GDN_WIDTH
    o3 = o2 + GDN_HEADS
    o4 = o3 + GDN_HEADS
    o5 = o4 + SWA_WIDTH
    o6 = o5 + SWA_KV_WIDTH
    qkv_pre, z, b_logit, a_logit, sq, sk, sv = jnp.split(h, [o1, o2, o3, o4, o5, o6], axis=-1)

    conv_out, conv_new = causal_depthwise_conv(qkv_pre, conv_hist, conv_w)
    qkv = jax.nn.silu(conv_out)
    gq = qkv[..., :GDN_QK_WIDTH].reshape(B, T, GDN_HEADS, GDN_DK)
    gk = qkv[..., GDN_QK_WIDTH:2 * GDN_QK_WIDTH].reshape(B, T, GDN_HEADS, GDN_DK)
    gv = qkv[..., 2 * GDN_QK_WIDTH:].reshape(B, T, GDN_HEADS, GDN_DV)
    gq = l2_normalize(gq) * GDN_DK ** -0.5
    gk = l2_normalize(gk)
    beta = jax.nn.sigmoid(b_logit.astype(f32))
    decay = -jnp.exp(a_log.astype(f32)) * jax.nn.softplus(a_logit.astype(f32) + dt_bias.astype(f32))
    o, s_new = gdn_core(gq, gk, gv, decay, beta, gdn_s0, chunk_len)
    gdn_out = gated_rms_norm(o, z.reshape(B, T, GDN_HEADS, GDN_DV), gdn_norm_w)
    gdn_out = gdn_out.reshape(B, T, GDN_WIDTH)

    pos = pos0 + jnp.arange(T, dtype=f32)
    sq = partial_rope(sq.reshape(B, T, SWA_Q_HEADS, SWA_HEAD_DIM), pos)
    sk = partial_rope(sk.reshape(B, T, SWA_KV_HEADS, SWA_HEAD_DIM), pos)
    sv = sv.reshape(B, T, SWA_KV_HEADS, SWA_HEAD_DIM)
    swa_out, k_new, v_new = swa_attention(sq, sk, sv, k_hist, v_hist, attn_sinks, pos0, chunk_len)

    mix = jnp.einsum('btm,md->btd', jnp.concatenate([gdn_out, swa_out], -1), w_o)
    x1 = layer_norm(ALPHA * x + mix, ln1_g, ln1_b)
    ff = moe_ffn(x1.reshape(B * T, D), w_router, router_bias, w_gate_e, w_up_e, w_down_e,
                 w_shared_gate, w_shared_up, w_shared_down).reshape(B, T, D)
    y = layer_norm(ALPHA * x1 + ff, ln2_g, ln2_b)
    return y, conv_new, s_new.astype(gdn_s0.dtype), k_new, v_new


def setup_inputs(seed: int = 0) -> dict:
    key = jax.random.key(seed)
    ks = jax.random.split(key, 26)
    f32 = jnp.float32

    def nrm(k, shape, scale=1.0):
        return jax.random.normal(k, shape, f32) * scale

    col_scale = jnp.ones((IN_DIM,), f32)
    col_scale = col_scale.at[2 * GDN_QK_WIDTH:GDN_CONV_DIM].set(BETA_INIT)
    col_scale = col_scale.at[IN_DIM - SWA_KV_WIDTH:].set(BETA_INIT)
    dt = jnp.exp(jax.random.uniform(ks[9], (DEPTH, GDN_HEADS), f32,
                                    minval=math.log(1e-3), maxval=math.log(1e-1)))
    return {
        'x_prompt': nrm(ks[0], (BATCH, SEQ, D_MODEL)),
        'x_sample': nrm(ks[1], (DEC_BATCH, DEC_SEQ, D_MODEL)),
        'state_conv': nrm(ks[2], (DEPTH, DEC_BATCH, CONV_WIDTH - 1, GDN_CONV_DIM)),
        'state_gdn': nrm(ks[3], (DEPTH, DEC_BATCH, GDN_HEADS, GDN_DK, GDN_DV), 0.1),
        'cache_swa_k': nrm(ks[4], (DEPTH, DEC_BATCH, WINDOW, SWA_KV_HEADS, SWA_HEAD_DIM)),
        'cache_swa_v': nrm(ks[5], (DEPTH, DEC_BATCH, WINDOW, SWA_KV_HEADS, SWA_HEAD_DIM)),
        'w_in': nrm(ks[6], (DEPTH, D_MODEL, IN_DIM), D_MODEL ** -0.5) * col_scale,
        'conv_w': nrm(ks[7], (DEPTH, CONV_WIDTH, GDN_CONV_DIM), CONV_WIDTH ** -0.5),
        'a_log': jnp.log(jax.random.uniform(ks[8], (DEPTH, GDN_HEADS), f32, minval=1.0, maxval=16.0)),
        'dt_bias': dt + jnp.log(-jnp.expm1(-dt)),
        'gdn_norm_w': 1.0 + nrm(ks[10], (DEPTH, GDN_DV), 0.01),
        'attn_sinks': nrm(ks[11], (DEPTH, SWA_Q_HEADS), 0.5),
        'w_o': nrm(ks[12], (DEPTH, MIX_WIDTH, D_MODEL), MIX_WIDTH ** -0.5 * BETA_INIT),
        'ln1_g': 1.0 + nrm(ks[13], (DEPTH, D_MODEL), 0.01),
        'ln1_b': nrm(ks[14], (DEPTH, D_MODEL), 0.01),
        'w_router': nrm(ks[15], (DEPTH, D_MODEL, N_EXPERTS), D_MODEL ** -0.5),
        'router_bias': nrm(ks[16], (DEPTH, N_EXPERTS), 0.01),
        'w_gate_e': nrm(ks[17], (DEPTH, N_EXPERTS, D_MODEL, EXPERT_FF), D_MODEL ** -0.5),
        'w_up_e': nrm(ks[18], (DEPTH, N_EXPERTS, D_MODEL, EXPERT_FF), D_MODEL ** -0.5),
        'w_down_e': nrm(ks[19], (DEPTH, N_EXPERTS, EXPERT_FF, D_MODEL), EXPERT_FF ** -0.5 * BETA_INIT),
        'w_shared_gate': nrm(ks[20], (DEPTH, D_MODEL, SHARED_FF), D_MODEL ** -0.5),
        'w_shared_up': nrm(ks[21], (DEPTH, D_MODEL, SHARED_FF), D_MODEL ** -0.5),
        'w_shared_down': nrm(ks[22], (DEPTH, SHARED_FF, D_MODEL), SHARED_FF ** -0.5 * BETA_INIT),
        'ln2_g': 1.0 + nrm(ks[23], (DEPTH, D_MODEL), 0.01),
        'ln2_b': nrm(ks[24], (DEPTH, D_MODEL), 0.01),
    }


def reference(x_prompt, x_sample, state_conv, state_gdn, cache_swa_k, cache_swa_v,
              w_in, conv_w, a_log, dt_bias, gdn_norm_w, attn_sinks, w_o, ln1_g, ln1_b,
              w_router, router_bias, w_gate_e, w_up_e, w_down_e,
              w_shared_gate, w_shared_up, w_shared_down, ln2_g, ln2_b):
    weights = (w_in, conv_w, a_log, dt_bias, gdn_norm_w, attn_sinks, w_o, ln1_g, ln1_b,
               w_router, router_bias, w_gate_e, w_up_e, w_down_e,
               w_shared_gate, w_shared_up, w_shared_down, ln2_g, ln2_b)
    bp = x_prompt.shape[0]
    dtp = x_prompt.dtype
    y_prompt = x_prompt
    y_sample = x_sample
    conv_p, gdn_p, k_p, v_p = [], [], [], []
    conv_s, gdn_s, k_s, v_s = [], [], [], []
    for l in range(DEPTH):
        lw = [wt[l] for wt in weights]
        y_prompt, c, s, kr, vr = trunk_layer(
            y_prompt, 0, CHUNK,
            jnp.zeros((bp, CONV_WIDTH - 1, GDN_CONV_DIM), dtp),
            jnp.zeros((bp, GDN_HEADS, GDN_DK, GDN_DV), state_gdn.dtype),
            jnp.zeros((bp, WINDOW, SWA_KV_HEADS, SWA_HEAD_DIM), dtp),
            jnp.zeros((bp, WINDOW, SWA_KV_HEADS, SWA_HEAD_DIM), dtp),
            *lw)
        conv_p.append(c)
        gdn_p.append(s)
        k_p.append(kr)
        v_p.append(vr)
        y_sample, c, s, kr, vr = trunk_layer(
            y_sample, PAST_LEN, y_sample.shape[1],
            state_conv[l], state_gdn[l], cache_swa_k[l], cache_swa_v[l], *lw)
        conv_s.append(c)
        gdn_s.append(s)
        k_s.append(kr)
        v_s.append(vr)
    return (y_prompt, y_sample,
            jnp.stack(conv_p), jnp.stack(gdn_p), jnp.stack(k_p), jnp.stack(v_p),
            jnp.stack(conv_s), jnp.stack(gdn_s), jnp.stack(k_s), jnp.stack(v_s))
```

```python
import functools
import math

import jax
import jax.numpy as jnp
from jax import lax
from jax.experimental import pallas as pl
from jax.experimental.pallas import tpu as pltpu

F32 = jnp.float32
BF16 = jnp.bfloat16
HIGHEST = lax.Precision.HIGHEST

D_MODEL = 1024
CHUNK = 64
GDN_HEADS = 4
GDN_D = 128
GDN_WIDTH = GDN_HEADS * GDN_D
CONV_DIM = 3 * GDN_WIDTH
CONV_WIDTH = 4
SWA_HEAD_DIM = 64
SWA_Q_HEADS = 8
SWA_KV_HEADS = 2
SWA_GROUP = SWA_Q_HEADS // SWA_KV_HEADS
SWA_WIDTH = SWA_Q_HEADS * SWA_HEAD_DIM
SWA_KV_WIDTH = SWA_KV_HEADS * SWA_HEAD_DIM
WINDOW = 128
ROPE_THETA = 500000.0
ROT_DIM = SWA_HEAD_DIM // 4
ROT_HALF = ROT_DIM // 2
N_EXPERTS = 256
TOP_K = 8
N_GROUPS = 8
GROUP_SIZE = N_EXPERTS // N_GROUPS
TOPK_GROUPS = 4
EXPERT_FF = 256
ROUTED_SCALE = 2.5
ALPHA = 2.0 ** 0.25
LN_EPS = 1e-5
NORM_EPS = 1e-6
PAST_LEN = 4096

LANES = 128
HIST_ROWS = 8

C_QKV = 0
C_Z = C_QKV + CONV_DIM
C_SQ = C_Z + GDN_WIDTH
C_SK = C_SQ + SWA_WIDTH
C_SV = C_SK + SWA_KV_WIDTH
C_BA = C_SV + SWA_KV_WIDTH
C_END = C_BA + LANES

IN_TILE = 512
MIX_TILE = 256
EXPERT_BLOCK = 256
VMEM_LIMIT = 56 * 1024 * 1024


def _dot(a, b, precision=None):
    return jnp.dot(a, b, preferred_element_type=F32, precision=precision)


def _dot_nt(a, b, precision=None):
    return lax.dot_general(a, b, (((1,), (1,)), ((), ())),
                           preferred_element_type=F32, precision=precision)


def _dot_tn(a, b):
    return lax.dot_general(a, b, (((0,), (0,)), ((), ())), preferred_element_type=F32)


def _silu(x):
    return x * jax.nn.sigmoid(x)


def _in_proj_kernel(x_ref, w_ref, cos_ref, s1_ref, s2_ref,
                    qkv_ref, z_ref, sq_ref, sk_ref, sv_ref, ba_ref):
    xb = x_ref[...].astype(BF16)
    qkv_ref[...] = _dot(xb, w_ref[:, C_QKV:C_Z])
    z_ref[...] = _dot(xb, w_ref[:, C_Z:C_SQ])
    sv_ref[...] = _dot(xb, w_ref[:, C_SV:C_BA])
    ba_ref[...] = _dot(xb, w_ref[:, C_BA:C_END])
    cos = cos_ref[...]
    s1 = s1_ref[...]
    s2 = s2_ref[...]

    def rope(t):
        return (t * cos + pltpu.roll(t, LANES - ROT_HALF, 1) * s1
                + pltpu.roll(t, ROT_HALF, 1) * s2)

    sk_ref[...] = rope(_dot(xb, w_ref[:, C_SK:C_SV]))
    for g in range(SWA_WIDTH // LANES):
        lo = C_SQ + g * LANES
        sq_ref[:, g * LANES:(g + 1) * LANES] = rope(_dot(xb, w_ref[:, lo:lo + LANES])).astype(BF16)


def _in_proj(x2d, w_all, cos, s1, s2):
    n = x2d.shape[0]
    period_tiles = cos.shape[0] // IN_TILE
    row = lambda i: (i, 0)
    tab = lambda i: (i % period_tiles, 0)
    out_shapes = (
        jax.ShapeDtypeStruct((n, CONV_DIM), F32),
        jax.ShapeDtypeStruct((n, GDN_WIDTH), F32),
        jax.ShapeDtypeStruct((n, SWA_WIDTH), BF16),
        jax.ShapeDtypeStruct((n, SWA_KV_WIDTH), F32),
        jax.ShapeDtypeStruct((n, SWA_KV_WIDTH), F32),
        jax.ShapeDtypeStruct((n, LANES), F32),
    )
    return pl.pallas_call(
        _in_proj_kernel,
        grid=(n // IN_TILE,),
        in_specs=[
            pl.BlockSpec((IN_TILE, D_MODEL), row),
            pl.BlockSpec((D_MODEL, C_END), lambda i: (0, 0)),
            pl.BlockSpec((IN_TILE, LANES), tab),
            pl.BlockSpec((IN_TILE, LANES), tab),
            pl.BlockSpec((IN_TILE, LANES), tab),
        ],
        out_specs=(
            pl.BlockSpec((IN_TILE, CONV_DIM), row),
            pl.BlockSpec((IN_TILE, GDN_WIDTH), row),
            pl.BlockSpec((IN_TILE, SWA_WIDTH), row),
            pl.BlockSpec((IN_TILE, SWA_KV_WIDTH), row),
            pl.BlockSpec((IN_TILE, SWA_KV_WIDTH), row),
            pl.BlockSpec((IN_TILE, LANES), row),
        ),
        out_shape=out_shapes,
        compiler_params=pltpu.CompilerParams(
            dimension_semantics=("arbitrary",), vmem_limit_bytes=VMEM_LIMIT),
        name="in_proj",
    )(x2d, w_all, cos, s1, s2)


def _rope_tables(pos):
    p = pos.shape[0]
    inv = ROPE_THETA ** (-jnp.arange(0, ROT_DIM, 2, dtype=F32) / ROT_DIM)
    ang = pos[:, None] * inv[None, :]
    cos = jnp.cos(ang)
    sin = jnp.sin(ang)
    rest = SWA_HEAD_DIM - ROT_DIM
    head_c = jnp.concatenate([cos, cos, jnp.ones((p, rest), F32)], 1)
    head_s1 = jnp.concatenate([-sin, jnp.zeros((p, SWA_HEAD_DIM - ROT_HALF), F32)], 1)
    head_s2 = jnp.concatenate([jnp.zeros((p, ROT_HALF), F32), sin, jnp.zeros((p, rest), F32)], 1)
    two = lambda t: jnp.concatenate([t, t], 1)
    return two(head_c), two(head_s1), two(head_s2)


def _gdn_kernel(L, qkv_ref, z_ref, ba_ref, hist_ref, s0_ref, convw_ref, alog_ref, dtb_ref,
                normw_ref, o_ref, sout_ref, xbuf, state):
    c = pl.program_id(1)

    @pl.when(c == 0)
    def _():
        xbuf[0:HIST_ROWS, :] = hist_ref[0]
        state[...] = s0_ref[0]

    xbuf[HIST_ROWS:HIST_ROWS + L, :] = qkv_ref[0]

    ba = ba_ref[0]
    beta_all = jax.nn.sigmoid(ba)
    sp_in = ba + dtb_ref[...]
    softplus = jnp.maximum(sp_in, 0.0) + jnp.log1p(jnp.exp(-jnp.abs(sp_in)))
    g_all = -jnp.exp(alog_ref[...]) * softplus

    row = lax.broadcasted_iota(jnp.int32, (L, L), 0)
    col = lax.broadcasted_iota(jnp.int32, (L, L), 1)
    incl = row >= col
    strict = row > col
    lower_ones = incl.astype(F32)
    eye = (row == col).astype(F32)

    def conv_silu(c0):
        acc = None
        for j in range(CONV_WIDTH):
            lo = HIST_ROWS - (CONV_WIDTH - 1) + j
            t = xbuf[lo:lo + L, c0:c0 + GDN_D] * convw_ref[j:j + 1, c0:c0 + GDN_D]
            acc = t if acc is None else acc + t
        return _silu(acc)

    def l2n(t):
        return t * lax.rsqrt(jnp.sum(t * t, -1, keepdims=True) + NORM_EPS)

    for h in range(GDN_HEADS):
        q = l2n(conv_silu(h * GDN_D)) * (GDN_D ** -0.5)
        k = l2n(conv_silu(GDN_WIDTH + h * GDN_D))
        v = conv_silu(2 * GDN_WIDTH + h * GDN_D)
        beta = beta_all[:, h:h + 1]
        g = g_all[:, GDN_HEADS + h:GDN_HEADS + h + 1]

        diff = _dot(lower_ones, jnp.where(strict, g, 0.0), HIGHEST)
        gcum = _dot(lower_ones, jnp.broadcast_to(g, (L, GDN_D)), HIGHEST)
        dec = jnp.where(incl, jnp.exp(jnp.where(incl, diff, 0.0)), 0.0)
        exp_g = jnp.exp(gcum)
        g_end = gcum[L - 1:L, :]
        kb = k.astype(BF16)
        kk = _dot_nt(kb, kb)
        a = jnp.where(strict, dec * kk, 0.0) * beta

        inv = eye - a
        pw = -a
        for _ in range(int(math.log2(L)) - 1):
            pw = _dot(pw, pw, HIGHEST)
            inv = inv + _dot(inv, pw, HIGHEST)
        w = _dot(inv, beta * v, HIGHEST)
        y = _dot(inv, (beta * exp_g) * k, HIGHEST)

        aqk = _dot_nt(q.astype(BF16), kb) * dec
        s = state[h]
        sb = s.astype(BF16)
        u = w - _dot(y.astype(BF16), sb)
        ub = u.astype(BF16)
        o = _dot((exp_g * q).astype(BF16), sb) + _dot(aqk.astype(BF16), ub)
        k_dec = jnp.exp(g_end - gcum) * k
        state[h] = jnp.exp(g_end) * s + _dot_tn(k_dec.astype(BF16), ub)

        on = o * lax.rsqrt(jnp.mean(o * o, -1, keepdims=True) + NORM_EPS)
        zz = z_ref[0, :, h * GDN_D:(h + 1) * GDN_D]
        o_ref[0, :, h * GDN_D:(h + 1) * GDN_D] = (on * normw_ref[...] * _silu(zz)).astype(BF16)

    xbuf[0:HIST_ROWS, :] = xbuf[L:L + HIST_ROWS, :]

    @pl.when(c == pl.num_programs(1) - 1)
    def _():
        sout_ref[0] = state[...]


def _gdn(L, qkv, z, ba, hist8, s0, conv_w, alog_pad, dtb_pad, norm_w):
    b, t, _ = qkv.shape
    nc = t // L
    tok = lambda i, c: (i, c, 0)
    per_b3 = lambda i, c: (i, 0, 0)
    const2 = lambda i, c: (0, 0)
    return pl.pallas_call(
        functools.partial(_gdn_kernel, L),
        grid=(b, nc),
        in_specs=[
            pl.BlockSpec((1, L, CONV_DIM), tok),
            pl.BlockSpec((1, L, GDN_WIDTH), tok),
            pl.BlockSpec((1, L, LANES), tok),
            pl.BlockSpec((1, HIST_ROWS, CONV_DIM), per_b3),
            pl.BlockSpec((1, GDN_HEADS, GDN_D, GDN_D), lambda i, c: (i, 0, 0, 0)),
            pl.BlockSpec((CONV_WIDTH, CONV_DIM), const2),
            pl.BlockSpec((1, LANES), const2),
            pl.BlockSpec((1, LANES), const2),
            pl.BlockSpec((1, GDN_D), const2),
        ],
        out_specs=(
            pl.BlockSpec((1, L, GDN_WIDTH), tok),
            pl.BlockSpec((1, GDN_HEADS, GDN_D, GDN_D), lambda i, c: (i, 0, 0, 0)),
        ),
        out_shape=(
            jax.ShapeDtypeStruct((b, t, GDN_WIDTH), BF16),
            jax.ShapeDtypeStruct((b, GDN_HEADS, GDN_D, GDN_D), F32),
        ),
        scratch_shapes=[
            pltpu.VMEM((HIST_ROWS + L, CONV_DIM), F32),
            pltpu.VMEM((GDN_HEADS, GDN_D, GDN_D), F32),
        ],
        compiler_params=pltpu.CompilerParams(
            dimension_semantics=("arbitrary", "arbitrary"), vmem_limit_bytes=VMEM_LIMIT),
        name="gdn",
    )(qkv, z, ba, hist8, s0, conv_w, alog_pad, dtb_pad, norm_w)


def _swa_kernel(L, pos0, q_ref, k_ref, v_ref, sink_ref, o_ref):
    c = pl.program_id(1)
    span = WINDOW + L
    start = pl.multiple_of(c * L, L)
    kx = k_ref[0, pl.ds(start, span), :].astype(BF16)
    vx = v_ref[0, pl.ds(start, span), :].astype(BF16)
    key_pos = pos0 - WINDOW + c * L + lax.broadcasted_iota(jnp.int32, (1, span), 1)
    valid = key_pos >= 0
    sinks = sink_ref[...]
    for h in range(SWA_Q_HEADS):
        kh = h // SWA_GROUP
        qh = q_ref[0, :, h * SWA_HEAD_DIM:(h + 1) * SWA_HEAD_DIM]
        kk = kx[:, kh * SWA_HEAD_DIM:(kh + 1) * SWA_HEAD_DIM]
        vv = vx[:, kh * SWA_HEAD_DIM:(kh + 1) * SWA_HEAD_DIM]
        s = _dot_nt(qh, kk) * (SWA_HEAD_DIM ** -0.5)
        s = jnp.where(valid, s, -jnp.inf)
        sink = sinks[:, h:h + 1]
        m = jnp.maximum(jnp.max(s, -1, keepdims=True), sink)
        p = jnp.exp(s - m)
        den = jnp.sum(p, -1, keepdims=True) + jnp.exp(sink - m)
        o = _dot(p.astype(BF16), vv) / den
        o_ref[0, :, h * SWA_HEAD_DIM:(h + 1) * SWA_HEAD_DIM] = o.astype(BF16)


def _swa(L, pos0, q, k_ext, v_ext, sinks_pad):
    b, t, _ = q.shape
    ext = k_ext.shape[1]
    return pl.pallas_call(
        functools.partial(_swa_kernel, L, pos0),
        grid=(b, t // L),
        in_specs=[
            pl.BlockSpec((1, L, SWA_WIDTH), lambda i, c: (i, c, 0)),
            pl.BlockSpec((1, ext, SWA_KV_WIDTH), lambda i, c: (i, 0, 0)),
            pl.BlockSpec((1, ext, SWA_KV_WIDTH), lambda i, c: (i, 0, 0)),
            pl.BlockSpec((1, LANES), lambda i, c: (0, 0)),
        ],
        out_specs=pl.BlockSpec((1, L, SWA_WIDTH), lambda i, c: (i, c, 0)),
        out_shape=jax.ShapeDtypeStruct((b, t, SWA_WIDTH), BF16),
        compiler_params=pltpu.CompilerParams(
            dimension_semantics=("arbitrary", "arbitrary"), vmem_limit_bytes=VMEM_LIMIT),
        name="swa",
    )(q, k_ext, v_ext, sinks_pad)


def _layer_norm(r, g, b):
    mu = jnp.mean(r, -1, keepdims=True)
    d = r - mu
    var = jnp.mean(d * d, -1, keepdims=True)
    return d * lax.rsqrt(var + LN_EPS) * g + b


def _route(x1b, wrt_ref, rbias_ref):
    tm = x1b.shape[0]
    scores = jax.nn.sigmoid(_dot_nt(wrt_ref[...], x1b))
    biased = scores + rbias_ref[...]
    neg = -jnp.inf
    r_g = lax.broadcasted_iota(jnp.int32, (GROUP_SIZE, tm), 0)
    blocks = []
    group_score = []
    for g in range(N_GROUPS):
        blk = biased[g * GROUP_SIZE:(g + 1) * GROUP_SIZE]
        m1 = jnp.max(blk, 0, keepdims=True)
        i1 = jnp.min(jnp.where(blk == m1, r_g, GROUP_SIZE), 0, keepdims=True)
        m2 = jnp.max(jnp.where(r_g == i1, neg, blk), 0, keepdims=True)
        blocks.append(blk)
        group_score.append(m1 + m2)
    masked = []
    for g in range(N_GROUPS):
        rank = jnp.zeros((1, tm), jnp.int32)
        for o in range(N_GROUPS):
            if o == g:
                continue
            ahead = group_score[o] > group_score[g]
            if o < g:
                ahead = ahead | (group_score[o] == group_score[g])
            rank = rank + ahead.astype(jnp.int32)
        masked.append(jnp.where(rank < TOPK_GROUPS, blocks[g], neg))
    masked = jnp.concatenate(masked, 0)
    r_e = lax.broadcasted_iota(jnp.int32, (N_EXPERTS, tm), 0)
    idx_rows = []
    w_rows = []
    for _ in range(TOP_K):
        m = jnp.max(masked, 0, keepdims=True)
        idx = jnp.min(jnp.where(masked == m, r_e, N_EXPERTS), 0, keepdims=True)
        hit = r_e == idx
        w_rows.append(jnp.sum(jnp.where(hit, scores, 0.0), 0, keepdims=True))
        masked = jnp.where(hit, neg, masked)
        idx_rows.append(idx)
    total = w_rows[0]
    for wr in w_rows[1:]:
        total = total + wr
    top_w = jnp.concatenate(w_rows, 0) / total * ROUTED_SCALE
    return jnp.concatenate(idx_rows, 0), top_w


def _mix_kernel(np_tiles, gp_ref, sp_ref, xp_ref, gs_ref, ss_ref, xs_ref, wo_ref, g1_ref, b1_ref,
                wrt_ref, rbias_ref, x1_ref, x1b_ref, tope_ref, topw_ref):
    is_prompt = pl.program_id(0) < np_tiles
    gd = jnp.where(is_prompt, gp_ref[...], gs_ref[...])
    sw = jnp.where(is_prompt, sp_ref[...], ss_ref[...])
    x = jnp.where(is_prompt, xp_ref[...], xs_ref[...])
    mix = _dot(gd, wo_ref[0:GDN_WIDTH, :]) + _dot(sw, wo_ref[GDN_WIDTH:, :])
    x1 = _layer_norm(ALPHA * x + mix, g1_ref[...], b1_ref[...])
    x1b = x1.astype(BF16)
    x1_ref[...] = x1
    x1b_ref[...] = x1b
    top_e, top_w = _route(x1b, wrt_ref, rbias_ref)
    tope_ref[...] = top_e
    topw_ref[...] = top_w


def _mix(gd_p, sw_p, x_p, gd_s, sw_s, x_s, wo, g1, b1, wrt, rbias):
    n_p, n_s = x_p.shape[0], x_s.shape[0]
    np_tiles = n_p // MIX_TILE
    nt = n_p + n_s
    pm = lambda i: (jnp.minimum(i, np_tiles - 1), 0)
    sm = lambda i: (jnp.maximum(i - np_tiles, 0), 0)
    const = lambda i: (0, 0)
    return pl.pallas_call(
        functools.partial(_mix_kernel, np_tiles),
        grid=(nt // MIX_TILE,),
        in_specs=[
            pl.BlockSpec((MIX_TILE, GDN_WIDTH), pm),
            pl.BlockSpec((MIX_TILE, SWA_WIDTH), pm),
            pl.BlockSpec((MIX_TILE, D_MODEL), pm),
            pl.BlockSpec((MIX_TILE, GDN_WIDTH), sm),
            pl.BlockSpec((MIX_TILE, SWA_WIDTH), sm),
            pl.BlockSpec((MIX_TILE, D_MODEL), sm),
            pl.BlockSpec((D_MODEL, D_MODEL), const),
            pl.BlockSpec((1, D_MODEL), const),
            pl.BlockSpec((1, D_MODEL), const),
            pl.BlockSpec((N_EXPERTS, D_MODEL), const),
            pl.BlockSpec((N_EXPERTS, 1), const),
        ],
        out_specs=(
            pl.BlockSpec((MIX_TILE, D_MODEL), lambda i: (i, 0)),
            pl.BlockSpec((MIX_TILE, D_MODEL), lambda i: (i, 0)),
            pl.BlockSpec((TOP_K, MIX_TILE), lambda i: (0, i)),
            pl.BlockSpec((TOP_K, MIX_TILE), lambda i: (0, i)),
        ),
        out_shape=(
            jax.ShapeDtypeStruct((nt, D_MODEL), F32),
            jax.ShapeDtypeStruct((nt, D_MODEL), BF16),
            jax.ShapeDtypeStruct((TOP_K, nt), jnp.int32),
            jax.ShapeDtypeStruct((TOP_K, nt), F32),
        ),
        compiler_params=pltpu.CompilerParams(
            dimension_semantics=("arbitrary",), vmem_limit_bytes=VMEM_LIMIT),
        name="mix",
    )(gd_p, sw_p, x_p, gd_s, sw_s, x_s, wo, g1, b1, wrt, rbias)


def _experts_kernel(be_ref, nused_ref, xg_ref, bw_ref, wg_ref, wu_ref, wd_ref, y_ref,
                    wg_b, wu_b, wd_b):
    i = pl.program_id(0)
    prev = be_ref[jnp.maximum(i - 1, 0)]

    @pl.when((i == 0) | (be_ref[i] != prev))
    def _():
        wg_b[...] = wg_ref[0].astype(BF16)
        wu_b[...] = wu_ref[0].astype(BF16)
        wd_b[...] = wd_ref[0].astype(BF16)

    @pl.when(i < nused_ref[0])
    def _():
        xb = xg_ref[...]
        gate = _dot(xb, wg_b[...])
        up = _dot(xb, wu_b[...])
        hb = _silu(gate) * up * bw_ref[...]
        y_ref[...] = _dot(hb.astype(BF16), wd_b[...])

    @pl.when(i >= nused_ref[0])
    def _():
        y_ref[...] = jnp.zeros_like(y_ref)


def _experts(block_e, n_used, xg, bw, w_gate_e, w_up_e, w_down_e):
    nb = block_e.shape[0]
    grid_spec = pltpu.PrefetchScalarGridSpec(
        num_scalar_prefetch=2,
        grid=(nb,),
        in_specs=[
            pl.BlockSpec((EXPERT_BLOCK, D_MODEL), lambda i, be, nu: (i, 0)),
            pl.BlockSpec((EXPERT_BLOCK, 1), lambda i, be, nu: (i, 0)),
            pl.BlockSpec((1, D_MODEL, EXPERT_FF), lambda i, be, nu: (be[i], 0, 0)),
            pl.BlockSpec((1, D_MODEL, EXPERT_FF), lambda i, be, nu: (be[i], 0, 0)),
            pl.BlockSpec((1, EXPERT_FF, D_MODEL), lambda i, be, nu: (be[i], 0, 0)),
        ],
        out_specs=pl.BlockSpec((EXPERT_BLOCK, D_MODEL), lambda i, be, nu: (i, 0)),
        scratch_shapes=[
            pltpu.VMEM((D_MODEL, EXPERT_FF), BF16),
            pltpu.VMEM((D_MODEL, EXPERT_FF), BF16),
            pltpu.VMEM((EXPERT_FF, D_MODEL), BF16),
        ],
    )
    return pl.pallas_call(
        _experts_kernel,
        grid_spec=grid_spec,
        out_shape=jax.ShapeDtypeStruct((nb * EXPERT_BLOCK, D_MODEL), F32),
        compiler_params=pltpu.CompilerParams(
            dimension_semantics=("arbitrary",), vmem_limit_bytes=VMEM_LIMIT),
        name="experts",
    )(block_e, n_used, xg, bw, w_gate_e, w_up_e, w_down_e)


def _final_kernel(np_tiles, x1_ref, x1b_ref, routed_ref, wsg_ref, wsu_ref, wsd_ref, g2_ref, b2_ref,
                  yp_ref, ys_ref):
    i = pl.program_id(0)
    xb = x1b_ref[...]
    hs = _silu(_dot(xb, wsg_ref[...])) * _dot(xb, wsu_ref[...])
    ff = routed_ref[...] + _dot(hs.astype(BF16), wsd_ref[...])
    y = _layer_norm(ALPHA * x1_ref[...] + ff, g2_ref[...], b2_ref[...])

    @pl.when(i < np_tiles)
    def _():
        yp_ref[...] = y

    @pl.when(i >= np_tiles)
    def _():
        ys_ref[...] = y


def _final(n_p, x1, x1b, routed, wsg, wsu, wsd, g2, b2):
    nt = x1.shape[0]
    np_tiles = n_p // MIX_TILE
    row = lambda i: (i, 0)
    const = lambda i: (0, 0)
    return pl.pallas_call(
        functools.partial(_final_kernel, np_tiles),
        grid=(nt // MIX_TILE,),
        in_specs=[
            pl.BlockSpec((MIX_TILE, D_MODEL), row),
            pl.BlockSpec((MIX_TILE, D_MODEL), row),
            pl.BlockSpec((MIX_TILE, D_MODEL), row),
            pl.BlockSpec((D_MODEL, EXPERT_FF), const),
            pl.BlockSpec((D_MODEL, EXPERT_FF), const),
            pl.BlockSpec((EXPERT_FF, D_MODEL), const),
            pl.BlockSpec((1, D_MODEL), const),
            pl.BlockSpec((1, D_MODEL), const),
        ],
        out_specs=(
            pl.BlockSpec((MIX_TILE, D_MODEL), lambda i: (jnp.minimum(i, np_tiles - 1), 0)),
            pl.BlockSpec((MIX_TILE, D_MODEL), lambda i: (jnp.maximum(i - np_tiles, 0), 0)),
        ),
        out_shape=(
            jax.ShapeDtypeStruct((n_p, D_MODEL), F32),
            jax.ShapeDtypeStruct((nt - n_p, D_MODEL), F32),
        ),
        compiler_params=pltpu.CompilerParams(
            dimension_semantics=("arbitrary",), vmem_limit_bytes=VMEM_LIMIT),
        name="final",
    )(x1, x1b, routed, wsg, wsu, wsd, g2, b2)


def _expert_order(top_e_t, top_w_t):
    k, nt = top_e_t.shape
    nk = k * nt
    nb = -(-nk // EXPERT_BLOCK) + N_EXPERTS
    flat_e = top_e_t.reshape(-1)
    flat_w = top_w_t.reshape(-1)
    flat_tok = jnp.arange(nk, dtype=jnp.int32) % nt
    order = jnp.argsort(flat_e)
    se = flat_e[order]
    counts = jnp.bincount(flat_e, length=N_EXPERTS)
    padded = (counts + EXPERT_BLOCK - 1) // EXPERT_BLOCK * EXPERT_BLOCK
    start = jnp.cumsum(counts) - counts
    pend = jnp.cumsum(padded)
    pstart = pend - padded
    dest_sorted = (pstart[se] + jnp.arange(nk, dtype=jnp.int32) - start[se]).astype(jnp.int32)
    buf_tok = jnp.zeros((nb * EXPERT_BLOCK,), jnp.int32).at[dest_sorted].set(flat_tok[order])
    buf_w = jnp.zeros((nb * EXPERT_BLOCK,), F32).at[dest_sorted].set(flat_w[order])
    block_e = jnp.searchsorted(pend, jnp.arange(nb) * EXPERT_BLOCK, side='right')
    block_e = jnp.clip(block_e, 0, N_EXPERTS - 1).astype(jnp.int32)
    n_used = (pend[-1] // EXPERT_BLOCK).astype(jnp.int32).reshape(1)
    dest_flat = jnp.zeros((nk,), jnp.int32).at[order].set(dest_sorted)
    return buf_tok, buf_w, block_e, n_used, dest_flat


def _pad_lanes(v, offset=0):
    out = jnp.zeros((1, LANES), F32)
    return out.at[0, offset:offset + v.shape[0]].set(v.astype(F32))


def kernel(x_prompt, x_sample, state_conv, state_gdn, cache_swa_k, cache_swa_v, w_in, conv_w, a_log, dt_bias, gdn_norm_w, attn_sinks, w_o, ln1_g, ln1_b, w_router, router_bias, w_gate_e, w_up_e, w_down_e, w_shared_gate, w_shared_up, w_shared_down, ln2_g, ln2_b):
    bp, tp, d = x_prompt.shape
    bs, ts, _ = x_sample.shape
    n_p, n_s = bp * tp, bs * ts

    wi = w_in[0]
    o1 = CONV_DIM
    o2 = o1 + GDN_WIDTH
    o4 = o2 + 2 * GDN_HEADS
    o5 = o4 + SWA_WIDTH
    o6 = o5 + SWA_KV_WIDTH
    ba_cols = jnp.zeros((d, LANES), F32).at[:, :2 * GDN_HEADS].set(wi[:, o2:o4])
    w_all = jnp.concatenate([wi[:, :o2], wi[:, o4:o5], wi[:, o5:o6], wi[:, o6:], ba_cols], 1).astype(BF16)
    wo = w_o[0].astype(BF16)
    wrt = w_router[0].T.astype(BF16)
    rbias = router_bias[0].astype(F32).reshape(N_EXPERTS, 1)
    alog_pad = _pad_lanes(a_log[0], GDN_HEADS)
    dtb_pad = _pad_lanes(dt_bias[0], GDN_HEADS)
    sinks_pad = _pad_lanes(attn_sinks[0])
    norm_w = gdn_norm_w[0].reshape(1, GDN_D)
    g1, b1 = ln1_g[0].reshape(1, d), ln1_b[0].reshape(1, d)
    g2, b2 = ln2_g[0].reshape(1, d), ln2_b[0].reshape(1, d)

    def front(x, pos0, L, conv_hist, s0, k_hist, v_hist):
        b, t, _ = x.shape
        period = max(t, IN_TILE)
        pos = pos0 + (jnp.arange(period, dtype=jnp.int32) % t).astype(F32)
        cos, s1, s2 = _rope_tables(pos)
        x2d = x.reshape(b * t, d)
        qkv, z, sq, sk, sv, ba = _in_proj(x2d, w_all, cos, s1, s2)
        qkv = qkv.reshape(b, t, CONV_DIM)
        hist8 = jnp.concatenate(
            [jnp.zeros((b, HIST_ROWS - (CONV_WIDTH - 1), CONV_DIM), F32), conv_hist], 1)
        gd, s_new = _gdn(L, qkv, z.reshape(b, t, GDN_WIDTH), ba.reshape(b, t, LANES), hist8, s0,
                         conv_w[0], alog_pad, dtb_pad, norm_w)
        k_ext = jnp.concatenate([k_hist.reshape(b, WINDOW, SWA_KV_WIDTH),
                                 sk.reshape(b, t, SWA_KV_WIDTH)], 1)
        v_ext = jnp.concatenate([v_hist.reshape(b, WINDOW, SWA_KV_WIDTH),
                                 sv.reshape(b, t, SWA_KV_WIDTH)], 1)
        sw = _swa(L, pos0, sq.reshape(b, t, SWA_WIDTH), k_ext, v_ext, sinks_pad)
        conv_new = jnp.concatenate([conv_hist, qkv], 1)[:, -(CONV_WIDTH - 1):]
        k_new = k_ext[:, -WINDOW:].reshape(b, WINDOW, SWA_KV_HEADS, SWA_HEAD_DIM)
        v_new = v_ext[:, -WINDOW:].reshape(b, WINDOW, SWA_KV_HEADS, SWA_HEAD_DIM)
        return (x2d, gd.reshape(b * t, GDN_WIDTH), sw.reshape(b * t, SWA_WIDTH),
                conv_new, s_new, k_new, v_new)

    zeros = lambda *s: jnp.zeros(s, F32)
    xp2, gd_p, sw_p, conv_p, gdn_p, k_p, v_p = front(
        x_prompt, 0, CHUNK, zeros(bp, CONV_WIDTH - 1, CONV_DIM),
        zeros(bp, GDN_HEADS, GDN_D, GDN_D), zeros(bp, WINDOW, SWA_KV_WIDTH),
        zeros(bp, WINDOW, SWA_KV_WIDTH))
    xs2, gd_s, sw_s, conv_s, gdn_s, k_s, v_s = front(
        x_sample, PAST_LEN, ts, state_conv[0], state_gdn[0], cache_swa_k[0], cache_swa_v[0])

    x1, x1b, top_e_t, top_w_t = _mix(gd_p, sw_p, xp2, gd_s, sw_s, xs2, wo, g1, b1, wrt, rbias)

    buf_tok, buf_w, block_e, n_used, dest_flat = _expert_order(top_e_t, top_w_t)
    xg = x1b[buf_tok]
    yb = _experts(block_e, n_used, xg, buf_w.reshape(-1, 1), w_gate_e[0], w_up_e[0], w_down_e[0])
    routed = yb[dest_flat].reshape(TOP_K, n_p + n_s, d).sum(0)

    y_p, y_s = _final(n_p, x1, x1b, routed,
                      w_shared_gate[0].astype(BF16), w_shared_up[0].astype(BF16),
                      w_shared_down[0].astype(BF16), g2, b2)
    return (y_p.reshape(bp, tp, d), y_s.reshape(bs, ts, d),
            conv_p[None], gdn_p[None], k_p[None], v_p[None],
            conv_s[None], gdn_s[None], k_s[None], v_s[None])
```

```python
import functools
import math

import jax
import jax.numpy as jnp
from jax import lax
from jax.experimental import pallas as pl
from jax.experimental.pallas import tpu as pltpu

F32 = jnp.float32
BF16 = jnp.bfloat16
HIGHEST = lax.Precision.HIGHEST

D_MODEL = 1024
CHUNK = 64
GDN_HEADS = 4
GDN_D = 128
GDN_WIDTH = GDN_HEADS * GDN_D
CONV_DIM = 3 * GDN_WIDTH
CONV_WIDTH = 4
SWA_HEAD_DIM = 64
SWA_Q_HEADS = 8
SWA_KV_HEADS = 2
SWA_GROUP = SWA_Q_HEADS // SWA_KV_HEADS
SWA_WIDTH = SWA_Q_HEADS * SWA_HEAD_DIM
SWA_KV_WIDTH = SWA_KV_HEADS * SWA_HEAD_DIM
WINDOW = 128
ROPE_THETA = 500000.0
ROT_DIM = SWA_HEAD_DIM // 4
ROT_HALF = ROT_DIM // 2
N_EXPERTS = 256
TOP_K = 8
N_GROUPS = 8
GROUP_SIZE = N_EXPERTS // N_GROUPS
TOPK_GROUPS = 4
EXPERT_FF = 256
ROUTED_SCALE = 2.5
ALPHA = 2.0 ** 0.25
LN_EPS = 1e-5
NORM_EPS = 1e-6
PAST_LEN = 4096

LANES = 128
HIST_ROWS = 8

C_QKV = 0
C_Z = C_QKV + CONV_DIM
C_SQ = C_Z + GDN_WIDTH
C_SK = C_SQ + SWA_WIDTH
C_SV = C_SK + SWA_KV_WIDTH
C_BA = C_SV + SWA_KV_WIDTH
C_END = C_BA + LANES

IN_TILE = 512
MIX_TILE = 256
DEST_TILE = 512
EXPERT_BLOCK = 256
GDN_BATCH = 2
VMEM_LIMIT = 56 * 1024 * 1024


def _dot(a, b, precision=None):
    return jnp.dot(a, b, preferred_element_type=F32, precision=precision)


def _dot_nt(a, b, precision=None):
    return lax.dot_general(a, b, (((1,), (1,)), ((), ())),
                           preferred_element_type=F32, precision=precision)


def _dot_tn(a, b):
    return lax.dot_general(a, b, (((0,), (0,)), ((), ())), preferred_element_type=F32)


def _split(x):
    hi = x.astype(BF16)
    return hi, (x - hi.astype(F32)).astype(BF16)


def _dot_split(a, b):
    return _dot(a[0], b[0]) + (_dot(a[1], b[0]) + _dot(a[0], b[1]))


def _silu(x):
    return x * jax.nn.sigmoid(x)


def _in_proj_kernel(x_ref, w_ref, cos_ref, s1_ref, s2_ref,
                    qkv_ref, z_ref, sq_ref, sk_ref, sv_ref, ba_ref):
    xb = x_ref[...].astype(BF16)
    qkv_ref[...] = _dot(xb, w_ref[:, C_QKV:C_Z])
    z_ref[...] = _dot(xb, w_ref[:, C_Z:C_SQ])
    sv_ref[...] = _dot(xb, w_ref[:, C_SV:C_BA])
    ba_ref[...] = _dot(xb, w_ref[:, C_BA:C_END])
    cos = cos_ref[...]
    s1 = s1_ref[...]
    s2 = s2_ref[...]

    def rope(t):
        return (t * cos + pltpu.roll(t, LANES - ROT_HALF, 1) * s1
                + pltpu.roll(t, ROT_HALF, 1) * s2)

    sk_ref[...] = rope(_dot(xb, w_ref[:, C_SK:C_SV]))
    for g in range(SWA_WIDTH // LANES):
        lo = C_SQ + g * LANES
        sq_ref[:, g * LANES:(g + 1) * LANES] = rope(_dot(xb, w_ref[:, lo:lo + LANES])).astype(BF16)


def _in_proj(x2d, w_all, cos, s1, s2):
    n = x2d.shape[0]
    period_tiles = cos.shape[0] // IN_TILE
    row = lambda i: (i, 0)
    tab = lambda i: (i % period_tiles, 0)
    out_shapes = (
        jax.ShapeDtypeStruct((n, CONV_DIM), F32),
        jax.ShapeDtypeStruct((n, GDN_WIDTH), F32),
        jax.ShapeDtypeStruct((n, SWA_WIDTH), BF16),
        jax.ShapeDtypeStruct((n, SWA_KV_WIDTH), F32),
        jax.ShapeDtypeStruct((n, SWA_KV_WIDTH), F32),
        jax.ShapeDtypeStruct((n, LANES), F32),
    )
    return pl.pallas_call(
        _in_proj_kernel,
        grid=(n // IN_TILE,),
        in_specs=[
            pl.BlockSpec((IN_TILE, D_MODEL), row),
            pl.BlockSpec((D_MODEL, C_END), lambda i: (0, 0)),
            pl.BlockSpec((IN_TILE, LANES), tab),
            pl.BlockSpec((IN_TILE, LANES), tab),
            pl.BlockSpec((IN_TILE, LANES), tab),
        ],
        out_specs=(
            pl.BlockSpec((IN_TILE, CONV_DIM), row),
            pl.BlockSpec((IN_TILE, GDN_WIDTH), row),
            pl.BlockSpec((IN_TILE, SWA_WIDTH), row),
            pl.BlockSpec((IN_TILE, SWA_KV_WIDTH), row),
            pl.BlockSpec((IN_TILE, SWA_KV_WIDTH), row),
            pl.BlockSpec((IN_TILE, LANES), row),
        ),
        out_shape=out_shapes,
        compiler_params=pltpu.CompilerParams(
            dimension_semantics=("arbitrary",), vmem_limit_bytes=VMEM_LIMIT),
        name="in_proj",
    )(x2d, w_all, cos, s1, s2)


def _rope_tables(pos):
    p = pos.shape[0]
    inv = ROPE_THETA ** (-jnp.arange(0, ROT_DIM, 2, dtype=F32) / ROT_DIM)
    ang = pos[:, None] * inv[None, :]
    cos = jnp.cos(ang)
    sin = jnp.sin(ang)
    rest = SWA_HEAD_DIM - ROT_DIM
    head_c = jnp.concatenate([cos, cos, jnp.ones((p, rest), F32)], 1)
    head_s1 = jnp.concatenate([-sin, jnp.zeros((p, SWA_HEAD_DIM - ROT_HALF), F32)], 1)
    head_s2 = jnp.concatenate([jnp.zeros((p, ROT_HALF), F32), sin, jnp.zeros((p, rest), F32)], 1)
    two = lambda t: jnp.concatenate([t, t], 1)
    return two(head_c), two(head_s1), two(head_s2)


def _gdn_kernel(L, qkv_ref, z_ref, ba_ref, hist_ref, s0_ref, convw_ref, alog_ref, dtb_ref,
                normw_ref, o_ref, sout_ref, xbuf, state):
    c = pl.program_id(1)
    nbat = qkv_ref.shape[0]
    chains = [(b, h) for b in range(nbat) for h in range(GDN_HEADS)]

    @pl.when(c == 0)
    def _():
        xbuf[:, 0:HIST_ROWS, :] = hist_ref[...]
        state[...] = s0_ref[...]

    xbuf[:, HIST_ROWS:HIST_ROWS + L, :] = qkv_ref[...]

    row = lax.broadcasted_iota(jnp.int32, (L, L), 0)
    col = lax.broadcasted_iota(jnp.int32, (L, L), 1)
    incl = row >= col
    strict = row > col
    lower_ones = incl.astype(BF16)
    eye = (row == col).astype(F32)

    beta_all, g_all = [], []
    for b in range(nbat):
        ba = ba_ref[b]
        beta_all.append(jax.nn.sigmoid(ba))
        sp_in = ba + dtb_ref[...]
        softplus = jnp.maximum(sp_in, 0.0) + jnp.log1p(jnp.exp(-jnp.abs(sp_in)))
        g_all.append(-jnp.exp(alog_ref[...]) * softplus)

    def conv_silu(b, c0):
        acc = None
        for j in range(CONV_WIDTH):
            lo = HIST_ROWS - (CONV_WIDTH - 1) + j
            t = xbuf[b, lo:lo + L, c0:c0 + GDN_D] * convw_ref[j:j + 1, c0:c0 + GDN_D]
            acc = t if acc is None else acc + t
        return _silu(acc)

    def l2n(t):
        return t * lax.rsqrt(jnp.sum(t * t, -1, keepdims=True) + NORM_EPS)

    q = [l2n(conv_silu(b, h * GDN_D)) * (GDN_D ** -0.5) for b, h in chains]
    k = [l2n(conv_silu(b, GDN_WIDTH + h * GDN_D)) for b, h in chains]
    v = [conv_silu(b, 2 * GDN_WIDTH + h * GDN_D) for b, h in chains]
    beta = [beta_all[b][:, h:h + 1] for b, h in chains]
    g = [g_all[b][:, GDN_HEADS + h:GDN_HEADS + h + 1] for b, h in chains]

    def pieces(t):
        p1 = t.astype(BF16).astype(F32)
        r1 = t - p1
        p2 = r1.astype(BF16).astype(F32)
        return p1, p2, r1 - p2

    g_parts = [pieces(t) for t in g]
    diff = [sum(_dot(lower_ones, jnp.where(strict, p, 0.0).astype(BF16)) for p in gp)
            for gp in g_parts]
    gcum = [sum(_dot(lower_ones, jnp.broadcast_to(p, (L, GDN_D)).astype(BF16)) for p in gp)
            for gp in g_parts]
    dec = [jnp.where(incl, jnp.exp(jnp.where(incl, d, 0.0)), 0.0) for d in diff]
    exp_g = [jnp.exp(t) for t in gcum]
    g_end = [t[L - 1:L, :] for t in gcum]
    kb = [t.astype(BF16) for t in k]
    kk = [_dot_nt(t, t) for t in kb]
    aqk = [_dot_nt(qq.astype(BF16), t) * d for qq, t, d in zip(q, kb, dec)]
    a = [jnp.where(strict, d * t, 0.0) * bt for d, t, bt in zip(dec, kk, beta)]

    inv = [eye - t for t in a]
    pw = [_split(-t) for t in a]
    for _ in range(int(math.log2(L)) - 1):
        pw = [_split(_dot_split(p, p)) for p in pw]
        inv = [t + _dot_split(_split(t), p) for t, p in zip(inv, pw)]
    inv_s = [_split(t) for t in inv]
    w = [_dot_split(t, _split(bt * vv)) for t, bt, vv in zip(inv_s, beta, v)]
    y = [_dot_split(t, _split((bt * eg) * kk_)) for t, bt, eg, kk_ in zip(inv_s, beta, exp_g, k)]

    s = [state[b, h] for b, h in chains]
    sb = [t.astype(BF16) for t in s]
    u = [ww - _dot(yy.astype(BF16), t) for ww, yy, t in zip(w, y, sb)]
    ub = [t.astype(BF16) for t in u]
    o = [_dot((eg * qq).astype(BF16), t) + _dot(aa.astype(BF16), uu)
         for eg, qq, t, aa, uu in zip(exp_g, q, sb, aqk, ub)]
    k_dec = [(jnp.exp(ge - gc) * kk_).astype(BF16) for ge, gc, kk_ in zip(g_end, gcum, k)]
    s_new = [jnp.exp(ge) * t + _dot_tn(kd, uu) for ge, t, kd, uu in zip(g_end, s, k_dec, ub)]

    for (b, h), t, oo in zip(chains, s_new, o):
        state[b, h] = t
        on = oo * lax.rsqrt(jnp.mean(oo * oo, -1, keepdims=True) + NORM_EPS)
        zz = z_ref[b, :, h * GDN_D:(h + 1) * GDN_D]
        o_ref[b, :, h * GDN_D:(h + 1) * GDN_D] = (on * normw_ref[...] * _silu(zz)).astype(BF16)

    xbuf[:, 0:HIST_ROWS, :] = xbuf[:, L:L + HIST_ROWS, :]

    @pl.when(c == pl.num_programs(1) - 1)
    def _():
        sout_ref[...] = state[...]


def _gdn(L, qkv, z, ba, hist8, s0, conv_w, alog_pad, dtb_pad, norm_w):
    b, t, _ = qkv.shape
    nc = t // L
    nbat = GDN_BATCH
    tok = lambda i, c: (i, c, 0)
    per_b3 = lambda i, c: (i, 0, 0)
    per_b4 = lambda i, c: (i, 0, 0, 0)
    const2 = lambda i, c: (0, 0)
    return pl.pallas_call(
        functools.partial(_gdn_kernel, L),
        grid=(b // nbat, nc),
        in_specs=[
            pl.BlockSpec((nbat, L, CONV_DIM), tok),
            pl.BlockSpec((nbat, L, GDN_WIDTH), tok),
            pl.BlockSpec((nbat, L, LANES), tok),
            pl.BlockSpec((nbat, HIST_ROWS, CONV_DIM), per_b3),
            pl.BlockSpec((nbat, GDN_HEADS, GDN_D, GDN_D), per_b4),
            pl.BlockSpec((CONV_WIDTH, CONV_DIM), const2),
            pl.BlockSpec((1, LANES), const2),
            pl.BlockSpec((1, LANES), const2),
            pl.BlockSpec((1, GDN_D), const2),
        ],
        out_specs=(
            pl.BlockSpec((nbat, L, GDN_WIDTH), tok),
            pl.BlockSpec((nbat, GDN_HEADS, GDN_D, GDN_D), per_b4),
        ),
        out_shape=(
            jax.ShapeDtypeStruct((b, t, GDN_WIDTH), BF16),
            jax.ShapeDtypeStruct((b, GDN_HEADS, GDN_D, GDN_D), F32),
        ),
        scratch_shapes=[
            pltpu.VMEM((nbat, HIST_ROWS + L, CONV_DIM), F32),
            pltpu.VMEM((nbat, GDN_HEADS, GDN_D, GDN_D), F32),
        ],
        compiler_params=pltpu.CompilerParams(
            dimension_semantics=("arbitrary", "arbitrary"), vmem_limit_bytes=VMEM_LIMIT),
        name="gdn",
    )(qkv, z, ba, hist8, s0, conv_w, alog_pad, dtb_pad, norm_w)


def _swa_kernel(L, pos0, q_ref, k_ref, v_ref, sink_ref, o_ref):
    c = pl.program_id(1)
    span = WINDOW + L
    start = pl.multiple_of(c * L, L)
    kx = k_ref[0, pl.ds(start, span), :]
    vx = v_ref[0, pl.ds(start, span), :]
    kx_sw = pltpu.roll(kx, SWA_HEAD_DIM, 1)
    vx_sw = pltpu.roll(vx, SWA_HEAD_DIM, 1)
    low = lax.broadcasted_iota(jnp.int32, (span, LANES), 1) < SWA_HEAD_DIM

    def halves(x, x_sw, kh):
        src_lo, src_hi = (x, x_sw) if kh == 0 else (x_sw, x)
        return (jnp.where(low, src_lo, 0.0).astype(BF16), jnp.where(low, 0.0, src_hi).astype(BF16))

    key_pos = pos0 - WINDOW + c * L + lax.broadcasted_iota(jnp.int32, (1, span), 1)
    valid = key_pos >= 0
    sinks = sink_ref[...]
    top_rows = lax.broadcasted_iota(jnp.int32, (2 * L, 1), 0) < L

    scores, sink_cols, v_halves = [], [], []
    for kh in range(SWA_KV_HEADS):
        c0 = kh * 2 * LANES
        q4 = jnp.concatenate([q_ref[0, :, c0:c0 + LANES], q_ref[0, :, c0 + LANES:c0 + 2 * LANES]], 0)
        k_halves = halves(kx, kx_sw, kh)
        v_halves.append(halves(vx, vx_sw, kh))
        for half in range(2):
            h_top = kh * SWA_GROUP + half
            s = _dot_nt(q4, k_halves[half]) * (SWA_HEAD_DIM ** -0.5)
            scores.append(jnp.where(valid, s, -jnp.inf))
            sink_cols.append(jnp.where(top_rows, sinks[:, h_top:h_top + 1],
                                       sinks[:, h_top + 2:h_top + 3]))
    m = [jnp.maximum(jnp.max(s, -1, keepdims=True), sk) for s, sk in zip(scores, sink_cols)]
    p = [jnp.exp(s - mm) for s, mm in zip(scores, m)]
    den = [jnp.sum(pp, -1, keepdims=True) + jnp.exp(sk - mm) for pp, sk, mm in zip(p, sink_cols, m)]
    pb = [pp.astype(BF16) for pp in p]
    for kh in range(SWA_KV_HEADS):
        va, vb = v_halves[kh]
        o = _dot(pb[2 * kh], va) / den[2 * kh] + _dot(pb[2 * kh + 1], vb) / den[2 * kh + 1]
        c0 = kh * 2 * LANES
        o_ref[0, :, c0:c0 + LANES] = o[0:L].astype(BF16)
        o_ref[0, :, c0 + LANES:c0 + 2 * LANES] = o[L:2 * L].astype(BF16)


def _swa(L, pos0, q, k_ext, v_ext, sinks_pad):
    b, t, _ = q.shape
    ext = k_ext.shape[1]
    return pl.pallas_call(
        functools.partial(_swa_kernel, L, pos0),
        grid=(b, t // L),
        in_specs=[
            pl.BlockSpec((1, L, SWA_WIDTH), lambda i, c: (i, c, 0)),
            pl.BlockSpec((1, ext, SWA_KV_WIDTH), lambda i, c: (i, 0, 0)),
            pl.BlockSpec((1, ext, SWA_KV_WIDTH), lambda i, c: (i, 0, 0)),
            pl.BlockSpec((1, LANES), lambda i, c: (0, 0)),
        ],
        out_specs=pl.BlockSpec((1, L, SWA_WIDTH), lambda i, c: (i, c, 0)),
        out_shape=jax.ShapeDtypeStruct((b, t, SWA_WIDTH), BF16),
        compiler_params=pltpu.CompilerParams(
            dimension_semantics=("arbitrary", "arbitrary"), vmem_limit_bytes=VMEM_LIMIT),
        name="swa",
    )(q, k_ext, v_ext, sinks_pad)


def _layer_norm(r, g, b):
    mu = jnp.mean(r, -1, keepdims=True)
    d = r - mu
    var = jnp.mean(d * d, -1, keepdims=True)
    return d * lax.rsqrt(var + LN_EPS) * g + b


def _route(x1b, wrt_ref, rbias_ref, count_ref):
    tm = x1b.shape[0]
    scores = jax.nn.sigmoid(_dot_nt(wrt_ref[...], x1b))
    biased = scores + rbias_ref[...]
    neg = -jnp.inf
    r_g = lax.broadcasted_iota(jnp.int32, (GROUP_SIZE, tm), 0)
    blocks = []
    group_score = []
    for g in range(N_GROUPS):
        blk = biased[g * GROUP_SIZE:(g + 1) * GROUP_SIZE]
        m1 = jnp.max(blk, 0, keepdims=True)
        i1 = jnp.min(jnp.where(blk == m1, r_g, GROUP_SIZE), 0, keepdims=True)
        m2 = jnp.max(jnp.where(r_g == i1, neg, blk), 0, keepdims=True)
        blocks.append(blk)
        group_score.append(m1 + m2)
    masked = []
    for g in range(N_GROUPS):
        rank = jnp.zeros((1, tm), jnp.int32)
        for o in range(N_GROUPS):
            if o == g:
                continue
            ahead = group_score[o] > group_score[g]
            if o < g:
                ahead = ahead | (group_score[o] == group_score[g])
            rank = rank + ahead.astype(jnp.int32)
        masked.append(jnp.where(rank < TOPK_GROUPS, blocks[g], neg))
    masked = jnp.concatenate(masked, 0)
    r_e = lax.broadcasted_iota(jnp.int32, (N_EXPERTS, tm), 0)
    idx_rows = []
    w_rows = []
    hits = []
    for _ in range(TOP_K):
        m = jnp.max(masked, 0, keepdims=True)
        idx = jnp.min(jnp.where(masked == m, r_e, N_EXPERTS), 0, keepdims=True)
        hit = r_e == idx
        w_rows.append(jnp.sum(jnp.where(hit, scores, 0.0), 0, keepdims=True))
        masked = jnp.where(hit, neg, masked)
        idx_rows.append(idx)
        hits.append(hit)
    total = w_rows[0]
    for wr in w_rows[1:]:
        total = total + wr
    top_w = jnp.concatenate(w_rows, 0) / total * ROUTED_SCALE

    chosen = jnp.zeros((N_EXPERTS, tm), F32)
    for hit in hits:
        chosen = chosen + hit.astype(F32)
    t_r = lax.broadcasted_iota(jnp.int32, (tm, tm), 0)
    t_c = lax.broadcasted_iota(jnp.int32, (tm, tm), 1)
    before = (t_r < t_c).astype(BF16)
    prior = _dot(chosen.astype(BF16), before) + count_ref[...]
    rank_rows = [jnp.sum(jnp.where(hit, prior, 0.0), 0, keepdims=True) for hit in hits]
    count_ref[...] = count_ref[...] + jnp.sum(chosen, 1, keepdims=True)
    rank = jnp.concatenate(rank_rows, 0).astype(jnp.int32)
    return jnp.concatenate(idx_rows, 0), top_w, rank


def _mix_kernel(np_tiles, gp_ref, sp_ref, xp_ref, gs_ref, ss_ref, xs_ref, wo_ref, g1_ref, b1_ref,
                wrt_ref, rbias_ref, x1_ref, tope_ref, topw_ref, rank_ref, cnt_ref, count_acc):
    i = pl.program_id(0)

    @pl.when(i == 0)
    def _():
        count_acc[...] = jnp.zeros_like(count_acc)

    is_prompt = i < np_tiles
    gd = jnp.where(is_prompt, gp_ref[...], gs_ref[...])
    sw = jnp.where(is_prompt, sp_ref[...], ss_ref[...])
    x = jnp.where(is_prompt, xp_ref[...], xs_ref[...])
    mix = _dot(gd, wo_ref[0:GDN_WIDTH, :]) + _dot(sw, wo_ref[GDN_WIDTH:, :])
    x1 = _layer_norm(ALPHA * x + mix, g1_ref[...], b1_ref[...])
    x1_ref[...] = x1
    top_e, top_w, rank = _route(x1.astype(BF16), wrt_ref, rbias_ref, count_acc)
    tope_ref[...] = top_e
    topw_ref[...] = top_w
    rank_ref[...] = rank
    cnt_ref[...] = count_acc[...].astype(jnp.int32)


def _mix(gd_p, sw_p, x_p, gd_s, sw_s, x_s, wo, g1, b1, wrt, rbias):
    n_p, n_s = x_p.shape[0], x_s.shape[0]
    np_tiles = n_p // MIX_TILE
    nt = n_p + n_s
    pm = lambda i: (jnp.minimum(i, np_tiles - 1), 0)
    sm = lambda i: (jnp.maximum(i - np_tiles, 0), 0)
    const = lambda i: (0, 0)
    col = lambda i: (0, i)
    return pl.pallas_call(
        functools.partial(_mix_kernel, np_tiles),
        grid=(nt // MIX_TILE,),
        in_specs=[
            pl.BlockSpec((MIX_TILE, GDN_WIDTH), pm),
            pl.BlockSpec((MIX_TILE, SWA_WIDTH), pm),
            pl.BlockSpec((MIX_TILE, D_MODEL), pm),
            pl.BlockSpec((MIX_TILE, GDN_WIDTH), sm),
            pl.BlockSpec((MIX_TILE, SWA_WIDTH), sm),
            pl.BlockSpec((MIX_TILE, D_MODEL), sm),
            pl.BlockSpec((D_MODEL, D_MODEL), const),
            pl.BlockSpec((1, D_MODEL), const),
            pl.BlockSpec((1, D_MODEL), const),
            pl.BlockSpec((N_EXPERTS, D_MODEL), const),
            pl.BlockSpec((N_EXPERTS, 1), const),
        ],
        out_specs=(
            pl.BlockSpec((MIX_TILE, D_MODEL), lambda i: (i, 0)),
            pl.BlockSpec((TOP_K, MIX_TILE), col),
            pl.BlockSpec((TOP_K, MIX_TILE), col),
            pl.BlockSpec((TOP_K, MIX_TILE), col),
            pl.BlockSpec((N_EXPERTS, 1), const),
        ),
        out_shape=(
            jax.ShapeDtypeStruct((nt, D_MODEL), F32),
            jax.ShapeDtypeStruct((TOP_K, nt), jnp.int32),
            jax.ShapeDtypeStruct((TOP_K, nt), F32),
            jax.ShapeDtypeStruct((TOP_K, nt), jnp.int32),
            jax.ShapeDtypeStruct((N_EXPERTS, 1), jnp.int32),
        ),
        scratch_shapes=[pltpu.VMEM((N_EXPERTS, 1), F32)],
        compiler_params=pltpu.CompilerParams(
            dimension_semantics=("arbitrary",), vmem_limit_bytes=VMEM_LIMIT),
        name="mix",
    )(gd_p, sw_p, x_p, gd_s, sw_s, x_s, wo, g1, b1, wrt, rbias)


def _dest_kernel(tope_ref, rank_ref, pstart_ref, dest_ref):
    tm = tope_ref.shape[1]
    r_e = lax.broadcasted_iota(jnp.int32, (N_EXPERTS, tm), 0)
    pstart = pstart_ref[...]
    top_e = tope_ref[...]
    rows = [jnp.sum(jnp.where(r_e == top_e[j:j + 1, :], pstart, 0), 0, keepdims=True)
            for j in range(TOP_K)]
    dest_ref[...] = jnp.concatenate(rows, 0) + rank_ref[...]


def _dest(top_e_t, rank_t, pstart):
    nt = top_e_t.shape[1]
    col = lambda i: (0, i)
    return pl.pallas_call(
        _dest_kernel,
        grid=(nt // DEST_TILE,),
        in_specs=[
            pl.BlockSpec((TOP_K, DEST_TILE), col),
            pl.BlockSpec((TOP_K, DEST_TILE), col),
            pl.BlockSpec((N_EXPERTS, 1), lambda i: (0, 0)),
        ],
        out_specs=pl.BlockSpec((TOP_K, DEST_TILE), col),
        out_shape=jax.ShapeDtypeStruct((TOP_K, nt), jnp.int32),
        compiler_params=pltpu.CompilerParams(dimension_semantics=("arbitrary",)),
        name="dest",
    )(top_e_t, rank_t, pstart)


def _dispatch_kernel(dest_ref, x_ref, xg_init_ref, xg_ref, sem):
    del xg_init_ref

    def row_copy(t, j):
        return pltpu.make_async_copy(x_ref.at[pl.ds(t, 1), :],
                                     xg_ref.at[pl.ds(dest_ref[j, t], 1), :], sem)

    def issue(t, carry):
        for j in range(TOP_K):
            row_copy(t, j).start()
        return carry

    lax.fori_loop(0, MIX_TILE, issue, 0)
    for _ in range(TOP_K):
        pltpu.make_async_copy(x_ref, xg_ref.at[pl.ds(0, MIX_TILE), :], sem).wait()


def _dispatch(dest, x1, n_rows):
    nt, d = x1.shape
    return pl.pallas_call(
        _dispatch_kernel,
        grid=(nt // MIX_TILE,),
        in_specs=[
            pl.BlockSpec((TOP_K, MIX_TILE), lambda i: (0, i), memory_space=pltpu.SMEM),
            pl.BlockSpec((MIX_TILE, d), lambda i: (i, 0)),
            pl.BlockSpec(memory_space=pl.ANY),
        ],
        out_specs=pl.BlockSpec(memory_space=pl.ANY),
        out_shape=jax.ShapeDtypeStruct((n_rows, d), F32),
        scratch_shapes=[pltpu.SemaphoreType.DMA(())],
        input_output_aliases={2: 0},
        compiler_params=pltpu.CompilerParams(
            dimension_semantics=("arbitrary",), vmem_limit_bytes=VMEM_LIMIT),
        name="dispatch",
    )(dest, x1, jnp.zeros((n_rows, d), F32))


def _experts_kernel(be_ref, nused_ref, xg_ref, wg_ref, wu_ref, wd_ref, y_ref,
                    wg_b, wu_b, wd_b):
    i = pl.program_id(0)
    prev = be_ref[jnp.maximum(i - 1, 0)]

    @pl.when((i == 0) | (be_ref[i] != prev))
    def _():
        wg_b[...] = wg_ref[0].astype(BF16)
        wu_b[...] = wu_ref[0].astype(BF16)
        wd_b[...] = wd_ref[0].astype(BF16)

    @pl.when(i < nused_ref[0])
    def _():
        xb = xg_ref[...].astype(BF16)
        hb = _silu(_dot(xb, wg_b[...])) * _dot(xb, wu_b[...])
        y_ref[...] = _dot(hb.astype(BF16), wd_b[...])

    @pl.when(i >= nused_ref[0])
    def _():
        y_ref[...] = jnp.zeros_like(y_ref)


def _experts(block_e, n_used, xg, w_gate_e, w_up_e, w_down_e):
    nb = block_e.shape[0]
    grid_spec = pltpu.PrefetchScalarGridSpec(
        num_scalar_prefetch=2,
        grid=(nb,),
        in_specs=[
            pl.BlockSpec((EXPERT_BLOCK, D_MODEL), lambda i, be, nu: (i, 0)),
            pl.BlockSpec((1, D_MODEL, EXPERT_FF), lambda i, be, nu: (be[i], 0, 0)),
            pl.BlockSpec((1, D_MODEL, EXPERT_FF), lambda i, be, nu: (be[i], 0, 0)),
            pl.BlockSpec((1, EXPERT_FF, D_MODEL), lambda i, be, nu: (be[i], 0, 0)),
        ],
        out_specs=pl.BlockSpec((EXPERT_BLOCK, D_MODEL), lambda i, be, nu: (i, 0)),
        scratch_shapes=[
            pltpu.VMEM((D_MODEL, EXPERT_FF), BF16),
            pltpu.VMEM((D_MODEL, EXPERT_FF), BF16),
            pltpu.VMEM((EXPERT_FF, D_MODEL), BF16),
        ],
    )
    return pl.pallas_call(
        _experts_kernel,
        grid_spec=grid_spec,
        out_shape=jax.ShapeDtypeStruct((nb * EXPERT_BLOCK, D_MODEL), F32),
        compiler_params=pltpu.CompilerParams(
            dimension_semantics=("arbitrary",), vmem_limit_bytes=VMEM_LIMIT),
        name="experts",
    )(block_e, n_used, xg, w_gate_e, w_up_e, w_down_e)


def _final_kernel(np_tiles, dest_ref, x1_ref, topw_ref, yb_ref, wsg_ref, wsu_ref, wsd_ref,
                  g2_ref, b2_ref, yp_ref, ys_ref, ybuf, sem):
    i = pl.program_id(0)

    def issue(t, carry):
        for j in range(TOP_K):
            pltpu.make_async_copy(yb_ref.at[pl.ds(dest_ref[j, t], 1), :],
                                  ybuf.at[j, pl.ds(t, 1), :], sem).start()
        return carry

    lax.fori_loop(0, MIX_TILE, issue, 0)

    x1 = x1_ref[...]
    xb = x1.astype(BF16)
    hs = _silu(_dot(xb, wsg_ref[...])) * _dot(xb, wsu_ref[...])
    ff = _dot(hs.astype(BF16), wsd_ref[...])
    r = lax.broadcasted_iota(jnp.int32, (MIX_TILE, MIX_TILE), 0)
    c = lax.broadcasted_iota(jnp.int32, (MIX_TILE, MIX_TILE), 1)
    w_cols = _dot_nt((r == c).astype(F32), topw_ref[...], HIGHEST)
    for j in range(TOP_K):
        pltpu.make_async_copy(yb_ref.at[pl.ds(0, MIX_TILE), :], ybuf.at[j], sem).wait()
    for j in range(TOP_K):
        ff = ff + w_cols[:, j:j + 1] * ybuf[j]
    y = _layer_norm(ALPHA * x1 + ff, g2_ref[...], b2_ref[...])

    @pl.when(i < np_tiles)
    def _():
        yp_ref[...] = y

    @pl.when(i >= np_tiles)
    def _():
        ys_ref[...] = y


def _final(n_p, dest, x1, top_w_t, yb, wsg, wsu, wsd, g2, b2):
    nt = x1.shape[0]
    np_tiles = n_p // MIX_TILE
    row = lambda i: (i, 0)
    const = lambda i: (0, 0)
    return pl.pallas_call(
        functools.partial(_final_kernel, np_tiles),
        grid=(nt // MIX_TILE,),
        in_specs=[
            pl.BlockSpec((TOP_K, MIX_TILE), lambda i: (0, i), memory_space=pltpu.SMEM),
            pl.BlockSpec((MIX_TILE, D_MODEL), row),
            pl.BlockSpec((TOP_K, MIX_TILE), lambda i: (0, i)),
            pl.BlockSpec(memory_space=pl.ANY),
            pl.BlockSpec((D_MODEL, EXPERT_FF), const),
            pl.BlockSpec((D_MODEL, EXPERT_FF), const),
            pl.BlockSpec((EXPERT_FF, D_MODEL), const),
            pl.BlockSpec((1, D_MODEL), const),
            pl.BlockSpec((1, D_MODEL), const),
        ],
        out_specs=(
            pl.BlockSpec((MIX_TILE, D_MODEL), lambda i: (jnp.minimum(i, np_tiles - 1), 0)),
            pl.BlockSpec((MIX_TILE, D_MODEL), lambda i: (jnp.maximum(i - np_tiles, 0), 0)),
        ),
        out_shape=(
            jax.ShapeDtypeStruct((n_p, D_MODEL), F32),
            jax.ShapeDtypeStruct((nt - n_p, D_MODEL), F32),
        ),
        scratch_shapes=[
            pltpu.VMEM((TOP_K, MIX_TILE, D_MODEL), F32),
            pltpu.SemaphoreType.DMA(()),
        ],
        compiler_params=pltpu.CompilerParams(
            dimension_semantics=("arbitrary",), vmem_limit_bytes=VMEM_LIMIT),
        name="final",
    )(dest, x1, top_w_t, yb, wsg, wsu, wsd, g2, b2)


def _expert_layout(counts, nk):
    nb = -(-nk // EXPERT_BLOCK) + N_EXPERTS
    padded = (counts + EXPERT_BLOCK - 1) // EXPERT_BLOCK * EXPERT_BLOCK
    pend = jnp.cumsum(padded)
    pstart = (pend - padded).astype(jnp.int32)
    block_start = jnp.arange(nb, dtype=jnp.int32) * EXPERT_BLOCK
    block_e = jnp.sum((pend[None, :] <= block_start[:, None]).astype(jnp.int32), 1)
    block_e = jnp.minimum(block_e, N_EXPERTS - 1)
    n_used = (pend[-1] // EXPERT_BLOCK).astype(jnp.int32).reshape(1)
    return pstart, block_e, n_used, nb


def _pad_lanes(v, offset=0):
    out = jnp.zeros((1, LANES), F32)
    return out.at[0, offset:offset + v.shape[0]].set(v.astype(F32))


def kernel(x_prompt, x_sample, state_conv, state_gdn, cache_swa_k, cache_swa_v, w_in, conv_w, a_log, dt_bias, gdn_norm_w, attn_sinks, w_o, ln1_g, ln1_b, w_router, router_bias, w_gate_e, w_up_e, w_down_e, w_shared_gate, w_shared_up, w_shared_down, ln2_g, ln2_b):
    bp, tp, d = x_prompt.shape
    bs, ts, _ = x_sample.shape
    n_p, n_s = bp * tp, bs * ts

    wi = w_in[0]
    o1 = CONV_DIM
    o2 = o1 + GDN_WIDTH
    o4 = o2 + 2 * GDN_HEADS
    o5 = o4 + SWA_WIDTH
    o6 = o5 + SWA_KV_WIDTH
    ba_cols = jnp.zeros((d, LANES), F32).at[:, :2 * GDN_HEADS].set(wi[:, o2:o4])
    w_all = jnp.concatenate([wi[:, :o2], wi[:, o4:o5], wi[:, o5:o6], wi[:, o6:], ba_cols], 1).astype(BF16)
    wo = w_o[0].astype(BF16)
    wrt = w_router[0].T.astype(BF16)
    rbias = router_bias[0].astype(F32).reshape(N_EXPERTS, 1)
    alog_pad = _pad_lanes(a_log[0], GDN_HEADS)
    dtb_pad = _pad_lanes(dt_bias[0], GDN_HEADS)
    sinks_pad = _pad_lanes(attn_sinks[0])
    norm_w = gdn_norm_w[0].reshape(1, GDN_D)
    g1, b1 = ln1_g[0].reshape(1, d), ln1_b[0].reshape(1, d)
    g2, b2 = ln2_g[0].reshape(1, d), ln2_b[0].reshape(1, d)

    def front(x, pos0, L, conv_hist, s0, k_hist, v_hist):
        b, t, _ = x.shape
        period = max(t, IN_TILE)
        pos = pos0 + (jnp.arange(period, dtype=jnp.int32) % t).astype(F32)
        cos, s1, s2 = _rope_tables(pos)
        x2d = x.reshape(b * t, d)
        qkv, z, sq, sk, sv, ba = _in_proj(x2d, w_all, cos, s1, s2)
        qkv = qkv.reshape(b, t, CONV_DIM)
        hist8 = jnp.concatenate(
            [jnp.zeros((b, HIST_ROWS - (CONV_WIDTH - 1), CONV_DIM), F32), conv_hist], 1)
        gd, s_new = _gdn(L, qkv, z.reshape(b, t, GDN_WIDTH), ba.reshape(b, t, LANES), hist8, s0,
                         conv_w[0], alog_pad, dtb_pad, norm_w)
        k_ext = jnp.concatenate([k_hist.reshape(b, WINDOW, SWA_KV_WIDTH),
                                 sk.reshape(b, t, SWA_KV_WIDTH)], 1)
        v_ext = jnp.concatenate([v_hist.reshape(b, WINDOW, SWA_KV_WIDTH),
                                 sv.reshape(b, t, SWA_KV_WIDTH)], 1)
        sw = _swa(L, pos0, sq.reshape(b, t, SWA_WIDTH), k_ext, v_ext, sinks_pad)
        conv_new = jnp.concatenate([conv_hist, qkv], 1)[:, -(CONV_WIDTH - 1):]
        k_new = k_ext[:, -WINDOW:].reshape(b, WINDOW, SWA_KV_HEADS, SWA_HEAD_DIM)
        v_new = v_ext[:, -WINDOW:].reshape(b, WINDOW, SWA_KV_HEADS, SWA_HEAD_DIM)
        return (x2d, gd.reshape(b * t, GDN_WIDTH), sw.reshape(b * t, SWA_WIDTH),
                conv_new, s_new, k_new, v_new)

    zeros = lambda *s: jnp.zeros(s, F32)
    xp2, gd_p, sw_p, conv_p, gdn_p, k_p, v_p = front(
        x_prompt, 0, CHUNK, zeros(bp, CONV_WIDTH - 1, CONV_DIM),
        zeros(bp, GDN_HEADS, GDN_D, GDN_D), zeros(bp, WINDOW, SWA_KV_WIDTH),
        zeros(bp, WINDOW, SWA_KV_WIDTH))
    xs2, gd_s, sw_s, conv_s, gdn_s, k_s, v_s = front(
        x_sample, PAST_LEN, ts, state_conv[0], state_gdn[0], cache_swa_k[0], cache_swa_v[0])

    x1, top_e_t, top_w_t, rank_t, counts = _mix(
        gd_p, sw_p, xp2, gd_s, sw_s, xs2, wo, g1, b1, wrt, rbias)

    pstart, block_e, n_used, nb = _expert_layout(counts[:, 0], TOP_K * (n_p + n_s))
    dest = _dest(top_e_t, rank_t, pstart.reshape(N_EXPERTS, 1))
    xg = _dispatch(dest, x1, nb * EXPERT_BLOCK)
    yb = _experts(block_e, n_used, xg, w_gate_e[0], w_up_e[0], w_down_e[0])

    y_p, y_s = _final(n_p, dest, x1, top_w_t, yb,
                      w_shared_gate[0].astype(BF16), w_shared_up[0].astype(BF16),
                      w_shared_down[0].astype(BF16), g2, b2)
    return (y_p.reshape(bp, tp, d), y_s.reshape(bs, ts, d),
            conv_p[None], gdn_p[None], k_p[None], v_p[None],
            conv_s[None], gdn_s[None], k_s[None], v_s[None])
```

```python
import functools
import math

import jax
import jax.numpy as jnp
from jax import lax
from jax.experimental import pallas as pl
from jax.experimental.pallas import tpu as pltpu

F32 = jnp.float32
BF16 = jnp.bfloat16
HIGHEST = lax.Precision.HIGHEST

D_MODEL = 1024
CHUNK = 64
GDN_HEADS = 4
GDN_D = 128
GDN_WIDTH = GDN_HEADS * GDN_D
CONV_DIM = 3 * GDN_WIDTH
CONV_WIDTH = 4
SWA_HEAD_DIM = 64
SWA_Q_HEADS = 8
SWA_KV_HEADS = 2
SWA_GROUP = SWA_Q_HEADS // SWA_KV_HEADS
SWA_WIDTH = SWA_Q_HEADS * SWA_HEAD_DIM
SWA_KV_WIDTH = SWA_KV_HEADS * SWA_HEAD_DIM
WINDOW = 128
ROPE_THETA = 500000.0
ROT_DIM = SWA_HEAD_DIM // 4
ROT_HALF = ROT_DIM // 2
N_EXPERTS = 256
TOP_K = 8
N_GROUPS = 8
GROUP_SIZE = N_EXPERTS // N_GROUPS
TOPK_GROUPS = 4
EXPERT_FF = 256
ROUTED_SCALE = 2.5
ALPHA = 2.0 ** 0.25
LN_EPS = 1e-5
NORM_EPS = 1e-6
PAST_LEN = 4096

LANES = 128
HIST_ROWS = 8

C_QKV = 0
C_Z = C_QKV + CONV_DIM
C_SQ = C_Z + GDN_WIDTH
C_SK = C_SQ + SWA_WIDTH
C_SV = C_SK + SWA_KV_WIDTH
C_BA = C_SV + SWA_KV_WIDTH
C_END = C_BA + LANES

IN_TILE = 512
MIX_TILE = 256
DEST_TILE = 512
EXPERT_BLOCK = 256
IDX_WINDOW = EXPERT_BLOCK + LANES
GDN_BATCH = 2
VMEM_LIMIT = 56 * 1024 * 1024


def _dot(a, b, precision=None):
    return jnp.dot(a, b, preferred_element_type=F32, precision=precision)


def _dot_nt(a, b, precision=None):
    return lax.dot_general(a, b, (((1,), (1,)), ((), ())),
                           preferred_element_type=F32, precision=precision)


def _dot_tn(a, b):
    return lax.dot_general(a, b, (((0,), (0,)), ((), ())), preferred_element_type=F32)


def _split(x):
    hi = x.astype(BF16)
    return hi, (x - hi.astype(F32)).astype(BF16)


def _dot_split(a, b):
    return _dot(a[0], b[0]) + (_dot(a[1], b[0]) + _dot(a[0], b[1]))


def _silu(x):
    return x * jax.nn.sigmoid(x)


def _in_proj_kernel(x_ref, w_ref, cos_ref, s1_ref, s2_ref,
                    qkv_ref, z_ref, sq_ref, sk_ref, sv_ref, ba_ref):
    xb = x_ref[...].astype(BF16)
    qkv_ref[...] = _dot(xb, w_ref[:, C_QKV:C_Z])
    z_ref[...] = _dot(xb, w_ref[:, C_Z:C_SQ])
    sv_ref[...] = _dot(xb, w_ref[:, C_SV:C_BA])
    ba_ref[...] = _dot(xb, w_ref[:, C_BA:C_END])
    cos = cos_ref[...]
    s1 = s1_ref[...]
    s2 = s2_ref[...]

    def rope(t):
        return (t * cos + pltpu.roll(t, LANES - ROT_HALF, 1) * s1
                + pltpu.roll(t, ROT_HALF, 1) * s2)

    sk_ref[...] = rope(_dot(xb, w_ref[:, C_SK:C_SV]))
    for g in range(SWA_WIDTH // LANES):
        lo = C_SQ + g * LANES
        sq_ref[:, g * LANES:(g + 1) * LANES] = rope(_dot(xb, w_ref[:, lo:lo + LANES])).astype(BF16)


def _in_proj(x2d, w_all, cos, s1, s2):
    n = x2d.shape[0]
    period_tiles = cos.shape[0] // IN_TILE
    row = lambda i: (i, 0)
    tab = lambda i: (i % period_tiles, 0)
    out_shapes = (
        jax.ShapeDtypeStruct((n, CONV_DIM), F32),
        jax.ShapeDtypeStruct((n, GDN_WIDTH), F32),
        jax.ShapeDtypeStruct((n, SWA_WIDTH), BF16),
        jax.ShapeDtypeStruct((n, SWA_KV_WIDTH), F32),
        jax.ShapeDtypeStruct((n, SWA_KV_WIDTH), F32),
        jax.ShapeDtypeStruct((n, LANES), F32),
    )
    return pl.pallas_call(
        _in_proj_kernel,
        grid=(n // IN_TILE,),
        in_specs=[
            pl.BlockSpec((IN_TILE, D_MODEL), row),
            pl.BlockSpec((D_MODEL, C_END), lambda i: (0, 0)),
            pl.BlockSpec((IN_TILE, LANES), tab),
            pl.BlockSpec((IN_TILE, LANES), tab),
            pl.BlockSpec((IN_TILE, LANES), tab),
        ],
        out_specs=(
            pl.BlockSpec((IN_TILE, CONV_DIM), row),
            pl.BlockSpec((IN_TILE, GDN_WIDTH), row),
            pl.BlockSpec((IN_TILE, SWA_WIDTH), row),
            pl.BlockSpec((IN_TILE, SWA_KV_WIDTH), row),
            pl.BlockSpec((IN_TILE, SWA_KV_WIDTH), row),
            pl.BlockSpec((IN_TILE, LANES), row),
        ),
        out_shape=out_shapes,
        compiler_params=pltpu.CompilerParams(
            dimension_semantics=("arbitrary",), vmem_limit_bytes=VMEM_LIMIT),
        name="in_proj",
    )(x2d, w_all, cos, s1, s2)


def _rope_tables(pos):
    p = pos.shape[0]
    inv = ROPE_THETA ** (-jnp.arange(0, ROT_DIM, 2, dtype=F32) / ROT_DIM)
    ang = pos[:, None] * inv[None, :]
    cos = jnp.cos(ang)
    sin = jnp.sin(ang)
    rest = SWA_HEAD_DIM - ROT_DIM
    head_c = jnp.concatenate([cos, cos, jnp.ones((p, rest), F32)], 1)
    head_s1 = jnp.concatenate([-sin, jnp.zeros((p, SWA_HEAD_DIM - ROT_HALF), F32)], 1)
    head_s2 = jnp.concatenate([jnp.zeros((p, ROT_HALF), F32), sin, jnp.zeros((p, rest), F32)], 1)
    two = lambda t: jnp.concatenate([t, t], 1)
    return two(head_c), two(head_s1), two(head_s2)


def _gdn_kernel(L, qkv_ref, z_ref, ba_ref, hist_ref, s0_ref, convw_ref, alog_ref, dtb_ref,
                normw_ref, o_ref, sout_ref, xbuf, state):
    c = pl.program_id(1)
    nbat = qkv_ref.shape[0]
    chains = [(b, h) for b in range(nbat) for h in range(GDN_HEADS)]

    @pl.when(c == 0)
    def _():
        xbuf[:, 0:HIST_ROWS, :] = hist_ref[...]
        state[...] = s0_ref[...]

    xbuf[:, HIST_ROWS:HIST_ROWS + L, :] = qkv_ref[...]

    row = lax.broadcasted_iota(jnp.int32, (L, L), 0)
    col = lax.broadcasted_iota(jnp.int32, (L, L), 1)
    incl = row >= col
    strict = row > col
    lower_ones = incl.astype(BF16)
    eye = (row == col).astype(F32)

    beta_all, g_all = [], []
    for b in range(nbat):
        ba = ba_ref[b]
        beta_all.append(jax.nn.sigmoid(ba))
        sp_in = ba + dtb_ref[...]
        softplus = jnp.maximum(sp_in, 0.0) + jnp.log1p(jnp.exp(-jnp.abs(sp_in)))
        g_all.append(-jnp.exp(alog_ref[...]) * softplus)

    def conv_silu(b, c0):
        acc = None
        for j in range(CONV_WIDTH):
            lo = HIST_ROWS - (CONV_WIDTH - 1) + j
            t = xbuf[b, lo:lo + L, c0:c0 + GDN_D] * convw_ref[j:j + 1, c0:c0 + GDN_D]
            acc = t if acc is None else acc + t
        return _silu(acc)

    def l2n(t):
        return t * lax.rsqrt(jnp.sum(t * t, -1, keepdims=True) + NORM_EPS)

    q = [l2n(conv_silu(b, h * GDN_D)) * (GDN_D ** -0.5) for b, h in chains]
    k = [l2n(conv_silu(b, GDN_WIDTH + h * GDN_D)) for b, h in chains]
    v = [conv_silu(b, 2 * GDN_WIDTH + h * GDN_D) for b, h in chains]
    beta = [beta_all[b][:, h:h + 1] for b, h in chains]
    g = [g_all[b][:, GDN_HEADS + h:GDN_HEADS + h + 1] for b, h in chains]

    def pieces(t):
        p1 = t.astype(BF16).astype(F32)
        r1 = t - p1
        p2 = r1.astype(BF16).astype(F32)
        return p1, p2, r1 - p2

    g_parts = [pieces(t) for t in g]
    diff = [sum(_dot(lower_ones, jnp.where(strict, p, 0.0).astype(BF16)) for p in gp)
            for gp in g_parts]
    gcum = [sum(_dot(lower_ones, jnp.broadcast_to(p, (L, GDN_D)).astype(BF16)) for p in gp)
            for gp in g_parts]
    dec = [jnp.where(incl, jnp.exp(jnp.where(incl, d, 0.0)), 0.0) for d in diff]
    exp_g = [jnp.exp(t) for t in gcum]
    g_end = [t[L - 1:L, :] for t in gcum]
    kb = [t.astype(BF16) for t in k]
    kk = [_dot_nt(t, t) for t in kb]
    aqk = [_dot_nt(qq.astype(BF16), t) * d for qq, t, d in zip(q, kb, dec)]
    a = [jnp.where(strict, d * t, 0.0) * bt for d, t, bt in zip(dec, kk, beta)]

    inv = [eye - t for t in a]
    pw = [_split(-t) for t in a]
    for _ in range(int(math.log2(L)) - 1):
        pw = [_split(_dot_split(p, p)) for p in pw]
        inv = [t + _dot_split(_split(t), p) for t, p in zip(inv, pw)]
    inv_s = [_split(t) for t in inv]
    w = [_dot_split(t, _split(bt * vv)) for t, bt, vv in zip(inv_s, beta, v)]
    y = [_dot_split(t, _split((bt * eg) * kk_)) for t, bt, eg, kk_ in zip(inv_s, beta, exp_g, k)]

    s = [state[b, h] for b, h in chains]
    sb = [t.astype(BF16) for t in s]
    u = [ww - _dot(yy.astype(BF16), t) for ww, yy, t in zip(w, y, sb)]
    ub = [t.astype(BF16) for t in u]
    o = [_dot((eg * qq).astype(BF16), t) + _dot(aa.astype(BF16), uu)
         for eg, qq, t, aa, uu in zip(exp_g, q, sb, aqk, ub)]
    k_dec = [(jnp.exp(ge - gc) * kk_).astype(BF16) for ge, gc, kk_ in zip(g_end, gcum, k)]
    s_new = [jnp.exp(ge) * t + _dot_tn(kd, uu) for ge, t, kd, uu in zip(g_end, s, k_dec, ub)]

    for (b, h), t, oo in zip(chains, s_new, o):
        state[b, h] = t
        on = oo * lax.rsqrt(jnp.mean(oo * oo, -1, keepdims=True) + NORM_EPS)
        zz = z_ref[b, :, h * GDN_D:(h + 1) * GDN_D]
        o_ref[b, :, h * GDN_D:(h + 1) * GDN_D] = (on * normw_ref[...] * _silu(zz)).astype(BF16)

    xbuf[:, 0:HIST_ROWS, :] = xbuf[:, L:L + HIST_ROWS, :]

    @pl.when(c == pl.num_programs(1) - 1)
    def _():
        sout_ref[...] = state[...]


def _gdn(L, qkv, z, ba, hist8, s0, conv_w, alog_pad, dtb_pad, norm_w):
    b, t, _ = qkv.shape
    nc = t // L
    nbat = GDN_BATCH
    tok = lambda i, c: (i, c, 0)
    per_b3 = lambda i, c: (i, 0, 0)
    per_b4 = lambda i, c: (i, 0, 0, 0)
    const2 = lambda i, c: (0, 0)
    return pl.pallas_call(
        functools.partial(_gdn_kernel, L),
        grid=(b // nbat, nc),
        in_specs=[
            pl.BlockSpec((nbat, L, CONV_DIM), tok),
            pl.BlockSpec((nbat, L, GDN_WIDTH), tok),
            pl.BlockSpec((nbat, L, LANES), tok),
            pl.BlockSpec((nbat, HIST_ROWS, CONV_DIM), per_b3),
            pl.BlockSpec((nbat, GDN_HEADS, GDN_D, GDN_D), per_b4),
            pl.BlockSpec((CONV_WIDTH, CONV_DIM), const2),
            pl.BlockSpec((1, LANES), const2),
            pl.BlockSpec((1, LANES), const2),
            pl.BlockSpec((1, GDN_D), const2),
        ],
        out_specs=(
            pl.BlockSpec((nbat, L, GDN_WIDTH), tok),
            pl.BlockSpec((nbat, GDN_HEADS, GDN_D, GDN_D), per_b4),
        ),
        out_shape=(
            jax.ShapeDtypeStruct((b, t, GDN_WIDTH), BF16),
            jax.ShapeDtypeStruct((b, GDN_HEADS, GDN_D, GDN_D), F32),
        ),
        scratch_shapes=[
            pltpu.VMEM((nbat, HIST_ROWS + L, CONV_DIM), F32),
            pltpu.VMEM((nbat, GDN_HEADS, GDN_D, GDN_D), F32),
        ],
        compiler_params=pltpu.CompilerParams(
            dimension_semantics=("arbitrary", "arbitrary"), vmem_limit_bytes=VMEM_LIMIT),
        name="gdn",
    )(qkv, z, ba, hist8, s0, conv_w, alog_pad, dtb_pad, norm_w)


def _swa_kernel(L, pos0, q_ref, k_ref, v_ref, sink_ref, o_ref):
    c = pl.program_id(1)
    span = WINDOW + L
    start = pl.multiple_of(c * L, L)
    kx = k_ref[0, pl.ds(start, span), :]
    vx = v_ref[0, pl.ds(start, span), :]
    kx_sw = pltpu.roll(kx, SWA_HEAD_DIM, 1)
    vx_sw = pltpu.roll(vx, SWA_HEAD_DIM, 1)
    low = lax.broadcasted_iota(jnp.int32, (span, LANES), 1) < SWA_HEAD_DIM

    def halves(x, x_sw, kh):
        src_lo, src_hi = (x, x_sw) if kh == 0 else (x_sw, x)
        return (jnp.where(low, src_lo, 0.0).astype(BF16), jnp.where(low, 0.0, src_hi).astype(BF16))

    key_pos = pos0 - WINDOW + c * L + lax.broadcasted_iota(jnp.int32, (1, span), 1)
    valid = key_pos >= 0
    sinks = sink_ref[...]
    top_rows = lax.broadcasted_iota(jnp.int32, (2 * L, 1), 0) < L

    scores, sink_cols, v_halves = [], [], []
    for kh in range(SWA_KV_HEADS):
        c0 = kh * 2 * LANES
        q4 = jnp.concatenate([q_ref[0, :, c0:c0 + LANES], q_ref[0, :, c0 + LANES:c0 + 2 * LANES]], 0)
        k_halves = halves(kx, kx_sw, kh)
        v_halves.append(halves(vx, vx_sw, kh))
        for half in range(2):
            h_top = kh * SWA_GROUP + half
            s = _dot_nt(q4, k_halves[half]) * (SWA_HEAD_DIM ** -0.5)
            scores.append(jnp.where(valid, s, -jnp.inf))
            sink_cols.append(jnp.where(top_rows, sinks[:, h_top:h_top + 1],
                                       sinks[:, h_top + 2:h_top + 3]))
    m = [jnp.maximum(jnp.max(s, -1, keepdims=True), sk) for s, sk in zip(scores, sink_cols)]
    p = [jnp.exp(s - mm) for s, mm in zip(scores, m)]
    den = [jnp.sum(pp, -1, keepdims=True) + jnp.exp(sk - mm) for pp, sk, mm in zip(p, sink_cols, m)]
    pb = [pp.astype(BF16) for pp in p]
    for kh in range(SWA_KV_HEADS):
        va, vb = v_halves[kh]
        o = _dot(pb[2 * kh], va) / den[2 * kh] + _dot(pb[2 * kh + 1], vb) / den[2 * kh + 1]
        c0 = kh * 2 * LANES
        o_ref[0, :, c0:c0 + LANES] = o[0:L].astype(BF16)
        o_ref[0, :, c0 + LANES:c0 + 2 * LANES] = o[L:2 * L].astype(BF16)


def _swa(L, pos0, q, k_ext, v_ext, sinks_pad):
    b, t, _ = q.shape
    ext = k_ext.shape[1]
    return pl.pallas_call(
        functools.partial(_swa_kernel, L, pos0),
        grid=(b, t // L),
        in_specs=[
            pl.BlockSpec((1, L, SWA_WIDTH), lambda i, c: (i, c, 0)),
            pl.BlockSpec((1, ext, SWA_KV_WIDTH), lambda i, c: (i, 0, 0)),
            pl.BlockSpec((1, ext, SWA_KV_WIDTH), lambda i, c: (i, 0, 0)),
            pl.BlockSpec((1, LANES), lambda i, c: (0, 0)),
        ],
        out_specs=pl.BlockSpec((1, L, SWA_WIDTH), lambda i, c: (i, c, 0)),
        out_shape=jax.ShapeDtypeStruct((b, t, SWA_WIDTH), BF16),
        compiler_params=pltpu.CompilerParams(
            dimension_semantics=("arbitrary", "arbitrary"), vmem_limit_bytes=VMEM_LIMIT),
        name="swa",
    )(q, k_ext, v_ext, sinks_pad)


def _layer_norm(r, g, b):
    mu = jnp.mean(r, -1, keepdims=True)
    d = r - mu
    var = jnp.mean(d * d, -1, keepdims=True)
    return d * lax.rsqrt(var + LN_EPS) * g + b


def _route(x1b, wrt_ref, rbias_ref, count_ref):
    tm = x1b.shape[0]
    scores = jax.nn.sigmoid(_dot_nt(wrt_ref[...], x1b))
    biased = scores + rbias_ref[...]
    neg = -jnp.inf
    r_g = lax.broadcasted_iota(jnp.int32, (GROUP_SIZE, tm), 0)
    blocks = []
    group_score = []
    for g in range(N_GROUPS):
        blk = biased[g * GROUP_SIZE:(g + 1) * GROUP_SIZE]
        m1 = jnp.max(blk, 0, keepdims=True)
        i1 = jnp.min(jnp.where(blk == m1, r_g, GROUP_SIZE), 0, keepdims=True)
        m2 = jnp.max(jnp.where(r_g == i1, neg, blk), 0, keepdims=True)
        blocks.append(blk)
        group_score.append(m1 + m2)
    masked = []
    for g in range(N_GROUPS):
        rank = jnp.zeros((1, tm), jnp.int32)
        for o in range(N_GROUPS):
            if o == g:
                continue
            ahead = group_score[o] > group_score[g]
            if o < g:
                ahead = ahead | (group_score[o] == group_score[g])
            rank = rank + ahead.astype(jnp.int32)
        masked.append(jnp.where(rank < TOPK_GROUPS, blocks[g], neg))
    masked = jnp.concatenate(masked, 0)
    r_e = lax.broadcasted_iota(jnp.int32, (N_EXPERTS, tm), 0)
    idx_rows = []
    w_rows = []
    hits = []
    for _ in range(TOP_K):
        m = jnp.max(masked, 0, keepdims=True)
        idx = jnp.min(jnp.where(masked == m, r_e, N_EXPERTS), 0, keepdims=True)
        hit = r_e == idx
        w_rows.append(jnp.sum(jnp.where(hit, scores, 0.0), 0, keepdims=True))
        masked = jnp.where(hit, neg, masked)
        idx_rows.append(idx)
        hits.append(hit)
    total = w_rows[0]
    for wr in w_rows[1:]:
        total = total + wr
    top_w = jnp.concatenate(w_rows, 0) / total * ROUTED_SCALE

    chosen = jnp.zeros((N_EXPERTS, tm), F32)
    for hit in hits:
        chosen = chosen + hit.astype(F32)
    t_r = lax.broadcasted_iota(jnp.int32, (tm, tm), 0)
    t_c = lax.broadcasted_iota(jnp.int32, (tm, tm), 1)
    before = (t_r < t_c).astype(BF16)
    prior = _dot(chosen.astype(BF16), before) + count_ref[...]
    rank_rows = [jnp.sum(jnp.where(hit, prior, 0.0), 0, keepdims=True) for hit in hits]
    count_ref[...] = count_ref[...] + jnp.sum(chosen, 1, keepdims=True)
    rank = jnp.concatenate(rank_rows, 0).astype(jnp.int32)
    return jnp.concatenate(idx_rows, 0), top_w, rank


def _mix_kernel(np_tiles, gp_ref, sp_ref, xp_ref, gs_ref, ss_ref, xs_ref, wo_ref, g1_ref, b1_ref,
                wrt_ref, rbias_ref, x1_ref, tope_ref, topw_ref, rank_ref, cnt_ref, count_acc):
    i = pl.program_id(0)

    @pl.when(i == 0)
    def _():
        count_acc[...] = jnp.zeros_like(count_acc)

    is_prompt = i < np_tiles
    gd = jnp.where(is_prompt, gp_ref[...], gs_ref[...])
    sw = jnp.where(is_prompt, sp_ref[...], ss_ref[...])
    x = jnp.where(is_prompt, xp_ref[...], xs_ref[...])
    mix = _dot(gd, wo_ref[0:GDN_WIDTH, :]) + _dot(sw, wo_ref[GDN_WIDTH:, :])
    x1 = _layer_norm(ALPHA * x + mix, g1_ref[...], b1_ref[...])
    x1_ref[...] = x1
    top_e, top_w, rank = _route(x1.astype(BF16), wrt_ref, rbias_ref, count_acc)
    tope_ref[...] = top_e
    topw_ref[...] = top_w
    rank_ref[...] = rank
    cnt_ref[...] = count_acc[...].astype(jnp.int32)


def _mix(gd_p, sw_p, x_p, gd_s, sw_s, x_s, wo, g1, b1, wrt, rbias):
    n_p, n_s = x_p.shape[0], x_s.shape[0]
    np_tiles = n_p // MIX_TILE
    nt = n_p + n_s
    pm = lambda i: (jnp.minimum(i, np_tiles - 1), 0)
    sm = lambda i: (jnp.maximum(i - np_tiles, 0), 0)
    const = lambda i: (0, 0)
    col = lambda i: (0, i)
    return pl.pallas_call(
        functools.partial(_mix_kernel, np_tiles),
        grid=(nt // MIX_TILE,),
        in_specs=[
            pl.BlockSpec((MIX_TILE, GDN_WIDTH), pm),
            pl.BlockSpec((MIX_TILE, SWA_WIDTH), pm),
            pl.BlockSpec((MIX_TILE, D_MODEL), pm),
            pl.BlockSpec((MIX_TILE, GDN_WIDTH), sm),
            pl.BlockSpec((MIX_TILE, SWA_WIDTH), sm),
            pl.BlockSpec((MIX_TILE, D_MODEL), sm),
            pl.BlockSpec((D_MODEL, D_MODEL), const),
            pl.BlockSpec((1, D_MODEL), const),
            pl.BlockSpec((1, D_MODEL), const),
            pl.BlockSpec((N_EXPERTS, D_MODEL), const),
            pl.BlockSpec((N_EXPERTS, 1), const),
        ],
        out_specs=(
            pl.BlockSpec((MIX_TILE, D_MODEL), lambda i: (i, 0)),
            pl.BlockSpec((TOP_K, MIX_TILE), col),
            pl.BlockSpec((TOP_K, MIX_TILE), col),
            pl.BlockSpec((TOP_K, MIX_TILE), col),
            pl.BlockSpec((N_EXPERTS, 1), const),
        ),
        out_shape=(
            jax.ShapeDtypeStruct((nt, D_MODEL), F32),
            jax.ShapeDtypeStruct((TOP_K, nt), jnp.int32),
            jax.ShapeDtypeStruct((TOP_K, nt), F32),
            jax.ShapeDtypeStruct((TOP_K, nt), jnp.int32),
            jax.ShapeDtypeStruct((N_EXPERTS, 1), jnp.int32),
        ),
        scratch_shapes=[pltpu.VMEM((N_EXPERTS, 1), F32)],
        compiler_params=pltpu.CompilerParams(
            dimension_semantics=("arbitrary",), vmem_limit_bytes=VMEM_LIMIT),
        name="mix",
    )(gd_p, sw_p, x_p, gd_s, sw_s, x_s, wo, g1, b1, wrt, rbias)


def _dest_kernel(tope_ref, rank_ref, pstart_ref, dest_ref):
    tm = tope_ref.shape[1]
    r_e = lax.broadcasted_iota(jnp.int32, (N_EXPERTS, tm), 0)
    pstart = pstart_ref[...]
    top_e = tope_ref[...]
    rows = [jnp.sum(jnp.where(r_e == top_e[j:j + 1, :], pstart, 0), 0, keepdims=True)
            for j in range(TOP_K)]
    dest_ref[...] = jnp.concatenate(rows, 0) + rank_ref[...]


def _dest(top_e_t, rank_t, pstart):
    nt = top_e_t.shape[1]
    col = lambda i: (0, i)
    return pl.pallas_call(
        _dest_kernel,
        grid=(nt // DEST_TILE,),
        in_specs=[
            pl.BlockSpec((TOP_K, DEST_TILE), col),
            pl.BlockSpec((TOP_K, DEST_TILE), col),
            pl.BlockSpec((N_EXPERTS, 1), lambda i: (0, 0)),
        ],
        out_specs=pl.BlockSpec((TOP_K, DEST_TILE), col),
        out_shape=jax.ShapeDtypeStruct((TOP_K, nt), jnp.int32),
        compiler_params=pltpu.CompilerParams(dimension_semantics=("arbitrary",)),
        name="dest",
    )(top_e_t, rank_t, pstart)


def _experts_kernel(be_ref, nused_ref, c0_ref, stok_ref, x1_ref, wg_ref, wu_ref, wd_ref, y_ref,
                    idx0, idx1, xbuf0, xbuf1, idx_sem, row_sem, wg_b, wu_b, wd_b):
    i = pl.program_id(0)
    n_used = nused_ref[0]
    idx = (idx0, idx1)
    xbuf = (xbuf0, xbuf1)

    def idx_copy(block, s):
        base = pl.multiple_of(c0_ref[block] // LANES * LANES, LANES)
        return pltpu.make_async_copy(stok_ref.at[pl.ds(base, IDX_WINDOW)], idx[s], idx_sem.at[s])

    def start_rows(block, s):
        off = lax.rem(c0_ref[block], LANES)
        for r in range(EXPERT_BLOCK):
            pltpu.make_async_copy(x1_ref.at[pl.ds(idx[s][off + r], 1), :],
                                  xbuf[s].at[pl.ds(r, 1), :], row_sem.at[s]).start()

    def wait_rows(s):
        pltpu.make_async_copy(x1_ref.at[pl.ds(0, EXPERT_BLOCK), :], xbuf[s], row_sem.at[s]).wait()

    @pl.when(i == 0)
    def _():
        idx_copy(0, 0).start()
        idx_copy(0, 0).wait()
        start_rows(0, 0)
        idx_copy(1, 1).start()

    prev = be_ref[jnp.maximum(i - 1, 0)]

    @pl.when((i == 0) | (be_ref[i] != prev))
    def _():
        wg_b[...] = wg_ref[0].astype(BF16)
        wu_b[...] = wu_ref[0].astype(BF16)
        wd_b[...] = wd_ref[0].astype(BF16)

    for slot in range(2):
        other = 1 - slot

        @pl.when((i < n_used) & (lax.rem(i, 2) == slot))
        def _():
            idx_copy(i + 1, other).wait()
            wait_rows(slot)
            start_rows(i + 1, other)
            idx_copy(i + 2, slot).start()
            xb = xbuf[slot][...].astype(BF16)
            hb = _silu(_dot(xb, wg_b[...])) * _dot(xb, wu_b[...])
            y_ref[...] = _dot(hb.astype(BF16), wd_b[...])

        @pl.when((i == n_used) & (lax.rem(i, 2) == slot))
        def _():
            wait_rows(slot)
            idx_copy(i + 1, other).wait()

    @pl.when(i >= n_used)
    def _():
        y_ref[...] = jnp.zeros_like(y_ref)


def _experts(block_e, n_used, block_c0, stok, x1, w_gate_e, w_up_e, w_down_e):
    nb = block_e.shape[0]
    grid_spec = pltpu.PrefetchScalarGridSpec(
        num_scalar_prefetch=3,
        grid=(nb,),
        in_specs=[
            pl.BlockSpec(memory_space=pl.ANY),
            pl.BlockSpec(memory_space=pl.ANY),
            pl.BlockSpec((1, D_MODEL, EXPERT_FF), lambda i, be, nu, c0: (be[i], 0, 0)),
            pl.BlockSpec((1, D_MODEL, EXPERT_FF), lambda i, be, nu, c0: (be[i], 0, 0)),
            pl.BlockSpec((1, EXPERT_FF, D_MODEL), lambda i, be, nu, c0: (be[i], 0, 0)),
        ],
        out_specs=pl.BlockSpec((EXPERT_BLOCK, D_MODEL), lambda i, be, nu, c0: (i, 0)),
        scratch_shapes=[
            pltpu.SMEM((IDX_WINDOW,), jnp.int32),
            pltpu.SMEM((IDX_WINDOW,), jnp.int32),
            pltpu.VMEM((EXPERT_BLOCK, D_MODEL), F32),
            pltpu.VMEM((EXPERT_BLOCK, D_MODEL), F32),
            pltpu.SemaphoreType.DMA((2,)),
            pltpu.SemaphoreType.DMA((2,)),
            pltpu.VMEM((D_MODEL, EXPERT_FF), BF16),
            pltpu.VMEM((D_MODEL, EXPERT_FF), BF16),
            pltpu.VMEM((EXPERT_FF, D_MODEL), BF16),
        ],
    )
    return pl.pallas_call(
        _experts_kernel,
        grid_spec=grid_spec,
        out_shape=jax.ShapeDtypeStruct((nb * EXPERT_BLOCK, D_MODEL), F32),
        compiler_params=pltpu.CompilerParams(
            dimension_semantics=("arbitrary",), vmem_limit_bytes=VMEM_LIMIT),
        name="experts",
    )(block_e, n_used, block_c0, stok, x1, w_gate_e, w_up_e, w_down_e)


def _final_kernel(np_tiles, dest_ref, x1_ref, topw_ref, yb_ref, wsg_ref, wsu_ref, wsd_ref,
                  g2_ref, b2_ref, yp_ref, ys_ref, ybuf, sem):
    i = pl.program_id(0)

    def issue(t, carry):
        for j in range(TOP_K):
            pltpu.make_async_copy(yb_ref.at[pl.ds(dest_ref[j, t], 1), :],
                                  ybuf.at[j, pl.ds(t, 1), :], sem).start()
        return carry

    lax.fori_loop(0, MIX_TILE, issue, 0)

    x1 = x1_ref[...]
    xb = x1.astype(BF16)
    hs = _silu(_dot(xb, wsg_ref[...])) * _dot(xb, wsu_ref[...])
    ff = _dot(hs.astype(BF16), wsd_ref[...])
    r = lax.broadcasted_iota(jnp.int32, (MIX_TILE, MIX_TILE), 0)
    c = lax.broadcasted_iota(jnp.int32, (MIX_TILE, MIX_TILE), 1)
    w_cols = _dot_nt((r == c).astype(F32), topw_ref[...], HIGHEST)
    for j in range(TOP_K):
        pltpu.make_async_copy(yb_ref.at[pl.ds(0, MIX_TILE), :], ybuf.at[j], sem).wait()
    for j in range(TOP_K):
        ff = ff + w_cols[:, j:j + 1] * ybuf[j]
    y = _layer_norm(ALPHA * x1 + ff, g2_ref[...], b2_ref[...])

    @pl.when(i < np_tiles)
    def _():
        yp_ref[...] = y

    @pl.when(i >= np_tiles)
    def _():
        ys_ref[...] = y


def _final(n_p, dest, x1, top_w_t, yb, wsg, wsu, wsd, g2, b2):
    nt = x1.shape[0]
    np_tiles = n_p // MIX_TILE
    row = lambda i: (i, 0)
    const = lambda i: (0, 0)
    return pl.pallas_call(
        functools.partial(_final_kernel, np_tiles),
        grid=(nt // MIX_TILE,),
        in_specs=[
            pl.BlockSpec((TOP_K, MIX_TILE), lambda i: (0, i), memory_space=pltpu.SMEM),
            pl.BlockSpec((MIX_TILE, D_MODEL), row),
            pl.BlockSpec((TOP_K, MIX_TILE), lambda i: (0, i)),
            pl.BlockSpec(memory_space=pl.ANY),
            pl.BlockSpec((D_MODEL, EXPERT_FF), const),
            pl.BlockSpec((D_MODEL, EXPERT_FF), const),
            pl.BlockSpec((EXPERT_FF, D_MODEL), const),
            pl.BlockSpec((1, D_MODEL), const),
            pl.BlockSpec((1, D_MODEL), const),
        ],
        out_specs=(
            pl.BlockSpec((MIX_TILE, D_MODEL), lambda i: (jnp.minimum(i, np_tiles - 1), 0)),
            pl.BlockSpec((MIX_TILE, D_MODEL), lambda i: (jnp.maximum(i - np_tiles, 0), 0)),
        ),
        out_shape=(
            jax.ShapeDtypeStruct((n_p, D_MODEL), F32),
            jax.ShapeDtypeStruct((nt - n_p, D_MODEL), F32),
        ),
        scratch_shapes=[
            pltpu.VMEM((TOP_K, MIX_TILE, D_MODEL), F32),
            pltpu.SemaphoreType.DMA(()),
        ],
        compiler_params=pltpu.CompilerParams(
            dimension_semantics=("arbitrary",), vmem_limit_bytes=VMEM_LIMIT),
        name="final",
    )(dest, x1, top_w_t, yb, wsg, wsu, wsd, g2, b2)


def _expert_layout(counts, nk):
    nb = -(-nk // EXPERT_BLOCK) + N_EXPERTS
    padded = (counts + EXPERT_BLOCK - 1) // EXPERT_BLOCK * EXPERT_BLOCK
    pend = jnp.cumsum(padded)
    pstart = (pend - padded).astype(jnp.int32)
    block_start = jnp.arange(nb, dtype=jnp.int32) * EXPERT_BLOCK
    block_e = jnp.sum((pend[None, :] <= block_start[:, None]).astype(jnp.int32), 1)
    block_e = jnp.minimum(block_e, N_EXPERTS - 1)
    n_used = (pend[-1] // EXPERT_BLOCK).astype(jnp.int32).reshape(1)
    start = (jnp.cumsum(counts) - counts).astype(jnp.int32)
    block_c0 = start[block_e] + block_start - pstart[block_e]
    block_c0 = jnp.where(block_start < pend[-1], block_c0, 0)
    block_c0 = jnp.concatenate([block_c0, jnp.zeros((2,), jnp.int32)]).astype(jnp.int32)
    return pstart, block_e, n_used, block_c0


def _pad_lanes(v, offset=0):
    out = jnp.zeros((1, LANES), F32)
    return out.at[0, offset:offset + v.shape[0]].set(v.astype(F32))


def kernel(x_prompt, x_sample, state_conv, state_gdn, cache_swa_k, cache_swa_v, w_in, conv_w, a_log, dt_bias, gdn_norm_w, attn_sinks, w_o, ln1_g, ln1_b, w_router, router_bias, w_gate_e, w_up_e, w_down_e, w_shared_gate, w_shared_up, w_shared_down, ln2_g, ln2_b):
    bp, tp, d = x_prompt.shape
    bs, ts, _ = x_sample.shape
    n_p, n_s = bp * tp, bs * ts

    wi = w_in[0]
    o1 = CONV_DIM
    o2 = o1 + GDN_WIDTH
    o4 = o2 + 2 * GDN_HEADS
    o5 = o4 + SWA_WIDTH
    o6 = o5 + SWA_KV_WIDTH
    ba_cols = jnp.zeros((d, LANES), F32).at[:, :2 * GDN_HEADS].set(wi[:, o2:o4])
    w_all = jnp.concatenate([wi[:, :o2], wi[:, o4:o5], wi[:, o5:o6], wi[:, o6:], ba_cols], 1).astype(BF16)
    wo = w_o[0].astype(BF16)
    wrt = w_router[0].T.astype(BF16)
    rbias = router_bias[0].astype(F32).reshape(N_EXPERTS, 1)
    alog_pad = _pad_lanes(a_log[0], GDN_HEADS)
    dtb_pad = _pad_lanes(dt_bias[0], GDN_HEADS)
    sinks_pad = _pad_lanes(attn_sinks[0])
    norm_w = gdn_norm_w[0].reshape(1, GDN_D)
    g1, b1 = ln1_g[0].reshape(1, d), ln1_b[0].reshape(1, d)
    g2, b2 = ln2_g[0].reshape(1, d), ln2_b[0].reshape(1, d)

    def front(x, pos0, L, conv_hist, s0, k_hist, v_hist):
        b, t, _ = x.shape
        period = max(t, IN_TILE)
        pos = pos0 + (jnp.arange(period, dtype=jnp.int32) % t).astype(F32)
        cos, s1, s2 = _rope_tables(pos)
        x2d = x.reshape(b * t, d)
        qkv, z, sq, sk, sv, ba = _in_proj(x2d, w_all, cos, s1, s2)
        qkv = qkv.reshape(b, t, CONV_DIM)
        hist8 = jnp.concatenate(
            [jnp.zeros((b, HIST_ROWS - (CONV_WIDTH - 1), CONV_DIM), F32), conv_hist], 1)
        gd, s_new = _gdn(L, qkv, z.reshape(b, t, GDN_WIDTH), ba.reshape(b, t, LANES), hist8, s0,
                         conv_w[0], alog_pad, dtb_pad, norm_w)
        k_ext = jnp.concatenate([k_hist.reshape(b, WINDOW, SWA_KV_WIDTH),
                                 sk.reshape(b, t, SWA_KV_WIDTH)], 1)
        v_ext = jnp.concatenate([v_hist.reshape(b, WINDOW, SWA_KV_WIDTH),
                                 sv.reshape(b, t, SWA_KV_WIDTH)], 1)
        sw = _swa(L, pos0, sq.reshape(b, t, SWA_WIDTH), k_ext, v_ext, sinks_pad)
        conv_new = jnp.concatenate([conv_hist, qkv], 1)[:, -(CONV_WIDTH - 1):]
        k_new = k_ext[:, -WINDOW:].reshape(b, WINDOW, SWA_KV_HEADS, SWA_HEAD_DIM)
        v_new = v_ext[:, -WINDOW:].reshape(b, WINDOW, SWA_KV_HEADS, SWA_HEAD_DIM)
        return (x2d, gd.reshape(b * t, GDN_WIDTH), sw.reshape(b * t, SWA_WIDTH),
                conv_new, s_new, k_new, v_new)

    zeros = lambda *s: jnp.zeros(s, F32)
    xp2, gd_p, sw_p, conv_p, gdn_p, k_p, v_p = front(
        x_prompt, 0, CHUNK, zeros(bp, CONV_WIDTH - 1, CONV_DIM),
        zeros(bp, GDN_HEADS, GDN_D, GDN_D), zeros(bp, WINDOW, SWA_KV_WIDTH),
        zeros(bp, WINDOW, SWA_KV_WIDTH))
    xs2, gd_s, sw_s, conv_s, gdn_s, k_s, v_s = front(
        x_sample, PAST_LEN, ts, state_conv[0], state_gdn[0], cache_swa_k[0], cache_swa_v[0])

    x1, top_e_t, top_w_t, rank_t, counts = _mix(
        gd_p, sw_p, xp2, gd_s, sw_s, xs2, wo, g1, b1, wrt, rbias)

    pstart, block_e, n_used, block_c0 = _expert_layout(counts[:, 0], TOP_K * (n_p + n_s))
    dest = _dest(top_e_t, rank_t, pstart.reshape(N_EXPERTS, 1))
    order = jnp.argsort(top_e_t.T.reshape(-1))
    stok = jnp.concatenate([(order // TOP_K).astype(jnp.int32),
                            jnp.zeros((IDX_WINDOW,), jnp.int32)])
    yb = _experts(block_e, n_used, block_c0, stok, x1, w_gate_e[0], w_up_e[0], w_down_e[0])

    y_p, y_s = _final(n_p, dest, x1, top_w_t, yb,
                      w_shared_gate[0].astype(BF16), w_shared_up[0].astype(BF16),
                      w_shared_down[0].astype(BF16), g2, b2)
    return (y_p.reshape(bp, tp, d), y_s.reshape(bs, ts, d),
            conv_p[None], gdn_p[None], k_p[None], v_p[None],
            conv_s[None], gdn_s[None], k_s[None], v_s[None])
```

```python
import functools
import math

import jax
import jax.numpy as jnp
from jax import lax
from jax.experimental import pallas as pl
from jax.experimental.pallas import tpu as pltpu

F32 = jnp.float32
BF16 = jnp.bfloat16
HIGHEST = lax.Precision.HIGHEST

D_MODEL = 1024
CHUNK = 64
GDN_HEADS = 4
GDN_D = 128
GDN_WIDTH = GDN_HEADS * GDN_D
CONV_DIM = 3 * GDN_WIDTH
CONV_WIDTH = 4
SWA_HEAD_DIM = 64
SWA_Q_HEADS = 8
SWA_KV_HEADS = 2
SWA_GROUP = SWA_Q_HEADS // SWA_KV_HEADS
SWA_WIDTH = SWA_Q_HEADS * SWA_HEAD_DIM
SWA_KV_WIDTH = SWA_KV_HEADS * SWA_HEAD_DIM
WINDOW = 128
ROPE_THETA = 500000.0
ROT_DIM = SWA_HEAD_DIM // 4
ROT_HALF = ROT_DIM // 2
N_EXPERTS = 256
TOP_K = 8
N_GROUPS = 8
GROUP_SIZE = N_EXPERTS // N_GROUPS
TOPK_GROUPS = 4
EXPERT_FF = 256
ROUTED_SCALE = 2.5
ALPHA = 2.0 ** 0.25
LN_EPS = 1e-5
NORM_EPS = 1e-6
PAST_LEN = 4096

LANES = 128
HIST_ROWS = 8

C_QKV = 0
C_Z = C_QKV + CONV_DIM
C_SQ = C_Z + GDN_WIDTH
C_SK = C_SQ + SWA_WIDTH
C_SV = C_SK + SWA_KV_WIDTH
C_BA = C_SV + SWA_KV_WIDTH
C_END = C_BA + LANES

IN_TILE = 512
MIX_TILE = 256
DEST_TILE = 512
EXPERT_BLOCK = 256
IDX_WINDOW = EXPERT_BLOCK + LANES
GDN_BATCH = 2
SWA_CHUNKS = 2
VMEM_LIMIT = 56 * 1024 * 1024


def _dot(a, b, precision=None):
    return jnp.dot(a, b, preferred_element_type=F32, precision=precision)


def _dot_nt(a, b, precision=None):
    return lax.dot_general(a, b, (((1,), (1,)), ((), ())),
                           preferred_element_type=F32, precision=precision)


def _dot_tn(a, b):
    return lax.dot_general(a, b, (((0,), (0,)), ((), ())), preferred_element_type=F32)


def _split(x):
    hi = x.astype(BF16)
    return hi, (x - hi.astype(F32)).astype(BF16)


def _dot_split(a, b):
    return _dot(a[0], b[0]) + (_dot(a[1], b[0]) + _dot(a[0], b[1]))


def _silu(x):
    return x * jax.nn.sigmoid(x)


def _in_proj_kernel(x_ref, w_ref, cos_ref, s1_ref, s2_ref,
                    qkv_ref, z_ref, sq_ref, sk_ref, sv_ref, ba_ref):
    xb = x_ref[...].astype(BF16)
    qkv_ref[...] = _dot(xb, w_ref[:, C_QKV:C_Z])
    z_ref[...] = _dot(xb, w_ref[:, C_Z:C_SQ])
    sv_ref[...] = _dot(xb, w_ref[:, C_SV:C_BA])
    ba_ref[...] = _dot(xb, w_ref[:, C_BA:C_END])
    cos = cos_ref[...]
    s1 = s1_ref[...]
    s2 = s2_ref[...]

    def rope(t):
        return (t * cos + pltpu.roll(t, LANES - ROT_HALF, 1) * s1
                + pltpu.roll(t, ROT_HALF, 1) * s2)

    sk_ref[...] = rope(_dot(xb, w_ref[:, C_SK:C_SV]))
    for g in range(SWA_WIDTH // LANES):
        lo = C_SQ + g * LANES
        sq_ref[:, g * LANES:(g + 1) * LANES] = rope(_dot(xb, w_ref[:, lo:lo + LANES])).astype(BF16)


def _in_proj(x2d, w_all, cos, s1, s2):
    n = x2d.shape[0]
    period_tiles = cos.shape[0] // IN_TILE
    row = lambda i: (i, 0)
    tab = lambda i: (i % period_tiles, 0)
    out_shapes = (
        jax.ShapeDtypeStruct((n, CONV_DIM), F32),
        jax.ShapeDtypeStruct((n, GDN_WIDTH), F32),
        jax.ShapeDtypeStruct((n, SWA_WIDTH), BF16),
        jax.ShapeDtypeStruct((n, SWA_KV_WIDTH), F32),
        jax.ShapeDtypeStruct((n, SWA_KV_WIDTH), F32),
        jax.ShapeDtypeStruct((n, LANES), F32),
    )
    return pl.pallas_call(
        _in_proj_kernel,
        grid=(n // IN_TILE,),
        in_specs=[
            pl.BlockSpec((IN_TILE, D_MODEL), row),
            pl.BlockSpec((D_MODEL, C_END), lambda i: (0, 0)),
            pl.BlockSpec((IN_TILE, LANES), tab),
            pl.BlockSpec((IN_TILE, LANES), tab),
            pl.BlockSpec((IN_TILE, LANES), tab),
        ],
        out_specs=(
            pl.BlockSpec((IN_TILE, CONV_DIM), row),
            pl.BlockSpec((IN_TILE, GDN_WIDTH), row),
            pl.BlockSpec((IN_TILE, SWA_WIDTH), row),
            pl.BlockSpec((IN_TILE, SWA_KV_WIDTH), row),
            pl.BlockSpec((IN_TILE, SWA_KV_WIDTH), row),
            pl.BlockSpec((IN_TILE, LANES), row),
        ),
        out_shape=out_shapes,
        compiler_params=pltpu.CompilerParams(
            dimension_semantics=("arbitrary",), vmem_limit_bytes=VMEM_LIMIT),
        name="in_proj",
    )(x2d, w_all, cos, s1, s2)


def _rope_tables(pos):
    p = pos.shape[0]
    inv = ROPE_THETA ** (-jnp.arange(0, ROT_DIM, 2, dtype=F32) / ROT_DIM)
    ang = pos[:, None] * inv[None, :]
    cos = jnp.cos(ang)
    sin = jnp.sin(ang)
    rest = SWA_HEAD_DIM - ROT_DIM
    head_c = jnp.concatenate([cos, cos, jnp.ones((p, rest), F32)], 1)
    head_s1 = jnp.concatenate([-sin, jnp.zeros((p, SWA_HEAD_DIM - ROT_HALF), F32)], 1)
    head_s2 = jnp.concatenate([jnp.zeros((p, ROT_HALF), F32), sin, jnp.zeros((p, rest), F32)], 1)
    two = lambda t: jnp.concatenate([t, t], 1)
    return two(head_c), two(head_s1), two(head_s2)


def _gdn_kernel(L, qkv_ref, z_ref, ba_ref, hist_ref, s0_ref, convw_ref, alog_ref, dtb_ref,
                normw_ref, o_ref, sout_ref, xbuf, state):
    c = pl.program_id(1)
    nbat = qkv_ref.shape[0]
    chains = [(b, h) for b in range(nbat) for h in range(GDN_HEADS)]

    @pl.when(c == 0)
    def _():
        xbuf[:, 0:HIST_ROWS, :] = hist_ref[...]
        state[...] = s0_ref[...]

    xbuf[:, HIST_ROWS:HIST_ROWS + L, :] = qkv_ref[...]

    row = lax.broadcasted_iota(jnp.int32, (L, L), 0)
    col = lax.broadcasted_iota(jnp.int32, (L, L), 1)
    incl = row >= col
    strict = row > col
    lower_ones = incl.astype(BF16)
    eye = (row == col).astype(F32)

    beta_all, g_all = [], []
    for b in range(nbat):
        ba = ba_ref[b]
        beta_all.append(jax.nn.sigmoid(ba))
        sp_in = ba + dtb_ref[...]
        softplus = jnp.maximum(sp_in, 0.0) + jnp.log1p(jnp.exp(-jnp.abs(sp_in)))
        g_all.append(-jnp.exp(alog_ref[...]) * softplus)

    def conv_silu(b, c0):
        acc = None
        for j in range(CONV_WIDTH):
            lo = HIST_ROWS - (CONV_WIDTH - 1) + j
            t = xbuf[b, lo:lo + L, c0:c0 + GDN_D] * convw_ref[j:j + 1, c0:c0 + GDN_D]
            acc = t if acc is None else acc + t
        return _silu(acc)

    def l2n(t):
        return t * lax.rsqrt(jnp.sum(t * t, -1, keepdims=True) + NORM_EPS)

    q = [l2n(conv_silu(b, h * GDN_D)) * (GDN_D ** -0.5) for b, h in chains]
    k = [l2n(conv_silu(b, GDN_WIDTH + h * GDN_D)) for b, h in chains]
    v = [conv_silu(b, 2 * GDN_WIDTH + h * GDN_D) for b, h in chains]
    beta = [beta_all[b][:, h:h + 1] for b, h in chains]
    g = [g_all[b][:, GDN_HEADS + h:GDN_HEADS + h + 1] for b, h in chains]

    def pieces(t):
        p1 = t.astype(BF16).astype(F32)
        r1 = t - p1
        p2 = r1.astype(BF16).astype(F32)
        return p1, p2, r1 - p2

    g_parts = [pieces(t) for t in g]
    diff = [sum(_dot(lower_ones, jnp.where(strict, p, 0.0).astype(BF16)) for p in gp)
            for gp in g_parts]
    gcum = [sum(_dot(lower_ones, jnp.broadcast_to(p, (L, GDN_D)).astype(BF16)) for p in gp)
            for gp in g_parts]
    dec = [jnp.where(incl, jnp.exp(jnp.where(incl, d, 0.0)), 0.0) for d in diff]
    exp_g = [jnp.exp(t) for t in gcum]
    g_end = [t[L - 1:L, :] for t in gcum]
    kb = [t.astype(BF16) for t in k]
    kk = [_dot_nt(t, t) for t in kb]
    aqk = [_dot_nt(qq.astype(BF16), t) * d for qq, t, d in zip(q, kb, dec)]
    a = [jnp.where(strict, d * t, 0.0) * bt for d, t, bt in zip(dec, kk, beta)]

    inv = [eye - t for t in a]
    pw = [_split(-t) for t in a]
    for _ in range(int(math.log2(L)) - 1):
        pw = [_split(_dot_split(p, p)) for p in pw]
        inv = [t + _dot_split(_split(t), p) for t, p in zip(inv, pw)]
    inv_s = [_split(t) for t in inv]
    w = [_dot_split(t, _split(bt * vv)) for t, bt, vv in zip(inv_s, beta, v)]
    y = [_dot_split(t, _split((bt * eg) * kk_)) for t, bt, eg, kk_ in zip(inv_s, beta, exp_g, k)]

    s = [state[b, h] for b, h in chains]
    sb = [t.astype(BF16) for t in s]
    u = [ww - _dot(yy.astype(BF16), t) for ww, yy, t in zip(w, y, sb)]
    ub = [t.astype(BF16) for t in u]
    o = [_dot((eg * qq).astype(BF16), t) + _dot(aa.astype(BF16), uu)
         for eg, qq, t, aa, uu in zip(exp_g, q, sb, aqk, ub)]
    k_dec = [(jnp.exp(ge - gc) * kk_).astype(BF16) for ge, gc, kk_ in zip(g_end, gcum, k)]
    s_new = [jnp.exp(ge) * t + _dot_tn(kd, uu) for ge, t, kd, uu in zip(g_end, s, k_dec, ub)]

    for (b, h), t, oo in zip(chains, s_new, o):
        state[b, h] = t
        on = oo * lax.rsqrt(jnp.mean(oo * oo, -1, keepdims=True) + NORM_EPS)
        zz = z_ref[b, :, h * GDN_D:(h + 1) * GDN_D]
        o_ref[b, :, h * GDN_D:(h + 1) * GDN_D] = (on * normw_ref[...] * _silu(zz)).astype(BF16)

    xbuf[:, 0:HIST_ROWS, :] = xbuf[:, L:L + HIST_ROWS, :]

    @pl.when(c == pl.num_programs(1) - 1)
    def _():
        sout_ref[...] = state[...]


def _gdn(L, qkv, z, ba, hist8, s0, conv_w, alog_pad, dtb_pad, norm_w):
    b, t, _ = qkv.shape
    nc = t // L
    nbat = GDN_BATCH
    tok = lambda i, c: (i, c, 0)
    per_b3 = lambda i, c: (i, 0, 0)
    per_b4 = lambda i, c: (i, 0, 0, 0)
    const2 = lambda i, c: (0, 0)
    return pl.pallas_call(
        functools.partial(_gdn_kernel, L),
        grid=(b // nbat, nc),
        in_specs=[
            pl.BlockSpec((nbat, L, CONV_DIM), tok),
            pl.BlockSpec((nbat, L, GDN_WIDTH), tok),
            pl.BlockSpec((nbat, L, LANES), tok),
            pl.BlockSpec((nbat, HIST_ROWS, CONV_DIM), per_b3),
            pl.BlockSpec((nbat, GDN_HEADS, GDN_D, GDN_D), per_b4),
            pl.BlockSpec((CONV_WIDTH, CONV_DIM), const2),
            pl.BlockSpec((1, LANES), const2),
            pl.BlockSpec((1, LANES), const2),
            pl.BlockSpec((1, GDN_D), const2),
        ],
        out_specs=(
            pl.BlockSpec((nbat, L, GDN_WIDTH), tok),
            pl.BlockSpec((nbat, GDN_HEADS, GDN_D, GDN_D), per_b4),
        ),
        out_shape=(
            jax.ShapeDtypeStruct((b, t, GDN_WIDTH), BF16),
            jax.ShapeDtypeStruct((b, GDN_HEADS, GDN_D, GDN_D), F32),
        ),
        scratch_shapes=[
            pltpu.VMEM((nbat, HIST_ROWS + L, CONV_DIM), F32),
            pltpu.VMEM((nbat, GDN_HEADS, GDN_D, GDN_D), F32),
        ],
        compiler_params=pltpu.CompilerParams(
            dimension_semantics=("arbitrary", "arbitrary"), vmem_limit_bytes=VMEM_LIMIT),
        name="gdn",
    )(qkv, z, ba, hist8, s0, conv_w, alog_pad, dtb_pad, norm_w)


def _swa_kernel(L, pos0, q_ref, k_ref, v_ref, sink_ref, o_ref):
    c = pl.program_id(1)
    nch = q_ref.shape[1] // L
    span = WINDOW + L
    rows = WINDOW + nch * L
    start = pl.multiple_of(c * (nch * L), L)
    kx = k_ref[0, pl.ds(start, rows), :]
    vx = v_ref[0, pl.ds(start, rows), :]
    kx_sw = pltpu.roll(kx, SWA_HEAD_DIM, 1)
    vx_sw = pltpu.roll(vx, SWA_HEAD_DIM, 1)
    low = lax.broadcasted_iota(jnp.int32, (rows, LANES), 1) < SWA_HEAD_DIM

    def halves(x, x_sw, kh):
        src_lo, src_hi = (x, x_sw) if kh == 0 else (x_sw, x)
        return (jnp.where(low, src_lo, 0.0).astype(BF16), jnp.where(low, 0.0, src_hi).astype(BF16))

    k_halves = [halves(kx, kx_sw, kh) for kh in range(SWA_KV_HEADS)]
    v_halves = [halves(vx, vx_sw, kh) for kh in range(SWA_KV_HEADS)]
    sinks = sink_ref[...]
    top_rows = lax.broadcasted_iota(jnp.int32, (2 * L, 1), 0) < L
    first_pos = pos0 - WINDOW + c * (nch * L)

    chains = [(u, kh, half) for u in range(nch) for kh in range(SWA_KV_HEADS) for half in range(2)]
    valid = [first_pos + u * L + lax.broadcasted_iota(jnp.int32, (1, span), 1) >= 0
             for u in range(nch)]
    q4 = {}
    for u in range(nch):
        for kh in range(SWA_KV_HEADS):
            c0 = kh * 2 * LANES
            q4[u, kh] = jnp.concatenate([q_ref[0, u * L:(u + 1) * L, c0:c0 + LANES],
                                         q_ref[0, u * L:(u + 1) * L, c0 + LANES:c0 + 2 * LANES]], 0)
    scores = [jnp.where(valid[u],
                        _dot_nt(q4[u, kh], k_halves[kh][half][u * L:u * L + span])
                        * (SWA_HEAD_DIM ** -0.5), -jnp.inf)
              for u, kh, half in chains]
    sink_cols = [jnp.where(top_rows, sinks[:, kh * SWA_GROUP + half:kh * SWA_GROUP + half + 1],
                           sinks[:, kh * SWA_GROUP + half + 2:kh * SWA_GROUP + half + 3])
                 for _, kh, half in chains]
    m = [jnp.maximum(jnp.max(s, -1, keepdims=True), sk) for s, sk in zip(scores, sink_cols)]
    p = [jnp.exp(s - mm) for s, mm in zip(scores, m)]
    den = [jnp.sum(pp, -1, keepdims=True) + jnp.exp(sk - mm) for pp, sk, mm in zip(p, sink_cols, m)]
    pv = [_dot(pp.astype(BF16), v_halves[kh][half][u * L:u * L + span]) / dd
          for pp, dd, (u, kh, half) in zip(p, den, chains)]
    for n in range(0, len(chains), 2):
        u, kh, _ = chains[n]
        o = pv[n] + pv[n + 1]
        c0 = kh * 2 * LANES
        o_ref[0, u * L:(u + 1) * L, c0:c0 + LANES] = o[0:L].astype(BF16)
        o_ref[0, u * L:(u + 1) * L, c0 + LANES:c0 + 2 * LANES] = o[L:2 * L].astype(BF16)


def _swa(L, pos0, q, k_ext, v_ext, sinks_pad):
    b, t, _ = q.shape
    ext = k_ext.shape[1]
    nch = SWA_CHUNKS if (t // L) % SWA_CHUNKS == 0 else 1
    return pl.pallas_call(
        functools.partial(_swa_kernel, L, pos0),
        grid=(b, t // (nch * L)),
        in_specs=[
            pl.BlockSpec((1, nch * L, SWA_WIDTH), lambda i, c: (i, c, 0)),
            pl.BlockSpec((1, ext, SWA_KV_WIDTH), lambda i, c: (i, 0, 0)),
            pl.BlockSpec((1, ext, SWA_KV_WIDTH), lambda i, c: (i, 0, 0)),
            pl.BlockSpec((1, LANES), lambda i, c: (0, 0)),
        ],
        out_specs=pl.BlockSpec((1, nch * L, SWA_WIDTH), lambda i, c: (i, c, 0)),
        out_shape=jax.ShapeDtypeStruct((b, t, SWA_WIDTH), BF16),
        compiler_params=pltpu.CompilerParams(
            dimension_semantics=("arbitrary", "arbitrary"), vmem_limit_bytes=VMEM_LIMIT),
        name="swa",
    )(q, k_ext, v_ext, sinks_pad)


def _layer_norm(r, g, b):
    mu = jnp.mean(r, -1, keepdims=True)
    d = r - mu
    var = jnp.mean(d * d, -1, keepdims=True)
    return d * lax.rsqrt(var + LN_EPS) * g + b


def _route(x1b, wrt_ref, rbias_ref, count_ref):
    tm = x1b.shape[0]
    scores = jax.nn.sigmoid(_dot_nt(wrt_ref[...], x1b))
    biased = scores + rbias_ref[...]
    neg = -jnp.inf
    r_g = lax.broadcasted_iota(jnp.int32, (GROUP_SIZE, tm), 0)
    blocks = []
    group_score = []
    for g in range(N_GROUPS):
        blk = biased[g * GROUP_SIZE:(g + 1) * GROUP_SIZE]
        m1 = jnp.max(blk, 0, keepdims=True)
        i1 = jnp.min(jnp.where(blk == m1, r_g, GROUP_SIZE), 0, keepdims=True)
        m2 = jnp.max(jnp.where(r_g == i1, neg, blk), 0, keepdims=True)
        blocks.append(blk)
        group_score.append(m1 + m2)
    masked = []
    for g in range(N_GROUPS):
        rank = jnp.zeros((1, tm), jnp.int32)
        for o in range(N_GROUPS):
            if o == g:
                continue
            ahead = group_score[o] > group_score[g]
            if o < g:
                ahead = ahead | (group_score[o] == group_score[g])
            rank = rank + ahead.astype(jnp.int32)
        masked.append(jnp.where(rank < TOPK_GROUPS, blocks[g], neg))
    masked = jnp.concatenate(masked, 0)
    r_e = lax.broadcasted_iota(jnp.int32, (N_EXPERTS, tm), 0)
    idx_rows = []
    w_rows = []
    hits = []
    for _ in range(TOP_K):
        m = jnp.max(masked, 0, keepdims=True)
        idx = jnp.min(jnp.where(masked == m, r_e, N_EXPERTS), 0, keepdims=True)
        hit = r_e == idx
        w_rows.append(jnp.sum(jnp.where(hit, scores, 0.0), 0, keepdims=True))
        masked = jnp.where(hit, neg, masked)
        idx_rows.append(idx)
        hits.append(hit)
    total = w_rows[0]
    for wr in w_rows[1:]:
        total = total + wr
    top_w = jnp.concatenate(w_rows, 0) / total * ROUTED_SCALE

    chosen = jnp.zeros((N_EXPERTS, tm), F32)
    for hit in hits:
        chosen = chosen + hit.astype(F32)
    t_r = lax.broadcasted_iota(jnp.int32, (tm, tm), 0)
    t_c = lax.broadcasted_iota(jnp.int32, (tm, tm), 1)
    before = (t_r < t_c).astype(BF16)
    prior = _dot(chosen.astype(BF16), before) + count_ref[...]
    rank_rows = [jnp.sum(jnp.where(hit, prior, 0.0), 0, keepdims=True) for hit in hits]
    count_ref[...] = count_ref[...] + jnp.sum(chosen, 1, keepdims=True)
    rank = jnp.concatenate(rank_rows, 0).astype(jnp.int32)
    return jnp.concatenate(idx_rows, 0), top_w, rank


def _mix_kernel(np_tiles, gp_ref, sp_ref, xp_ref, gs_ref, ss_ref, xs_ref, wo_ref, g1_ref, b1_ref,
                wrt_ref, rbias_ref, x1_ref, tope_ref, topw_ref, rank_ref, cnt_ref, count_acc):
    i = pl.program_id(0)

    @pl.when(i == 0)
    def _():
        count_acc[...] = jnp.zeros_like(count_acc)

    is_prompt = i < np_tiles
    gd = jnp.where(is_prompt, gp_ref[...], gs_ref[...])
    sw = jnp.where(is_prompt, sp_ref[...], ss_ref[...])
    x = jnp.where(is_prompt, xp_ref[...], xs_ref[...])
    mix = _dot(gd, wo_ref[0:GDN_WIDTH, :]) + _dot(sw, wo_ref[GDN_WIDTH:, :])
    x1 = _layer_norm(ALPHA * x + mix, g1_ref[...], b1_ref[...])
    x1_ref[...] = x1
    top_e, top_w, rank = _route(x1.astype(BF16), wrt_ref, rbias_ref, count_acc)
    tope_ref[...] = top_e
    topw_ref[...] = top_w
    rank_ref[...] = rank
    cnt_ref[...] = count_acc[...].astype(jnp.int32)


def _mix(gd_p, sw_p, x_p, gd_s, sw_s, x_s, wo, g1, b1, wrt, rbias):
    n_p, n_s = x_p.shape[0], x_s.shape[0]
    np_tiles = n_p // MIX_TILE
    nt = n_p + n_s
    pm = lambda i: (jnp.minimum(i, np_tiles - 1), 0)
    sm = lambda i: (jnp.maximum(i - np_tiles, 0), 0)
    const = lambda i: (0, 0)
    col = lambda i: (0, i)
    return pl.pallas_call(
        functools.partial(_mix_kernel, np_tiles),
        grid=(nt // MIX_TILE,),
        in_specs=[
            pl.BlockSpec((MIX_TILE, GDN_WIDTH), pm),
            pl.BlockSpec((MIX_TILE, SWA_WIDTH), pm),
            pl.BlockSpec((MIX_TILE, D_MODEL), pm),
            pl.BlockSpec((MIX_TILE, GDN_WIDTH), sm),
            pl.BlockSpec((MIX_TILE, SWA_WIDTH), sm),
            pl.BlockSpec((MIX_TILE, D_MODEL), sm),
            pl.BlockSpec((D_MODEL, D_MODEL), const),
            pl.BlockSpec((1, D_MODEL), const),
            pl.BlockSpec((1, D_MODEL), const),
            pl.BlockSpec((N_EXPERTS, D_MODEL), const),
            pl.BlockSpec((N_EXPERTS, 1), const),
        ],
        out_specs=(
            pl.BlockSpec((MIX_TILE, D_MODEL), lambda i: (i, 0)),
            pl.BlockSpec((TOP_K, MIX_TILE), col),
            pl.BlockSpec((TOP_K, MIX_TILE), col),
            pl.BlockSpec((TOP_K, MIX_TILE), col),
            pl.BlockSpec((N_EXPERTS, 1), const),
        ),
        out_shape=(
            jax.ShapeDtypeStruct((nt, D_MODEL), F32),
            jax.ShapeDtypeStruct((TOP_K, nt), jnp.int32),
            jax.ShapeDtypeStruct((TOP_K, nt), F32),
            jax.ShapeDtypeStruct((TOP_K, nt), jnp.int32),
            jax.ShapeDtypeStruct((N_EXPERTS, 1), jnp.int32),
        ),
        scratch_shapes=[pltpu.VMEM((N_EXPERTS, 1), F32)],
        compiler_params=pltpu.CompilerParams(
            dimension_semantics=("arbitrary",), vmem_limit_bytes=VMEM_LIMIT),
        name="mix",
    )(gd_p, sw_p, x_p, gd_s, sw_s, x_s, wo, g1, b1, wrt, rbias)


def _dest_kernel(tope_ref, rank_ref, pstart_ref, dest_ref):
    tm = tope_ref.shape[1]
    r_e = lax.broadcasted_iota(jnp.int32, (N_EXPERTS, tm), 0)
    pstart = pstart_ref[...]
    top_e = tope_ref[...]
    rows = [jnp.sum(jnp.where(r_e == top_e[j:j + 1, :], pstart, 0), 0, keepdims=True)
            for j in range(TOP_K)]
    dest_ref[...] = jnp.concatenate(rows, 0) + rank_ref[...]


def _dest(top_e_t, rank_t, pstart):
    nt = top_e_t.shape[1]
    col = lambda i: (0, i)
    return pl.pallas_call(
        _dest_kernel,
        grid=(nt // DEST_TILE,),
        in_specs=[
            pl.BlockSpec((TOP_K, DEST_TILE), col),
            pl.BlockSpec((TOP_K, DEST_TILE), col),
            pl.BlockSpec((N_EXPERTS, 1), lambda i: (0, 0)),
        ],
        out_specs=pl.BlockSpec((TOP_K, DEST_TILE), col),
        out_shape=jax.ShapeDtypeStruct((TOP_K, nt), jnp.int32),
        compiler_params=pltpu.CompilerParams(dimension_semantics=("arbitrary",)),
        name="dest",
    )(top_e_t, rank_t, pstart)


def _experts_kernel(be_ref, nused_ref, c0_ref, stok_ref, x1_ref, wg_ref, wu_ref, wd_ref, y_ref,
                    idx0, idx1, xbuf0, xbuf1, idx_sem, row_sem, wg_b, wu_b, wd_b):
    i = pl.program_id(0)
    n_used = nused_ref[0]
    idx = (idx0, idx1)
    xbuf = (xbuf0, xbuf1)

    def idx_copy(block, s):
        base = pl.multiple_of(c0_ref[block] // LANES * LANES, LANES)
        return pltpu.make_async_copy(stok_ref.at[pl.ds(base, IDX_WINDOW)], idx[s], idx_sem.at[s])

    def start_rows(block, s):
        off = lax.rem(c0_ref[block], LANES)
        for r in range(EXPERT_BLOCK):
            pltpu.make_async_copy(x1_ref.at[pl.ds(idx[s][off + r], 1), :],
                                  xbuf[s].at[pl.ds(r, 1), :], row_sem.at[s]).start(priority=r % 2)

    def wait_rows(s):
        pltpu.make_async_copy(x1_ref.at[pl.ds(0, EXPERT_BLOCK), :], xbuf[s], row_sem.at[s]).wait()

    @pl.when(i == 0)
    def _():
        idx_copy(0, 0).start()
        idx_copy(0, 0).wait()
        start_rows(0, 0)
        idx_copy(1, 1).start()

    prev = be_ref[jnp.maximum(i - 1, 0)]

    @pl.when((i == 0) | (be_ref[i] != prev))
    def _():
        wg_b[...] = wg_ref[0].astype(BF16)
        wu_b[...] = wu_ref[0].astype(BF16)
        wd_b[...] = wd_ref[0].astype(BF16)

    for slot in range(2):
        other = 1 - slot

        @pl.when((i < n_used) & (lax.rem(i, 2) == slot))
        def _():
            idx_copy(i + 1, other).wait()
            wait_rows(slot)
            start_rows(i + 1, other)
            idx_copy(i + 2, slot).start()
            xb = xbuf[slot][...].astype(BF16)
            hb = _silu(_dot(xb, wg_b[...])) * _dot(xb, wu_b[...])
            y_ref[...] = _dot(hb.astype(BF16), wd_b[...])

        @pl.when((i == n_used) & (lax.rem(i, 2) == slot))
        def _():
            wait_rows(slot)
            idx_copy(i + 1, other).wait()

    @pl.when(i >= n_used)
    def _():
        y_ref[...] = jnp.zeros_like(y_ref)


def _experts(block_e, n_used, block_c0, stok, x1, w_gate_e, w_up_e, w_down_e):
    nb = block_e.shape[0]
    grid_spec = pltpu.PrefetchScalarGridSpec(
        num_scalar_prefetch=3,
        grid=(nb,),
        in_specs=[
            pl.BlockSpec(memory_space=pl.ANY),
            pl.BlockSpec(memory_space=pl.ANY),
            pl.BlockSpec((1, D_MODEL, EXPERT_FF), lambda i, be, nu, c0: (be[i], 0, 0)),
            pl.BlockSpec((1, D_MODEL, EXPERT_FF), lambda i, be, nu, c0: (be[i], 0, 0)),
            pl.BlockSpec((1, EXPERT_FF, D_MODEL), lambda i, be, nu, c0: (be[i], 0, 0)),
        ],
        out_specs=pl.BlockSpec((EXPERT_BLOCK, D_MODEL), lambda i, be, nu, c0: (i, 0)),
        scratch_shapes=[
            pltpu.SMEM((IDX_WINDOW,), jnp.int32),
            pltpu.SMEM((IDX_WINDOW,), jnp.int32),
            pltpu.VMEM((EXPERT_BLOCK, D_MODEL), F32),
            pltpu.VMEM((EXPERT_BLOCK, D_MODEL), F32),
            pltpu.SemaphoreType.DMA((2,)),
            pltpu.SemaphoreType.DMA((2,)),
            pltpu.VMEM((D_MODEL, EXPERT_FF), BF16),
            pltpu.VMEM((D_MODEL, EXPERT_FF), BF16),
            pltpu.VMEM((EXPERT_FF, D_MODEL), BF16),
        ],
    )
    return pl.pallas_call(
        _experts_kernel,
        grid_spec=grid_spec,
        out_shape=jax.ShapeDtypeStruct((nb * EXPERT_BLOCK, D_MODEL), F32),
        compiler_params=pltpu.CompilerParams(
            dimension_semantics=("arbitrary",), vmem_limit_bytes=VMEM_LIMIT),
        name="experts",
    )(block_e, n_used, block_c0, stok, x1, w_gate_e, w_up_e, w_down_e)


def _final_kernel(np_tiles, dest_ref, dest_next_ref, x1_ref, topw_ref, yb_ref, wsg_ref, wsu_ref,
                  wsd_ref, g2_ref, b2_ref, yp_ref, ys_ref, ybuf, sem):
    i = pl.program_id(0)
    slot = lax.rem(i, 2)

    def issue_rows(d_ref, s):
        def body(t, carry):
            for j in range(TOP_K):
                pltpu.make_async_copy(yb_ref.at[pl.ds(d_ref[j, t], 1), :],
                                      ybuf.at[s, j, pl.ds(t, 1), :],
                                      sem.at[s]).start(priority=j % 2)
            return carry
        lax.fori_loop(0, MIX_TILE, body, 0)

    @pl.when(i == 0)
    def _():
        issue_rows(dest_ref, 0)

    @pl.when(i + 1 < pl.num_programs(0))
    def _():
        issue_rows(dest_next_ref, 1 - slot)

    x1 = x1_ref[...]
    xb = x1.astype(BF16)
    hs = _silu(_dot(xb, wsg_ref[...])) * _dot(xb, wsu_ref[...])
    ff = _dot(hs.astype(BF16), wsd_ref[...])
    r = lax.broadcasted_iota(jnp.int32, (MIX_TILE, MIX_TILE), 0)
    c = lax.broadcasted_iota(jnp.int32, (MIX_TILE, MIX_TILE), 1)
    w_cols = _dot_nt((r == c).astype(F32), topw_ref[...], HIGHEST)
    for j in range(TOP_K):
        pltpu.make_async_copy(yb_ref.at[pl.ds(0, MIX_TILE), :], ybuf.at[slot, j],
                              sem.at[slot]).wait()
    for j in range(TOP_K):
        ff = ff + w_cols[:, j:j + 1] * ybuf[slot, j]
    y = _layer_norm(ALPHA * x1 + ff, g2_ref[...], b2_ref[...])

    @pl.when(i < np_tiles)
    def _():
        yp_ref[...] = y

    @pl.when(i >= np_tiles)
    def _():
        ys_ref[...] = y


def _final(n_p, dest, x1, top_w_t, yb, wsg, wsu, wsd, g2, b2):
    nt = x1.shape[0]
    np_tiles = n_p // MIX_TILE
    row = lambda i: (i, 0)
    const = lambda i: (0, 0)
    n_tiles = nt // MIX_TILE
    return pl.pallas_call(
        functools.partial(_final_kernel, np_tiles),
        grid=(n_tiles,),
        in_specs=[
            pl.BlockSpec((TOP_K, MIX_TILE), lambda i: (0, i), memory_space=pltpu.SMEM),
            pl.BlockSpec((TOP_K, MIX_TILE), lambda i: (0, jnp.minimum(i + 1, n_tiles - 1)),
                         memory_space=pltpu.SMEM),
            pl.BlockSpec((MIX_TILE, D_MODEL), row),
            pl.BlockSpec((TOP_K, MIX_TILE), lambda i: (0, i)),
            pl.BlockSpec(memory_space=pl.ANY),
            pl.BlockSpec((D_MODEL, EXPERT_FF), const),
            pl.BlockSpec((D_MODEL, EXPERT_FF), const),
            pl.BlockSpec((EXPERT_FF, D_MODEL), const),
            pl.BlockSpec((1, D_MODEL), const),
            pl.BlockSpec((1, D_MODEL), const),
        ],
        out_specs=(
            pl.BlockSpec((MIX_TILE, D_MODEL), lambda i: (jnp.minimum(i, np_tiles - 1), 0)),
            pl.BlockSpec((MIX_TILE, D_MODEL), lambda i: (jnp.maximum(i - np_tiles, 0), 0)),
        ),
        out_shape=(
            jax.ShapeDtypeStruct((n_p, D_MODEL), F32),
            jax.ShapeDtypeStruct((nt - n_p, D_MODEL), F32),
        ),
        scratch_shapes=[
            pltpu.VMEM((2, TOP_K, MIX_TILE, D_MODEL), F32),
            pltpu.SemaphoreType.DMA((2,)),
        ],
        compiler_params=pltpu.CompilerParams(
            dimension_semantics=("arbitrary",), vmem_limit_bytes=VMEM_LIMIT),
        name="final",
    )(dest, dest, x1, top_w_t, yb, wsg, wsu, wsd, g2, b2)


def _expert_layout(counts, nk):
    nb = -(-nk // EXPERT_BLOCK) + N_EXPERTS
    padded = (counts + EXPERT_BLOCK - 1) // EXPERT_BLOCK * EXPERT_BLOCK
    pend = jnp.cumsum(padded)
    pstart = (pend - padded).astype(jnp.int32)
    block_start = jnp.arange(nb, dtype=jnp.int32) * EXPERT_BLOCK
    block_e = jnp.sum((pend[None, :] <= block_start[:, None]).astype(jnp.int32), 1)
    block_e = jnp.minimum(block_e, N_EXPERTS - 1)
    n_used = (pend[-1] // EXPERT_BLOCK).astype(jnp.int32).reshape(1)
    start = (jnp.cumsum(counts) - counts).astype(jnp.int32)
    block_c0 = start[block_e] + block_start - pstart[block_e]
    block_c0 = jnp.where(block_start < pend[-1], block_c0, 0)
    block_c0 = jnp.concatenate([block_c0, jnp.zeros((2,), jnp.int32)]).astype(jnp.int32)
    return pstart, block_e, n_used, block_c0


def _pad_lanes(v, offset=0):
    out = jnp.zeros((1, LANES), F32)
    return out.at[0, offset:offset + v.shape[0]].set(v.astype(F32))


def kernel(x_prompt, x_sample, state_conv, state_gdn, cache_swa_k, cache_swa_v, w_in, conv_w, a_log, dt_bias, gdn_norm_w, attn_sinks, w_o, ln1_g, ln1_b, w_router, router_bias, w_gate_e, w_up_e, w_down_e, w_shared_gate, w_shared_up, w_shared_down, ln2_g, ln2_b):
    bp, tp, d = x_prompt.shape
    bs, ts, _ = x_sample.shape
    n_p, n_s = bp * tp, bs * ts

    wi = w_in[0]
    o1 = CONV_DIM
    o2 = o1 + GDN_WIDTH
    o4 = o2 + 2 * GDN_HEADS
    o5 = o4 + SWA_WIDTH
    o6 = o5 + SWA_KV_WIDTH
    ba_cols = jnp.zeros((d, LANES), F32).at[:, :2 * GDN_HEADS].set(wi[:, o2:o4])
    w_all = jnp.concatenate([wi[:, :o2], wi[:, o4:o5], wi[:, o5:o6], wi[:, o6:], ba_cols], 1).astype(BF16)
    wo = w_o[0].astype(BF16)
    wrt = w_router[0].T.astype(BF16)
    rbias = router_bias[0].astype(F32).reshape(N_EXPERTS, 1)
    alog_pad = _pad_lanes(a_log[0], GDN_HEADS)
    dtb_pad = _pad_lanes(dt_bias[0], GDN_HEADS)
    sinks_pad = _pad_lanes(attn_sinks[0])
    norm_w = gdn_norm_w[0].reshape(1, GDN_D)
    g1, b1 = ln1_g[0].reshape(1, d), ln1_b[0].reshape(1, d)
    g2, b2 = ln2_g[0].reshape(1, d), ln2_b[0].reshape(1, d)

    def front(x, pos0, L, conv_hist, s0, k_hist, v_hist):
        b, t, _ = x.shape
        period = max(t, IN_TILE)
        pos = pos0 + (jnp.arange(period, dtype=jnp.int32) % t).astype(F32)
        cos, s1, s2 = _rope_tables(pos)
        x2d = x.reshape(b * t, d)
        qkv, z, sq, sk, sv, ba = _in_proj(x2d, w_all, cos, s1, s2)
        qkv = qkv.reshape(b, t, CONV_DIM)
        hist8 = jnp.concatenate(
            [jnp.zeros((b, HIST_ROWS - (CONV_WIDTH - 1), CONV_DIM), F32), conv_hist], 1)
        gd, s_new = _gdn(L, qkv, z.reshape(b, t, GDN_WIDTH), ba.reshape(b, t, LANES), hist8, s0,
                         conv_w[0], alog_pad, dtb_pad, norm_w)
        k_ext = jnp.concatenate([k_hist.reshape(b, WINDOW, SWA_KV_WIDTH),
                                 sk.reshape(b, t, SWA_KV_WIDTH)], 1)
        v_ext = jnp.concatenate([v_hist.reshape(b, WINDOW, SWA_KV_WIDTH),
                                 sv.reshape(b, t, SWA_KV_WIDTH)], 1)
        sw = _swa(L, pos0, sq.reshape(b, t, SWA_WIDTH), k_ext, v_ext, sinks_pad)
        conv_new = jnp.concatenate([conv_hist, qkv], 1)[:, -(CONV_WIDTH - 1):]
        k_new = k_ext[:, -WINDOW:].reshape(b, WINDOW, SWA_KV_HEADS, SWA_HEAD_DIM)
        v_new = v_ext[:, -WINDOW:].reshape(b, WINDOW, SWA_KV_HEADS, SWA_HEAD_DIM)
        return (x2d, gd.reshape(b * t, GDN_WIDTH), sw.reshape(b * t, SWA_WIDTH),
                conv_new, s_new, k_new, v_new)

    zeros = lambda *s: jnp.zeros(s, F32)
    xp2, gd_p, sw_p, conv_p, gdn_p, k_p, v_p = front(
        x_prompt, 0, CHUNK, zeros(bp, CONV_WIDTH - 1, CONV_DIM),
        zeros(bp, GDN_HEADS, GDN_D, GDN_D), zeros(bp, WINDOW, SWA_KV_WIDTH),
        zeros(bp, WINDOW, SWA_KV_WIDTH))
    xs2, gd_s, sw_s, conv_s, gdn_s, k_s, v_s = front(
        x_sample, PAST_LEN, ts, state_conv[0], state_gdn[0], cache_swa_k[0], cache_swa_v[0])

    x1, top_e_t, top_w_t, rank_t, counts = _mix(
        gd_p, sw_p, xp2, gd_s, sw_s, xs2, wo, g1, b1, wrt, rbias)

    pstart, block_e, n_used, block_c0 = _expert_layout(counts[:, 0], TOP_K * (n_p + n_s))
    dest = _dest(top_e_t, rank_t, pstart.reshape(N_EXPERTS, 1))
    order = jnp.argsort(top_e_t.T.reshape(-1))
    stok = jnp.concatenate([(order // TOP_K).astype(jnp.int32),
                            jnp.zeros((IDX_WINDOW,), jnp.int32)])
    yb = _experts(block_e, n_used, block_c0, stok, x1, w_gate_e[0], w_up_e[0], w_down_e[0])

    y_p, y_s = _final(n_p, dest, x1, top_w_t, yb,
                      w_shared_gate[0].astype(BF16), w_shared_up[0].astype(BF16),
                      w_shared_down[0].astype(BF16), g2, b2)
    return (y_p.reshape(bp, tp, d), y_s.reshape(bs, ts, d),
            conv_p[None], gdn_p[None], k_p[None], v_p[None],
            conv_s[None], gdn_s[None], k_s[None], v_s[None])
```

```python
import functools
import math

import jax
import jax.numpy as jnp
from jax import lax
from jax.experimental import pallas as pl
from jax.experimental.pallas import tpu as pltpu

F32 = jnp.float32
BF16 = jnp.bfloat16
HIGHEST = lax.Precision.HIGHEST

D_MODEL = 1024
CHUNK = 64
GDN_HEADS = 4
GDN_D = 128
GDN_WIDTH = GDN_HEADS * GDN_D
CONV_DIM = 3 * GDN_WIDTH
CONV_WIDTH = 4
SWA_HEAD_DIM = 64
SWA_Q_HEADS = 8
SWA_KV_HEADS = 2
SWA_GROUP = SWA_Q_HEADS // SWA_KV_HEADS
SWA_WIDTH = SWA_Q_HEADS * SWA_HEAD_DIM
SWA_KV_WIDTH = SWA_KV_HEADS * SWA_HEAD_DIM
WINDOW = 128
ROPE_THETA = 500000.0
ROT_DIM = SWA_HEAD_DIM // 4
ROT_HALF = ROT_DIM // 2
N_EXPERTS = 256
TOP_K = 8
N_GROUPS = 8
GROUP_SIZE = N_EXPERTS // N_GROUPS
TOPK_GROUPS = 4
EXPERT_FF = 256
ROUTED_SCALE = 2.5
ALPHA = 2.0 ** 0.25
LN_EPS = 1e-5
NORM_EPS = 1e-6
PAST_LEN = 4096

LANES = 128
HIST_ROWS = 8
ROW_TILES = D_MODEL // LANES

C_QKV = 0
C_Z = C_QKV + CONV_DIM
C_SQ = C_Z + GDN_WIDTH
C_SK = C_SQ + SWA_WIDTH
C_SV = C_SK + SWA_KV_WIDTH
C_BA = C_SV + SWA_KV_WIDTH
C_END = C_BA + LANES

IN_TILE = 512
MIX_TILE = 256
DEST_TILE = 512
EXPERT_BLOCK = 256
IDX_WINDOW = EXPERT_BLOCK + LANES
GDN_BATCH = 2
SWA_CHUNKS = 2
VMEM_LIMIT = 56 * 1024 * 1024


def _dot(a, b, precision=None):
    return jnp.dot(a, b, preferred_element_type=F32, precision=precision)


def _dot_nt(a, b, precision=None):
    return lax.dot_general(a, b, (((1,), (1,)), ((), ())),
                           preferred_element_type=F32, precision=precision)


def _dot_tn(a, b):
    return lax.dot_general(a, b, (((0,), (0,)), ((), ())), preferred_element_type=F32)


def _split(x):
    hi = x.astype(BF16)
    return hi, (x - hi.astype(F32)).astype(BF16)


def _dot_split(a, b):
    return _dot(a[0], b[0]) + (_dot(a[1], b[0]) + _dot(a[0], b[1]))


def _silu(x):
    return x * jax.nn.sigmoid(x)


def _store_token_tiles(ref, x):
    m = x.shape[0]
    for s in range(ROW_TILES):
        ref[pl.ds(s, m, stride=ROW_TILES), :] = x[:, s * LANES:(s + 1) * LANES]


def _load_token_tiles(ref, m):
    return jnp.concatenate([ref[pl.ds(s, m, stride=ROW_TILES), :] for s in range(ROW_TILES)], 1)


def _in_proj_kernel(x_ref, w_ref, cos_ref, s1_ref, s2_ref,
                    qkv_ref, z_ref, sq_ref, sk_ref, sv_ref, ba_ref):
    xb = x_ref[...].astype(BF16)
    qkv_ref[...] = _dot(xb, w_ref[:, C_QKV:C_Z])
    z_ref[...] = _dot(xb, w_ref[:, C_Z:C_SQ])
    sv_ref[...] = _dot(xb, w_ref[:, C_SV:C_BA])
    ba_ref[...] = _dot(xb, w_ref[:, C_BA:C_END])
    cos = cos_ref[...]
    s1 = s1_ref[...]
    s2 = s2_ref[...]

    def rope(t):
        return (t * cos + pltpu.roll(t, LANES - ROT_HALF, 1) * s1
                + pltpu.roll(t, ROT_HALF, 1) * s2)

    sk_ref[...] = rope(_dot(xb, w_ref[:, C_SK:C_SV]))
    for g in range(SWA_WIDTH // LANES):
        lo = C_SQ + g * LANES
        sq_ref[:, g * LANES:(g + 1) * LANES] = rope(_dot(xb, w_ref[:, lo:lo + LANES])).astype(BF16)


def _in_proj(x2d, w_all, cos, s1, s2):
    n = x2d.shape[0]
    period_tiles = cos.shape[0] // IN_TILE
    row = lambda i: (i, 0)
    tab = lambda i: (i % period_tiles, 0)
    out_shapes = (
        jax.ShapeDtypeStruct((n, CONV_DIM), F32),
        jax.ShapeDtypeStruct((n, GDN_WIDTH), F32),
        jax.ShapeDtypeStruct((n, SWA_WIDTH), BF16),
        jax.ShapeDtypeStruct((n, SWA_KV_WIDTH), F32),
        jax.ShapeDtypeStruct((n, SWA_KV_WIDTH), F32),
        jax.ShapeDtypeStruct((n, LANES), F32),
    )
    return pl.pallas_call(
        _in_proj_kernel,
        grid=(n // IN_TILE,),
        in_specs=[
            pl.BlockSpec((IN_TILE, D_MODEL), row),
            pl.BlockSpec((D_MODEL, C_END), lambda i: (0, 0)),
            pl.BlockSpec((IN_TILE, LANES), tab),
            pl.BlockSpec((IN_TILE, LANES), tab),
            pl.BlockSpec((IN_TILE, LANES), tab),
        ],
        out_specs=(
            pl.BlockSpec((IN_TILE, CONV_DIM), row),
            pl.BlockSpec((IN_TILE, GDN_WIDTH), row),
            pl.BlockSpec((IN_TILE, SWA_WIDTH), row),
            pl.BlockSpec((IN_TILE, SWA_KV_WIDTH), row),
            pl.BlockSpec((IN_TILE, SWA_KV_WIDTH), row),
            pl.BlockSpec((IN_TILE, LANES), row),
        ),
        out_shape=out_shapes,
        compiler_params=pltpu.CompilerParams(
            dimension_semantics=("arbitrary",), vmem_limit_bytes=VMEM_LIMIT),
        name="in_proj",
    )(x2d, w_all, cos, s1, s2)


def _rope_tables(pos):
    p = pos.shape[0]
    inv = ROPE_THETA ** (-jnp.arange(0, ROT_DIM, 2, dtype=F32) / ROT_DIM)
    ang = pos[:, None] * inv[None, :]
    cos = jnp.cos(ang)
    sin = jnp.sin(ang)
    rest = SWA_HEAD_DIM - ROT_DIM
    head_c = jnp.concatenate([cos, cos, jnp.ones((p, rest), F32)], 1)
    head_s1 = jnp.concatenate([-sin, jnp.zeros((p, SWA_HEAD_DIM - ROT_HALF), F32)], 1)
    head_s2 = jnp.concatenate([jnp.zeros((p, ROT_HALF), F32), sin, jnp.zeros((p, rest), F32)], 1)
    two = lambda t: jnp.concatenate([t, t], 1)
    return two(head_c), two(head_s1), two(head_s2)


def _gdn_kernel(L, qkv_ref, z_ref, ba_ref, hist_ref, s0_ref, convw_ref, alog_ref, dtb_ref,
                normw_ref, o_ref, sout_ref, xbuf, state):
    c = pl.program_id(1)
    nbat = qkv_ref.shape[0]
    chains = [(b, h) for b in range(nbat) for h in range(GDN_HEADS)]

    @pl.when(c == 0)
    def _():
        xbuf[:, 0:HIST_ROWS, :] = hist_ref[...]
        state[...] = s0_ref[...]

    xbuf[:, HIST_ROWS:HIST_ROWS + L, :] = qkv_ref[...]

    row = lax.broadcasted_iota(jnp.int32, (L, L), 0)
    col = lax.broadcasted_iota(jnp.int32, (L, L), 1)
    incl = row >= col
    strict = row > col
    lower_ones = incl.astype(BF16)
    eye = (row == col).astype(F32)

    beta_all, g_all = [], []
    for b in range(nbat):
        ba = ba_ref[b]
        beta_all.append(jax.nn.sigmoid(ba))
        sp_in = ba + dtb_ref[...]
        softplus = jnp.maximum(sp_in, 0.0) + jnp.log1p(jnp.exp(-jnp.abs(sp_in)))
        g_all.append(-jnp.exp(alog_ref[...]) * softplus)

    def conv_silu(b, c0):
        acc = None
        for j in range(CONV_WIDTH):
            lo = HIST_ROWS - (CONV_WIDTH - 1) + j
            t = xbuf[b, lo:lo + L, c0:c0 + GDN_D] * convw_ref[j:j + 1, c0:c0 + GDN_D]
            acc = t if acc is None else acc + t
        return _silu(acc)

    def l2n(t):
        return t * lax.rsqrt(jnp.sum(t * t, -1, keepdims=True) + NORM_EPS)

    q = [l2n(conv_silu(b, h * GDN_D)) * (GDN_D ** -0.5) for b, h in chains]
    k = [l2n(conv_silu(b, GDN_WIDTH + h * GDN_D)) for b, h in chains]
    v = [conv_silu(b, 2 * GDN_WIDTH + h * GDN_D) for b, h in chains]
    beta = [beta_all[b][:, h:h + 1] for b, h in chains]
    g = [g_all[b][:, GDN_HEADS + h:GDN_HEADS + h + 1] for b, h in chains]

    def pieces(t):
        p1 = t.astype(BF16).astype(F32)
        r1 = t - p1
        p2 = r1.astype(BF16).astype(F32)
        return p1, p2, r1 - p2

    g_parts = [pieces(t) for t in g]
    diff = [sum(_dot(lower_ones, jnp.where(strict, p, 0.0).astype(BF16)) for p in gp)
            for gp in g_parts]
    gcum = [sum(_dot(lower_ones, jnp.broadcast_to(p, (L, GDN_D)).astype(BF16)) for p in gp)
            for gp in g_parts]
    dec = [jnp.where(incl, jnp.exp(jnp.where(incl, d, 0.0)), 0.0) for d in diff]
    exp_g = [jnp.exp(t) for t in gcum]
    g_end = [t[L - 1:L, :] for t in gcum]
    kb = [t.astype(BF16) for t in k]
    kk = [_dot_nt(t, t) for t in kb]
    aqk = [_dot_nt(qq.astype(BF16), t) * d for qq, t, d in zip(q, kb, dec)]
    a = [jnp.where(strict, d * t, 0.0) * bt for d, t, bt in zip(dec, kk, beta)]

    inv = [eye - t for t in a]
    pw = [_split(-t) for t in a]
    for _ in range(int(math.log2(L)) - 1):
        pw = [_split(_dot_split(p, p)) for p in pw]
        inv = [t + _dot_split(_split(t), p) for t, p in zip(inv, pw)]
    inv_s = [_split(t) for t in inv]
    w = [_dot_split(t, _split(bt * vv)) for t, bt, vv in zip(inv_s, beta, v)]
    y = [_dot_split(t, _split((bt * eg) * kk_)) for t, bt, eg, kk_ in zip(inv_s, beta, exp_g, k)]

    s = [state[b, h] for b, h in chains]
    sb = [t.astype(BF16) for t in s]
    u = [ww - _dot(yy.astype(BF16), t) for ww, yy, t in zip(w, y, sb)]
    ub = [t.astype(BF16) for t in u]
    o = [_dot((eg * qq).astype(BF16), t) + _dot(aa.astype(BF16), uu)
         for eg, qq, t, aa, uu in zip(exp_g, q, sb, aqk, ub)]
    k_dec = [(jnp.exp(ge - gc) * kk_).astype(BF16) for ge, gc, kk_ in zip(g_end, gcum, k)]
    s_new = [jnp.exp(ge) * t + _dot_tn(kd, uu) for ge, t, kd, uu in zip(g_end, s, k_dec, ub)]

    for (b, h), t, oo in zip(chains, s_new, o):
        state[b, h] = t
        on = oo * lax.rsqrt(jnp.mean(oo * oo, -1, keepdims=True) + NORM_EPS)
        zz = z_ref[b, :, h * GDN_D:(h + 1) * GDN_D]
        o_ref[b, :, h * GDN_D:(h + 1) * GDN_D] = (on * normw_ref[...] * _silu(zz)).astype(BF16)

    xbuf[:, 0:HIST_ROWS, :] = xbuf[:, L:L + HIST_ROWS, :]

    @pl.when(c == pl.num_programs(1) - 1)
    def _():
        sout_ref[...] = state[...]


def _gdn(L, qkv, z, ba, hist8, s0, conv_w, alog_pad, dtb_pad, norm_w):
    b, t, _ = qkv.shape
    nc = t // L
    nbat = GDN_BATCH
    tok = lambda i, c: (i, c, 0)
    per_b3 = lambda i, c: (i, 0, 0)
    per_b4 = lambda i, c: (i, 0, 0, 0)
    const2 = lambda i, c: (0, 0)
    return pl.pallas_call(
        functools.partial(_gdn_kernel, L),
        grid=(b // nbat, nc),
        in_specs=[
            pl.BlockSpec((nbat, L, CONV_DIM), tok),
            pl.BlockSpec((nbat, L, GDN_WIDTH), tok),
            pl.BlockSpec((nbat, L, LANES), tok),
            pl.BlockSpec((nbat, HIST_ROWS, CONV_DIM), per_b3),
            pl.BlockSpec((nbat, GDN_HEADS, GDN_D, GDN_D), per_b4),
            pl.BlockSpec((CONV_WIDTH, CONV_DIM), const2),
            pl.BlockSpec((1, LANES), const2),
            pl.BlockSpec((1, LANES), const2),
            pl.BlockSpec((1, GDN_D), const2),
        ],
        out_specs=(
            pl.BlockSpec((nbat, L, GDN_WIDTH), tok),
            pl.BlockSpec((nbat, GDN_HEADS, GDN_D, GDN_D), per_b4),
        ),
        out_shape=(
            jax.ShapeDtypeStruct((b, t, GDN_WIDTH), BF16),
            jax.ShapeDtypeStruct((b, GDN_HEADS, GDN_D, GDN_D), F32),
        ),
        scratch_shapes=[
            pltpu.VMEM((nbat, HIST_ROWS + L, CONV_DIM), F32),
            pltpu.VMEM((nbat, GDN_HEADS, GDN_D, GDN_D), F32),
        ],
        compiler_params=pltpu.CompilerParams(
            dimension_semantics=("arbitrary", "arbitrary"), vmem_limit_bytes=VMEM_LIMIT),
        name="gdn",
    )(qkv, z, ba, hist8, s0, conv_w, alog_pad, dtb_pad, norm_w)


def _swa_kernel(L, pos0, q_ref, k_ref, v_ref, sink_ref, o_ref):
    c = pl.program_id(1)
    nch = q_ref.shape[1] // L
    span = WINDOW + L
    rows = WINDOW + nch * L
    start = pl.multiple_of(c * (nch * L), L)
    kx = k_ref[0, pl.ds(start, rows), :]
    vx = v_ref[0, pl.ds(start, rows), :]
    kx_sw = pltpu.roll(kx, SWA_HEAD_DIM, 1)
    vx_sw = pltpu.roll(vx, SWA_HEAD_DIM, 1)
    low = lax.broadcasted_iota(jnp.int32, (rows, LANES), 1) < SWA_HEAD_DIM

    def halves(x, x_sw, kh):
        src_lo, src_hi = (x, x_sw) if kh == 0 else (x_sw, x)
        return (jnp.where(low, src_lo, 0.0).astype(BF16), jnp.where(low, 0.0, src_hi).astype(BF16))

    k_halves = [halves(kx, kx_sw, kh) for kh in range(SWA_KV_HEADS)]
    v_halves = [halves(vx, vx_sw, kh) for kh in range(SWA_KV_HEADS)]
    sinks = sink_ref[...]
    top_rows = lax.broadcasted_iota(jnp.int32, (2 * L, 1), 0) < L
    first_pos = pos0 - WINDOW + c * (nch * L)

    chains = [(u, kh, half) for u in range(nch) for kh in range(SWA_KV_HEADS) for half in range(2)]
    valid = [first_pos + u * L + lax.broadcasted_iota(jnp.int32, (1, span), 1) >= 0
             for u in range(nch)]
    q4 = {}
    for u in range(nch):
        for kh in range(SWA_KV_HEADS):
            c0 = kh * 2 * LANES
            q4[u, kh] = jnp.concatenate([q_ref[0, u * L:(u + 1) * L, c0:c0 + LANES],
                                         q_ref[0, u * L:(u + 1) * L, c0 + LANES:c0 + 2 * LANES]], 0)
    scores = [jnp.where(valid[u],
                        _dot_nt(q4[u, kh], k_halves[kh][half][u * L:u * L + span])
                        * (SWA_HEAD_DIM ** -0.5), -jnp.inf)
              for u, kh, half in chains]
    sink_cols = [jnp.where(top_rows, sinks[:, kh * SWA_GROUP + half:kh * SWA_GROUP + half + 1],
                           sinks[:, kh * SWA_GROUP + half + 2:kh * SWA_GROUP + half + 3])
                 for _, kh, half in chains]
    m = [jnp.maximum(jnp.max(s, -1, keepdims=True), sk) for s, sk in zip(scores, sink_cols)]
    p = [jnp.exp(s - mm) for s, mm in zip(scores, m)]
    den = [jnp.sum(pp, -1, keepdims=True) + jnp.exp(sk - mm) for pp, sk, mm in zip(p, sink_cols, m)]
    pv = [_dot(pp.astype(BF16), v_halves[kh][half][u * L:u * L + span]) / dd
          for pp, dd, (u, kh, half) in zip(p, den, chains)]
    for n in range(0, len(chains), 2):
        u, kh, _ = chains[n]
        o = pv[n] + pv[n + 1]
        c0 = kh * 2 * LANES
        o_ref[0, u * L:(u + 1) * L, c0:c0 + LANES] = o[0:L].astype(BF16)
        o_ref[0, u * L:(u + 1) * L, c0 + LANES:c0 + 2 * LANES] = o[L:2 * L].astype(BF16)


def _swa(L, pos0, q, k_ext, v_ext, sinks_pad):
    b, t, _ = q.shape
    ext = k_ext.shape[1]
    nch = SWA_CHUNKS if (t // L) % SWA_CHUNKS == 0 else 1
    return pl.pallas_call(
        functools.partial(_swa_kernel, L, pos0),
        grid=(b, t // (nch * L)),
        in_specs=[
            pl.BlockSpec((1, nch * L, SWA_WIDTH), lambda i, c: (i, c, 0)),
            pl.BlockSpec((1, ext, SWA_KV_WIDTH), lambda i, c: (i, 0, 0)),
            pl.BlockSpec((1, ext, SWA_KV_WIDTH), lambda i, c: (i, 0, 0)),
            pl.BlockSpec((1, LANES), lambda i, c: (0, 0)),
        ],
        out_specs=pl.BlockSpec((1, nch * L, SWA_WIDTH), lambda i, c: (i, c, 0)),
        out_shape=jax.ShapeDtypeStruct((b, t, SWA_WIDTH), BF16),
        compiler_params=pltpu.CompilerParams(
            dimension_semantics=("arbitrary", "arbitrary"), vmem_limit_bytes=VMEM_LIMIT),
        name="swa",
    )(q, k_ext, v_ext, sinks_pad)


def _layer_norm(r, g, b):
    mu = jnp.mean(r, -1, keepdims=True)
    d = r - mu
    var = jnp.mean(d * d, -1, keepdims=True)
    return d * lax.rsqrt(var + LN_EPS) * g + b


def _route(x1b, wrt_ref, rbias_ref, count_ref):
    tm = x1b.shape[0]
    scores = jax.nn.sigmoid(_dot_nt(wrt_ref[...], x1b))
    biased = scores + rbias_ref[...]
    neg = -jnp.inf
    r_g = lax.broadcasted_iota(jnp.int32, (GROUP_SIZE, tm), 0)
    blocks = []
    group_score = []
    for g in range(N_GROUPS):
        blk = biased[g * GROUP_SIZE:(g + 1) * GROUP_SIZE]
        m1 = jnp.max(blk, 0, keepdims=True)
        i1 = jnp.min(jnp.where(blk == m1, r_g, GROUP_SIZE), 0, keepdims=True)
        m2 = jnp.max(jnp.where(r_g == i1, neg, blk), 0, keepdims=True)
        blocks.append(blk)
        group_score.append(m1 + m2)
    masked = []
    for g in range(N_GROUPS):
        rank = jnp.zeros((1, tm), jnp.int32)
        for o in range(N_GROUPS):
            if o == g:
                continue
            ahead = group_score[o] > group_score[g]
            if o < g:
                ahead = ahead | (group_score[o] == group_score[g])
            rank = rank + ahead.astype(jnp.int32)
        masked.append(jnp.where(rank < TOPK_GROUPS, blocks[g], neg))
    masked = jnp.concatenate(masked, 0)
    r_e = lax.broadcasted_iota(jnp.int32, (N_EXPERTS, tm), 0)
    idx_rows = []
    w_rows = []
    hits = []
    for _ in range(TOP_K):
        m = jnp.max(masked, 0, keepdims=True)
        idx = jnp.min(jnp.where(masked == m, r_e, N_EXPERTS), 0, keepdims=True)
        hit = r_e == idx
        w_rows.append(jnp.sum(jnp.where(hit, scores, 0.0), 0, keepdims=True))
        masked = jnp.where(hit, neg, masked)
        idx_rows.append(idx)
        hits.append(hit)
    total = w_rows[0]
    for wr in w_rows[1:]:
        total = total + wr
    top_w = jnp.concatenate(w_rows, 0) / total * ROUTED_SCALE

    chosen = jnp.zeros((N_EXPERTS, tm), F32)
    for hit in hits:
        chosen = chosen + hit.astype(F32)
    t_r = lax.broadcasted_iota(jnp.int32, (tm, tm), 0)
    t_c = lax.broadcasted_iota(jnp.int32, (tm, tm), 1)
    before = (t_r < t_c).astype(BF16)
    prior = _dot(chosen.astype(BF16), before) + count_ref[...]
    rank_rows = [jnp.sum(jnp.where(hit, prior, 0.0), 0, keepdims=True) for hit in hits]
    count_ref[...] = count_ref[...] + jnp.sum(chosen, 1, keepdims=True)
    rank = jnp.concatenate(rank_rows, 0).astype(jnp.int32)
    return jnp.concatenate(idx_rows, 0), top_w, rank


def _mix_kernel(np_tiles, gp_ref, sp_ref, xp_ref, gs_ref, ss_ref, xs_ref, wo_ref, g1_ref, b1_ref,
                wrt_ref, rbias_ref, x1_ref, tope_ref, topw_ref, rank_ref, cnt_ref, count_acc):
    i = pl.program_id(0)

    @pl.when(i == 0)
    def _():
        count_acc[...] = jnp.zeros_like(count_acc)

    is_prompt = i < np_tiles
    gd = jnp.where(is_prompt, gp_ref[...], gs_ref[...])
    sw = jnp.where(is_prompt, sp_ref[...], ss_ref[...])
    x = jnp.where(is_prompt, xp_ref[...], xs_ref[...])
    mix = _dot(gd, wo_ref[0:GDN_WIDTH, :]) + _dot(sw, wo_ref[GDN_WIDTH:, :])
    x1 = _layer_norm(ALPHA * x + mix, g1_ref[...], b1_ref[...])
    _store_token_tiles(x1_ref, x1)
    top_e, top_w, rank = _route(x1.astype(BF16), wrt_ref, rbias_ref, count_acc)
    tope_ref[...] = top_e
    topw_ref[...] = top_w
    rank_ref[...] = rank
    cnt_ref[...] = count_acc[...].astype(jnp.int32)


def _mix(gd_p, sw_p, x_p, gd_s, sw_s, x_s, wo, g1, b1, wrt, rbias):
    n_p, n_s = x_p.shape[0], x_s.shape[0]
    np_tiles = n_p // MIX_TILE
    nt = n_p + n_s
    pm = lambda i: (jnp.minimum(i, np_tiles - 1), 0)
    sm = lambda i: (jnp.maximum(i - np_tiles, 0), 0)
    const = lambda i: (0, 0)
    col = lambda i: (0, i)
    return pl.pallas_call(
        functools.partial(_mix_kernel, np_tiles),
        grid=(nt // MIX_TILE,),
        in_specs=[
            pl.BlockSpec((MIX_TILE, GDN_WIDTH), pm),
            pl.BlockSpec((MIX_TILE, SWA_WIDTH), pm),
            pl.BlockSpec((MIX_TILE, D_MODEL), pm),
            pl.BlockSpec((MIX_TILE, GDN_WIDTH), sm),
            pl.BlockSpec((MIX_TILE, SWA_WIDTH), sm),
            pl.BlockSpec((MIX_TILE, D_MODEL), sm),
            pl.BlockSpec((D_MODEL, D_MODEL), const),
            pl.BlockSpec((1, D_MODEL), const),
            pl.BlockSpec((1, D_MODEL), const),
            pl.BlockSpec((N_EXPERTS, D_MODEL), const),
            pl.BlockSpec((N_EXPERTS, 1), const),
        ],
        out_specs=(
            pl.BlockSpec((MIX_TILE * ROW_TILES, LANES), lambda i: (i, 0)),
            pl.BlockSpec((TOP_K, MIX_TILE), col),
            pl.BlockSpec((TOP_K, MIX_TILE), col),
            pl.BlockSpec((TOP_K, MIX_TILE), col),
            pl.BlockSpec((N_EXPERTS, 1), const),
        ),
        out_shape=(
            jax.ShapeDtypeStruct((nt * ROW_TILES, LANES), F32),
            jax.ShapeDtypeStruct((TOP_K, nt), jnp.int32),
            jax.ShapeDtypeStruct((TOP_K, nt), F32),
            jax.ShapeDtypeStruct((TOP_K, nt), jnp.int32),
            jax.ShapeDtypeStruct((N_EXPERTS, 1), jnp.int32),
        ),
        scratch_shapes=[pltpu.VMEM((N_EXPERTS, 1), F32)],
        compiler_params=pltpu.CompilerParams(
            dimension_semantics=("arbitrary",), vmem_limit_bytes=VMEM_LIMIT),
        name="mix",
    )(gd_p, sw_p, x_p, gd_s, sw_s, x_s, wo, g1, b1, wrt, rbias)


def _dest_kernel(tope_ref, rank_ref, pstart_ref, dest_ref):
    tm = tope_ref.shape[1]
    r_e = lax.broadcasted_iota(jnp.int32, (N_EXPERTS, tm), 0)
    pstart = pstart_ref[...]
    top_e = tope_ref[...]
    rows = [jnp.sum(jnp.where(r_e == top_e[j:j + 1, :], pstart, 0), 0, keepdims=True)
            for j in range(TOP_K)]
    dest_ref[...] = (jnp.concatenate(rows, 0) + rank_ref[...]) * ROW_TILES


def _dest(top_e_t, rank_t, pstart):
    nt = top_e_t.shape[1]
    col = lambda i: (0, i)
    return pl.pallas_call(
        _dest_kernel,
        grid=(nt // DEST_TILE,),
        in_specs=[
            pl.BlockSpec((TOP_K, DEST_TILE), col),
            pl.BlockSpec((TOP_K, DEST_TILE), col),
            pl.BlockSpec((N_EXPERTS, 1), lambda i: (0, 0)),
        ],
        out_specs=pl.BlockSpec((TOP_K, DEST_TILE), col),
        out_shape=jax.ShapeDtypeStruct((TOP_K, nt), jnp.int32),
        compiler_params=pltpu.CompilerParams(dimension_semantics=("arbitrary",)),
        name="dest",
    )(top_e_t, rank_t, pstart)


def _experts_kernel(be_ref, nused_ref, c0_ref, stok_ref, x1_ref, wg_ref, wu_ref, wd_ref, y_ref,
                    idx0, idx1, xbuf0, xbuf1, idx_sem, row_sem, wg_b, wu_b, wd_b):
    i = pl.program_id(0)
    n_used = nused_ref[0]
    idx = (idx0, idx1)
    xbuf = (xbuf0, xbuf1)

    def idx_copy(block, s):
        base = pl.multiple_of(c0_ref[block] // LANES * LANES, LANES)
        return pltpu.make_async_copy(stok_ref.at[pl.ds(base, IDX_WINDOW)], idx[s], idx_sem.at[s])

    def start_rows(block, s):
        off = lax.rem(c0_ref[block], LANES)
        for r in range(EXPERT_BLOCK):
            src = pl.multiple_of(idx[s][off + r], ROW_TILES)
            pltpu.make_async_copy(x1_ref.at[pl.ds(src, ROW_TILES), :],
                                  xbuf[s].at[pl.ds(r * ROW_TILES, ROW_TILES), :],
                                  row_sem.at[s]).start(priority=r % 2)

    def wait_rows(s):
        pltpu.make_async_copy(x1_ref.at[pl.ds(0, EXPERT_BLOCK * ROW_TILES), :], xbuf[s],
                              row_sem.at[s]).wait()

    @pl.when(i == 0)
    def _():
        idx_copy(0, 0).start()
        idx_copy(0, 0).wait()
        start_rows(0, 0)
        idx_copy(1, 1).start()

    prev = be_ref[jnp.maximum(i - 1, 0)]

    @pl.when((i == 0) | (be_ref[i] != prev))
    def _():
        wg_b[...] = wg_ref[0].astype(BF16)
        wu_b[...] = wu_ref[0].astype(BF16)
        wd_b[...] = wd_ref[0].astype(BF16)

    for slot in range(2):
        other = 1 - slot

        @pl.when((i < n_used) & (lax.rem(i, 2) == slot))
        def _():
            idx_copy(i + 1, other).wait()
            wait_rows(slot)
            start_rows(i + 1, other)
            idx_copy(i + 2, slot).start()
            xb = _load_token_tiles(xbuf[slot], EXPERT_BLOCK).astype(BF16)
            hb = _silu(_dot(xb, wg_b[...])) * _dot(xb, wu_b[...])
            _store_token_tiles(y_ref, _dot(hb.astype(BF16), wd_b[...]))

        @pl.when((i == n_used) & (lax.rem(i, 2) == slot))
        def _():
            wait_rows(slot)
            idx_copy(i + 1, other).wait()

    @pl.when(i >= n_used)
    def _():
        y_ref[...] = jnp.zeros_like(y_ref)


def _experts(block_e, n_used, block_c0, stok, x1, w_gate_e, w_up_e, w_down_e):
    nb = block_e.shape[0]
    grid_spec = pltpu.PrefetchScalarGridSpec(
        num_scalar_prefetch=3,
        grid=(nb,),
        in_specs=[
            pl.BlockSpec(memory_space=pl.ANY),
            pl.BlockSpec(memory_space=pl.ANY),
            pl.BlockSpec((1, D_MODEL, EXPERT_FF), lambda i, be, nu, c0: (be[i], 0, 0)),
            pl.BlockSpec((1, D_MODEL, EXPERT_FF), lambda i, be, nu, c0: (be[i], 0, 0)),
            pl.BlockSpec((1, EXPERT_FF, D_MODEL), lambda i, be, nu, c0: (be[i], 0, 0)),
        ],
        out_specs=pl.BlockSpec((EXPERT_BLOCK * ROW_TILES, LANES), lambda i, be, nu, c0: (i, 0)),
        scratch_shapes=[
            pltpu.SMEM((IDX_WINDOW,), jnp.int32),
            pltpu.SMEM((IDX_WINDOW,), jnp.int32),
            pltpu.VMEM((EXPERT_BLOCK * ROW_TILES, LANES), F32),
            pltpu.VMEM((EXPERT_BLOCK * ROW_TILES, LANES), F32),
            pltpu.SemaphoreType.DMA((2,)),
            pltpu.SemaphoreType.DMA((2,)),
            pltpu.VMEM((D_MODEL, EXPERT_FF), BF16),
            pltpu.VMEM((D_MODEL, EXPERT_FF), BF16),
            pltpu.VMEM((EXPERT_FF, D_MODEL), BF16),
        ],
    )
    return pl.pallas_call(
        _experts_kernel,
        grid_spec=grid_spec,
        out_shape=jax.ShapeDtypeStruct((nb * EXPERT_BLOCK * ROW_TILES, LANES), F32),
        compiler_params=pltpu.CompilerParams(
            dimension_semantics=("arbitrary",), vmem_limit_bytes=VMEM_LIMIT),
        name="experts",
    )(block_e, n_used, block_c0, stok, x1, w_gate_e, w_up_e, w_down_e)


def _final_kernel(np_tiles, dest_ref, dest_next_ref, x1_ref, topw_ref, yb_ref, wsg_ref, wsu_ref,
                  wsd_ref, g2_ref, b2_ref, yp_ref, ys_ref, ybuf, sem):
    i = pl.program_id(0)
    slot = lax.rem(i, 2)

    def issue_rows(d_ref, s):
        def body(t, carry):
            dst = pl.multiple_of(t * ROW_TILES, ROW_TILES)
            for j in range(TOP_K):
                src = pl.multiple_of(d_ref[j, t], ROW_TILES)
                pltpu.make_async_copy(yb_ref.at[pl.ds(src, ROW_TILES), :],
                                      ybuf.at[s, j, pl.ds(dst, ROW_TILES), :],
                                      sem.at[s]).start(priority=j % 2)
            return carry
        lax.fori_loop(0, MIX_TILE, body, 0)

    @pl.when(i == 0)
    def _():
        issue_rows(dest_ref, 0)

    @pl.when(i + 1 < pl.num_programs(0))
    def _():
        issue_rows(dest_next_ref, 1 - slot)

    x1 = _load_token_tiles(x1_ref, MIX_TILE)
    xb = x1.astype(BF16)
    hs = _silu(_dot(xb, wsg_ref[...])) * _dot(xb, wsu_ref[...])
    ff = _dot(hs.astype(BF16), wsd_ref[...])
    r = lax.broadcasted_iota(jnp.int32, (MIX_TILE, MIX_TILE), 0)
    c = lax.broadcasted_iota(jnp.int32, (MIX_TILE, MIX_TILE), 1)
    w_cols = _dot_nt((r == c).astype(F32), topw_ref[...], HIGHEST)
    for j in range(TOP_K):
        pltpu.make_async_copy(yb_ref.at[pl.ds(0, MIX_TILE * ROW_TILES), :], ybuf.at[slot, j],
                              sem.at[slot]).wait()
    for j in range(TOP_K):
        ff = ff + w_cols[:, j:j + 1] * _load_token_tiles(ybuf.at[slot, j], MIX_TILE)
    y = _layer_norm(ALPHA * x1 + ff, g2_ref[...], b2_ref[...])

    @pl.when(i < np_tiles)
    def _():
        yp_ref[...] = y

    @pl.when(i >= np_tiles)
    def _():
        ys_ref[...] = y


def _final(n_p, dest, x1, top_w_t, yb, wsg, wsu, wsd, g2, b2):
    nt = x1.shape[0] // ROW_TILES
    np_tiles = n_p // MIX_TILE
    row = lambda i: (i, 0)
    const = lambda i: (0, 0)
    n_tiles = nt // MIX_TILE
    return pl.pallas_call(
        functools.partial(_final_kernel, np_tiles),
        grid=(n_tiles,),
        in_specs=[
            pl.BlockSpec((TOP_K, MIX_TILE), lambda i: (0, i), memory_space=pltpu.SMEM),
            pl.BlockSpec((TOP_K, MIX_TILE), lambda i: (0, jnp.minimum(i + 1, n_tiles - 1)),
                         memory_space=pltpu.SMEM),
            pl.BlockSpec((MIX_TILE * ROW_TILES, LANES), row),
            pl.BlockSpec((TOP_K, MIX_TILE), lambda i: (0, i)),
            pl.BlockSpec(memory_space=pl.ANY),
            pl.BlockSpec((D_MODEL, EXPERT_FF), const),
            pl.BlockSpec((D_MODEL, EXPERT_FF), const),
            pl.BlockSpec((EXPERT_FF, D_MODEL), const),
            pl.BlockSpec((1, D_MODEL), const),
            pl.BlockSpec((1, D_MODEL), const),
        ],
        out_specs=(
            pl.BlockSpec((MIX_TILE, D_MODEL), lambda i: (jnp.minimum(i, np_tiles - 1), 0)),
            pl.BlockSpec((MIX_TILE, D_MODEL), lambda i: (jnp.maximum(i - np_tiles, 0), 0)),
        ),
        out_shape=(
            jax.ShapeDtypeStruct((n_p, D_MODEL), F32),
            jax.ShapeDtypeStruct((nt - n_p, D_MODEL), F32),
        ),
        scratch_shapes=[
            pltpu.VMEM((2, TOP_K, MIX_TILE * ROW_TILES, LANES), F32),
            pltpu.SemaphoreType.DMA((2,)),
        ],
        compiler_params=pltpu.CompilerParams(
            dimension_semantics=("arbitrary",), vmem_limit_bytes=VMEM_LIMIT),
        name="final",
    )(dest, dest, x1, top_w_t, yb, wsg, wsu, wsd, g2, b2)


def _expert_layout(counts, nk):
    nb = -(-nk // EXPERT_BLOCK) + N_EXPERTS
    padded = (counts + EXPERT_BLOCK - 1) // EXPERT_BLOCK * EXPERT_BLOCK
    pend = jnp.cumsum(padded)
    pstart = (pend - padded).astype(jnp.int32)
    block_start = jnp.arange(nb, dtype=jnp.int32) * EXPERT_BLOCK
    block_e = jnp.sum((pend[None, :] <= block_start[:, None]).astype(jnp.int32), 1)
    block_e = jnp.minimum(block_e, N_EXPERTS - 1)
    n_used = (pend[-1] // EXPERT_BLOCK).astype(jnp.int32).reshape(1)
    start = (jnp.cumsum(counts) - counts).astype(jnp.int32)
    block_c0 = start[block_e] + block_start - pstart[block_e]
    block_c0 = jnp.where(block_start < pend[-1], block_c0, 0)
    block_c0 = jnp.concatenate([block_c0, jnp.zeros((2,), jnp.int32)]).astype(jnp.int32)
    return pstart, block_e, n_used, block_c0


def _pad_lanes(v, offset=0):
    out = jnp.zeros((1, LANES), F32)
    return out.at[0, offset:offset + v.shape[0]].set(v.astype(F32))


def kernel(x_prompt, x_sample, state_conv, state_gdn, cache_swa_k, cache_swa_v, w_in, conv_w, a_log, dt_bias, gdn_norm_w, attn_sinks, w_o, ln1_g, ln1_b, w_router, router_bias, w_gate_e, w_up_e, w_down_e, w_shared_gate, w_shared_up, w_shared_down, ln2_g, ln2_b):
    bp, tp, d = x_prompt.shape
    bs, ts, _ = x_sample.shape
    n_p, n_s = bp * tp, bs * ts

    wi = w_in[0]
    o1 = CONV_DIM
    o2 = o1 + GDN_WIDTH
    o4 = o2 + 2 * GDN_HEADS
    o5 = o4 + SWA_WIDTH
    o6 = o5 + SWA_KV_WIDTH
    ba_cols = jnp.zeros((d, LANES), F32).at[:, :2 * GDN_HEADS].set(wi[:, o2:o4])
    w_all = jnp.concatenate([wi[:, :o2], wi[:, o4:o5], wi[:, o5:o6], wi[:, o6:], ba_cols], 1).astype(BF16)
    wo = w_o[0].astype(BF16)
    wrt = w_router[0].T.astype(BF16)
    rbias = router_bias[0].astype(F32).reshape(N_EXPERTS, 1)
    alog_pad = _pad_lanes(a_log[0], GDN_HEADS)
    dtb_pad = _pad_lanes(dt_bias[0], GDN_HEADS)
    sinks_pad = _pad_lanes(attn_sinks[0])
    norm_w = gdn_norm_w[0].reshape(1, GDN_D)
    g1, b1 = ln1_g[0].reshape(1, d), ln1_b[0].reshape(1, d)
    g2, b2 = ln2_g[0].reshape(1, d), ln2_b[0].reshape(1, d)

    def front(x, pos0, L, conv_hist, s0, k_hist, v_hist):
        b, t, _ = x.shape
        period = max(t, IN_TILE)
        pos = pos0 + (jnp.arange(period, dtype=jnp.int32) % t).astype(F32)
        cos, s1, s2 = _rope_tables(pos)
        x2d = x.reshape(b * t, d)
        qkv, z, sq, sk, sv, ba = _in_proj(x2d, w_all, cos, s1, s2)
        qkv = qkv.reshape(b, t, CONV_DIM)
        hist8 = jnp.concatenate(
            [jnp.zeros((b, HIST_ROWS - (CONV_WIDTH - 1), CONV_DIM), F32), conv_hist], 1)
        gd, s_new = _gdn(L, qkv, z.reshape(b, t, GDN_WIDTH), ba.reshape(b, t, LANES), hist8, s0,
                         conv_w[0], alog_pad, dtb_pad, norm_w)
        k_ext = jnp.concatenate([k_hist.reshape(b, WINDOW, SWA_KV_WIDTH),
                                 sk.reshape(b, t, SWA_KV_WIDTH)], 1)
        v_ext = jnp.concatenate([v_hist.reshape(b, WINDOW, SWA_KV_WIDTH),
                                 sv.reshape(b, t, SWA_KV_WIDTH)], 1)
        sw = _swa(L, pos0, sq.reshape(b, t, SWA_WIDTH), k_ext, v_ext, sinks_pad)
        conv_new = jnp.concatenate([conv_hist, qkv], 1)[:, -(CONV_WIDTH - 1):]
        k_new = k_ext[:, -WINDOW:].reshape(b, WINDOW, SWA_KV_HEADS, SWA_HEAD_DIM)
        v_new = v_ext[:, -WINDOW:].reshape(b, WINDOW, SWA_KV_HEADS, SWA_HEAD_DIM)
        return (x2d, gd.reshape(b * t, GDN_WIDTH), sw.reshape(b * t, SWA_WIDTH),
                conv_new, s_new, k_new, v_new)

    zeros = lambda *s: jnp.zeros(s, F32)
    xp2, gd_p, sw_p, conv_p, gdn_p, k_p, v_p = front(
        x_prompt, 0, CHUNK, zeros(bp, CONV_WIDTH - 1, CONV_DIM),
        zeros(bp, GDN_HEADS, GDN_D, GDN_D), zeros(bp, WINDOW, SWA_KV_WIDTH),
        zeros(bp, WINDOW, SWA_KV_WIDTH))
    xs2, gd_s, sw_s, conv_s, gdn_s, k_s, v_s = front(
        x_sample, PAST_LEN, ts, state_conv[0], state_gdn[0], cache_swa_k[0], cache_swa_v[0])

    x1, top_e_t, top_w_t, rank_t, counts = _mix(
        gd_p, sw_p, xp2, gd_s, sw_s, xs2, wo, g1, b1, wrt, rbias)

    pstart, block_e, n_used, block_c0 = _expert_layout(counts[:, 0], TOP_K * (n_p + n_s))
    dest = _dest(top_e_t, rank_t, pstart.reshape(N_EXPERTS, 1))
    order = jnp.argsort(top_e_t.T.reshape(-1))
    stok = jnp.concatenate([(order // TOP_K * ROW_TILES).astype(jnp.int32),
                            jnp.zeros((IDX_WINDOW,), jnp.int32)])
    yb = _experts(block_e, n_used, block_c0, stok, x1, w_gate_e[0], w_up_e[0], w_down_e[0])

    y_p, y_s = _final(n_p, dest, x1, top_w_t, yb,
                      w_shared_gate[0].astype(BF16), w_shared_up[0].astype(BF16),
                      w_shared_down[0].astype(BF16), g2, b2)
    return (y_p.reshape(bp, tp, d), y_s.reshape(bs, ts, d),
            conv_p[None], gdn_p[None], k_p[None], v_p[None],
            conv_s[None], gdn_s[None], k_s[None], v_s[None])
```

```python
import functools
import math

import jax
import jax.numpy as jnp
from jax import lax
from jax.experimental import pallas as pl
from jax.experimental.pallas import tpu as pltpu

F32 = jnp.float32
BF16 = jnp.bfloat16
HIGHEST = lax.Precision.HIGHEST

D_MODEL = 1024
CHUNK = 64
GDN_HEADS = 4
GDN_D = 128
GDN_WIDTH = GDN_HEADS * GDN_D
CONV_DIM = 3 * GDN_WIDTH
CONV_WIDTH = 4
SWA_HEAD_DIM = 64
SWA_Q_HEADS = 8
SWA_KV_HEADS = 2
SWA_GROUP = SWA_Q_HEADS // SWA_KV_HEADS
SWA_WIDTH = SWA_Q_HEADS * SWA_HEAD_DIM
SWA_KV_WIDTH = SWA_KV_HEADS * SWA_HEAD_DIM
WINDOW = 128
ROPE_THETA = 500000.0
ROT_DIM = SWA_HEAD_DIM // 4
ROT_HALF = ROT_DIM // 2
N_EXPERTS = 256
TOP_K = 8
N_GROUPS = 8
GROUP_SIZE = N_EXPERTS // N_GROUPS
TOPK_GROUPS = 4
EXPERT_FF = 256
ROUTED_SCALE = 2.5
ALPHA = 2.0 ** 0.25
LN_EPS = 1e-5
NORM_EPS = 1e-6
PAST_LEN = 4096

LANES = 128
HIST_ROWS = 8
ROW_TILES = D_MODEL // LANES

C_QKV = 0
C_Z = C_QKV + CONV_DIM
C_SQ = C_Z + GDN_WIDTH
C_SK = C_SQ + SWA_WIDTH
C_SV = C_SK + SWA_KV_WIDTH
C_BA = C_SV + SWA_KV_WIDTH
C_END = C_BA + LANES

IN_TILE = 512
MIX_TILE = 256
DEST_TILE = 512
EXPERT_BLOCK = 256
IDX_WINDOW = EXPERT_BLOCK + LANES
GDN_BATCH = 2
SWA_CHUNKS = 2
VMEM_LIMIT = 56 * 1024 * 1024


def _dot(a, b, precision=None):
    return jnp.dot(a, b, preferred_element_type=F32, precision=precision)


def _dot_nt(a, b, precision=None):
    return lax.dot_general(a, b, (((1,), (1,)), ((), ())),
                           preferred_element_type=F32, precision=precision)


def _dot_tn(a, b):
    return lax.dot_general(a, b, (((0,), (0,)), ((), ())), preferred_element_type=F32)


def _split(x):
    hi = x.astype(BF16)
    return hi, (x - hi.astype(F32)).astype(BF16)


def _dot_split(a, b):
    return _dot(a[0], b[0]) + (_dot(a[1], b[0]) + _dot(a[0], b[1]))


def _silu(x):
    return x * jax.nn.sigmoid(x)


def _store_token_tiles(ref, x):
    m = x.shape[0]
    for s in range(ROW_TILES):
        ref[pl.ds(s, m, stride=ROW_TILES), :] = x[:, s * LANES:(s + 1) * LANES]


def _load_token_tiles(ref, m):
    return jnp.concatenate([ref[pl.ds(s, m, stride=ROW_TILES), :] for s in range(ROW_TILES)], 1)


def _in_proj_kernel(x_ref, w_ref, cos_ref, s1_ref, s2_ref,
                    qkv_ref, z_ref, sq_ref, sk_ref, sv_ref, ba_ref):
    xb = x_ref[...].astype(BF16)
    qkv_ref[...] = _dot(xb, w_ref[:, C_QKV:C_Z])
    z_ref[...] = _dot(xb, w_ref[:, C_Z:C_SQ])
    sv_ref[...] = _dot(xb, w_ref[:, C_SV:C_BA])
    ba_ref[...] = _dot(xb, w_ref[:, C_BA:C_END])
    cos = cos_ref[...]
    s1 = s1_ref[...]
    s2 = s2_ref[...]

    def rope(t):
        return (t * cos + pltpu.roll(t, LANES - ROT_HALF, 1) * s1
                + pltpu.roll(t, ROT_HALF, 1) * s2)

    sk_ref[...] = rope(_dot(xb, w_ref[:, C_SK:C_SV]))
    for g in range(SWA_WIDTH // LANES):
        lo = C_SQ + g * LANES
        sq_ref[:, g * LANES:(g + 1) * LANES] = rope(_dot(xb, w_ref[:, lo:lo + LANES])).astype(BF16)


def _in_proj(x2d, w_all, cos, s1, s2):
    n = x2d.shape[0]
    period_tiles = cos.shape[0] // IN_TILE
    row = lambda i: (i, 0)
    tab = lambda i: (i % period_tiles, 0)
    out_shapes = (
        jax.ShapeDtypeStruct((n, CONV_DIM), F32),
        jax.ShapeDtypeStruct((n, GDN_WIDTH), F32),
        jax.ShapeDtypeStruct((n, SWA_WIDTH), BF16),
        jax.ShapeDtypeStruct((n, SWA_KV_WIDTH), F32),
        jax.ShapeDtypeStruct((n, SWA_KV_WIDTH), F32),
        jax.ShapeDtypeStruct((n, LANES), F32),
    )
    return pl.pallas_call(
        _in_proj_kernel,
        grid=(n // IN_TILE,),
        in_specs=[
            pl.BlockSpec((IN_TILE, D_MODEL), row),
            pl.BlockSpec((D_MODEL, C_END), lambda i: (0, 0)),
            pl.BlockSpec((IN_TILE, LANES), tab),
            pl.BlockSpec((IN_TILE, LANES), tab),
            pl.BlockSpec((IN_TILE, LANES), tab),
        ],
        out_specs=(
            pl.BlockSpec((IN_TILE, CONV_DIM), row),
            pl.BlockSpec((IN_TILE, GDN_WIDTH), row),
            pl.BlockSpec((IN_TILE, SWA_WIDTH), row),
            pl.BlockSpec((IN_TILE, SWA_KV_WIDTH), row),
            pl.BlockSpec((IN_TILE, SWA_KV_WIDTH), row),
            pl.BlockSpec((IN_TILE, LANES), row),
        ),
        out_shape=out_shapes,
        compiler_params=pltpu.CompilerParams(
            dimension_semantics=("arbitrary",), vmem_limit_bytes=VMEM_LIMIT),
        name="in_proj",
    )(x2d, w_all, cos, s1, s2)


def _rope_tables(pos):
    p = pos.shape[0]
    inv = ROPE_THETA ** (-jnp.arange(0, ROT_DIM, 2, dtype=F32) / ROT_DIM)
    ang = pos[:, None] * inv[None, :]
    cos = jnp.cos(ang)
    sin = jnp.sin(ang)
    rest = SWA_HEAD_DIM - ROT_DIM
    head_c = jnp.concatenate([cos, cos, jnp.ones((p, rest), F32)], 1)
    head_s1 = jnp.concatenate([-sin, jnp.zeros((p, SWA_HEAD_DIM - ROT_HALF), F32)], 1)
    head_s2 = jnp.concatenate([jnp.zeros((p, ROT_HALF), F32), sin, jnp.zeros((p, rest), F32)], 1)
    two = lambda t: jnp.concatenate([t, t], 1)
    return two(head_c), two(head_s1), two(head_s2)


def _gdn_kernel(L, qkv_ref, z_ref, ba_ref, hist_ref, s0_ref, convw_ref, alog_ref, dtb_ref,
                normw_ref, o_ref, sout_ref, xbuf, state):
    c = pl.program_id(1)
    nbat = qkv_ref.shape[0]
    chains = [(b, h) for b in range(nbat) for h in range(GDN_HEADS)]

    @pl.when(c == 0)
    def _():
        xbuf[:, 0:HIST_ROWS, :] = hist_ref[...]
        state[...] = s0_ref[...]

    xbuf[:, HIST_ROWS:HIST_ROWS + L, :] = qkv_ref[...]

    row = lax.broadcasted_iota(jnp.int32, (L, L), 0)
    col = lax.broadcasted_iota(jnp.int32, (L, L), 1)
    incl = row >= col
    strict = row > col
    lower_ones = incl.astype(BF16)
    eye = (row == col).astype(F32)

    beta_all, g_all = [], []
    for b in range(nbat):
        ba = ba_ref[b]
        beta_all.append(jax.nn.sigmoid(ba))
        sp_in = ba + dtb_ref[...]
        softplus = jnp.maximum(sp_in, 0.0) + jnp.log1p(jnp.exp(-jnp.abs(sp_in)))
        g_all.append(-jnp.exp(alog_ref[...]) * softplus)

    def conv_silu(b, c0):
        acc = None
        for j in range(CONV_WIDTH):
            lo = HIST_ROWS - (CONV_WIDTH - 1) + j
            t = xbuf[b, lo:lo + L, c0:c0 + GDN_D] * convw_ref[j:j + 1, c0:c0 + GDN_D]
            acc = t if acc is None else acc + t
        return _silu(acc)

    def l2n(t):
        return t * lax.rsqrt(jnp.sum(t * t, -1, keepdims=True) + NORM_EPS)

    q = [l2n(conv_silu(b, h * GDN_D)) * (GDN_D ** -0.5) for b, h in chains]
    k = [l2n(conv_silu(b, GDN_WIDTH + h * GDN_D)) for b, h in chains]
    v = [conv_silu(b, 2 * GDN_WIDTH + h * GDN_D) for b, h in chains]
    beta = [beta_all[b][:, h:h + 1] for b, h in chains]
    g = [g_all[b][:, GDN_HEADS + h:GDN_HEADS + h + 1] for b, h in chains]

    def pieces(t):
        p1 = t.astype(BF16).astype(F32)
        r1 = t - p1
        p2 = r1.astype(BF16).astype(F32)
        return p1, p2, r1 - p2

    g_parts = [pieces(t) for t in g]
    diff = [sum(_dot(lower_ones, jnp.where(strict, p, 0.0).astype(BF16)) for p in gp)
            for gp in g_parts]
    gcum = [sum(_dot(lower_ones, jnp.broadcast_to(p, (L, GDN_D)).astype(BF16)) for p in gp)
            for gp in g_parts]
    dec = [jnp.where(incl, jnp.exp(jnp.where(incl, d, 0.0)), 0.0) for d in diff]
    exp_g = [jnp.exp(t) for t in gcum]
    g_end = [t[L - 1:L, :] for t in gcum]
    kb = [t.astype(BF16) for t in k]
    kk = [_dot_nt(t, t) for t in kb]
    aqk = [_dot_nt(qq.astype(BF16), t) * d for qq, t, d in zip(q, kb, dec)]
    a = [jnp.where(strict, d * t, 0.0) * bt for d, t, bt in zip(dec, kk, beta)]

    inv = [eye - t for t in a]
    pw = [_split(-t) for t in a]
    for _ in range(int(math.log2(L)) - 1):
        pw = [_split(_dot_split(p, p)) for p in pw]
        inv = [t + _dot_split(_split(t), p) for t, p in zip(inv, pw)]
    inv_s = [_split(t) for t in inv]
    w = [_dot_split(t, _split(bt * vv)) for t, bt, vv in zip(inv_s, beta, v)]
    y = [_dot_split(t, _split((bt * eg) * kk_)) for t, bt, eg, kk_ in zip(inv_s, beta, exp_g, k)]

    s = [state[b, h] for b, h in chains]
    sb = [t.astype(BF16) for t in s]
    u = [ww - _dot(yy.astype(BF16), t) for ww, yy, t in zip(w, y, sb)]
    ub = [t.astype(BF16) for t in u]
    o = [_dot((eg * qq).astype(BF16), t) + _dot(aa.astype(BF16), uu)
         for eg, qq, t, aa, uu in zip(exp_g, q, sb, aqk, ub)]
    k_dec = [(jnp.exp(ge - gc) * kk_).astype(BF16) for ge, gc, kk_ in zip(g_end, gcum, k)]
    s_new = [jnp.exp(ge) * t + _dot_tn(kd, uu) for ge, t, kd, uu in zip(g_end, s, k_dec, ub)]

    for (b, h), t, oo in zip(chains, s_new, o):
        state[b, h] = t
        on = oo * lax.rsqrt(jnp.mean(oo * oo, -1, keepdims=True) + NORM_EPS)
        zz = z_ref[b, :, h * GDN_D:(h + 1) * GDN_D]
        o_ref[b, :, h * GDN_D:(h + 1) * GDN_D] = (on * normw_ref[...] * _silu(zz)).astype(BF16)

    xbuf[:, 0:HIST_ROWS, :] = xbuf[:, L:L + HIST_ROWS, :]

    @pl.when(c == pl.num_programs(1) - 1)
    def _():
        sout_ref[...] = state[...]


def _gdn(L, qkv, z, ba, hist8, s0, conv_w, alog_pad, dtb_pad, norm_w):
    b, t, _ = qkv.shape
    nc = t // L
    nbat = GDN_BATCH
    tok = lambda i, c: (i, c, 0)
    per_b3 = lambda i, c: (i, 0, 0)
    per_b4 = lambda i, c: (i, 0, 0, 0)
    const2 = lambda i, c: (0, 0)
    return pl.pallas_call(
        functools.partial(_gdn_kernel, L),
        grid=(b // nbat, nc),
        in_specs=[
            pl.BlockSpec((nbat, L, CONV_DIM), tok),
            pl.BlockSpec((nbat, L, GDN_WIDTH), tok),
            pl.BlockSpec((nbat, L, LANES), tok),
            pl.BlockSpec((nbat, HIST_ROWS, CONV_DIM), per_b3),
            pl.BlockSpec((nbat, GDN_HEADS, GDN_D, GDN_D), per_b4),
            pl.BlockSpec((CONV_WIDTH, CONV_DIM), const2),
            pl.BlockSpec((1, LANES), const2),
            pl.BlockSpec((1, LANES), const2),
            pl.BlockSpec((1, GDN_D), const2),
        ],
        out_specs=(
            pl.BlockSpec((nbat, L, GDN_WIDTH), tok),
            pl.BlockSpec((nbat, GDN_HEADS, GDN_D, GDN_D), per_b4),
        ),
        out_shape=(
            jax.ShapeDtypeStruct((b, t, GDN_WIDTH), BF16),
            jax.ShapeDtypeStruct((b, GDN_HEADS, GDN_D, GDN_D), F32),
        ),
        scratch_shapes=[
            pltpu.VMEM((nbat, HIST_ROWS + L, CONV_DIM), F32),
            pltpu.VMEM((nbat, GDN_HEADS, GDN_D, GDN_D), F32),
        ],
        compiler_params=pltpu.CompilerParams(
            dimension_semantics=("arbitrary", "arbitrary"), vmem_limit_bytes=VMEM_LIMIT),
        name="gdn",
    )(qkv, z, ba, hist8, s0, conv_w, alog_pad, dtb_pad, norm_w)


def _swa_kernel(L, pos0, q_ref, k_ref, v_ref, sink_ref, o_ref):
    c = pl.program_id(1)
    nch = q_ref.shape[1] // L
    span = WINDOW + L
    rows = WINDOW + nch * L
    start = pl.multiple_of(c * (nch * L), L)
    kx = k_ref[0, pl.ds(start, rows), :]
    vx = v_ref[0, pl.ds(start, rows), :]
    kx_sw = pltpu.roll(kx, SWA_HEAD_DIM, 1)
    vx_sw = pltpu.roll(vx, SWA_HEAD_DIM, 1)
    low = lax.broadcasted_iota(jnp.int32, (rows, LANES), 1) < SWA_HEAD_DIM

    def halves(x, x_sw, kh):
        src_lo, src_hi = (x, x_sw) if kh == 0 else (x_sw, x)
        return (jnp.where(low, src_lo, 0.0).astype(BF16), jnp.where(low, 0.0, src_hi).astype(BF16))

    k_halves = [halves(kx, kx_sw, kh) for kh in range(SWA_KV_HEADS)]
    v_halves = [halves(vx, vx_sw, kh) for kh in range(SWA_KV_HEADS)]
    sinks = sink_ref[...]
    top_rows = lax.broadcasted_iota(jnp.int32, (2 * L, 1), 0) < L
    first_pos = pos0 - WINDOW + c * (nch * L)

    chains = [(u, kh, half) for u in range(nch) for kh in range(SWA_KV_HEADS) for half in range(2)]
    valid = [first_pos + u * L + lax.broadcasted_iota(jnp.int32, (1, span), 1) >= 0
             for u in range(nch)]
    q4 = {}
    for u in range(nch):
        for kh in range(SWA_KV_HEADS):
            c0 = kh * 2 * LANES
            q4[u, kh] = jnp.concatenate([q_ref[0, u * L:(u + 1) * L, c0:c0 + LANES],
                                         q_ref[0, u * L:(u + 1) * L, c0 + LANES:c0 + 2 * LANES]], 0)
    scores = [jnp.where(valid[u],
                        _dot_nt(q4[u, kh], k_halves[kh][half][u * L:u * L + span])
                        * (SWA_HEAD_DIM ** -0.5), -jnp.inf)
              for u, kh, half in chains]
    sink_cols = [jnp.where(top_rows, sinks[:, kh * SWA_GROUP + half:kh * SWA_GROUP + half + 1],
                           sinks[:, kh * SWA_GROUP + half + 2:kh * SWA_GROUP + half + 3])
                 for _, kh, half in chains]
    m = [jnp.maximum(jnp.max(s, -1, keepdims=True), sk) for s, sk in zip(scores, sink_cols)]
    p = [jnp.exp(s - mm) for s, mm in zip(scores, m)]
    den = [jnp.sum(pp, -1, keepdims=True) + jnp.exp(sk - mm) for pp, sk, mm in zip(p, sink_cols, m)]
    pv = [_dot(pp.astype(BF16), v_halves[kh][half][u * L:u * L + span]) / dd
          for pp, dd, (u, kh, half) in zip(p, den, chains)]
    for n in range(0, len(chains), 2):
        u, kh, _ = chains[n]
        o = pv[n] + pv[n + 1]
        c0 = kh * 2 * LANES
        o_ref[0, u * L:(u + 1) * L, c0:c0 + LANES] = o[0:L].astype(BF16)
        o_ref[0, u * L:(u + 1) * L, c0 + LANES:c0 + 2 * LANES] = o[L:2 * L].astype(BF16)


def _swa(L, pos0, q, k_ext, v_ext, sinks_pad):
    b, t, _ = q.shape
    ext = k_ext.shape[1]
    nch = SWA_CHUNKS if (t // L) % SWA_CHUNKS == 0 else 1
    return pl.pallas_call(
        functools.partial(_swa_kernel, L, pos0),
        grid=(b, t // (nch * L)),
        in_specs=[
            pl.BlockSpec((1, nch * L, SWA_WIDTH), lambda i, c: (i, c, 0)),
            pl.BlockSpec((1, ext, SWA_KV_WIDTH), lambda i, c: (i, 0, 0)),
            pl.BlockSpec((1, ext, SWA_KV_WIDTH), lambda i, c: (i, 0, 0)),
            pl.BlockSpec((1, LANES), lambda i, c: (0, 0)),
        ],
        out_specs=pl.BlockSpec((1, nch * L, SWA_WIDTH), lambda i, c: (i, c, 0)),
        out_shape=jax.ShapeDtypeStruct((b, t, SWA_WIDTH), BF16),
        compiler_params=pltpu.CompilerParams(
            dimension_semantics=("arbitrary", "arbitrary"), vmem_limit_bytes=VMEM_LIMIT),
        name="swa",
    )(q, k_ext, v_ext, sinks_pad)


def _layer_norm(r, g, b):
    mu = jnp.mean(r, -1, keepdims=True)
    d = r - mu
    var = jnp.mean(d * d, -1, keepdims=True)
    return d * lax.rsqrt(var + LN_EPS) * g + b


def _route(x1b, wrt_ref, rbias_ref, count_ref):
    tm = x1b.shape[0]
    scores = jax.nn.sigmoid(_dot_nt(wrt_ref[...], x1b))
    biased = scores + rbias_ref[...]
    neg = -jnp.inf
    r_g = lax.broadcasted_iota(jnp.int32, (GROUP_SIZE, tm), 0)
    blocks = []
    group_score = []
    for g in range(N_GROUPS):
        blk = biased[g * GROUP_SIZE:(g + 1) * GROUP_SIZE]
        m1 = jnp.max(blk, 0, keepdims=True)
        i1 = jnp.min(jnp.where(blk == m1, r_g, GROUP_SIZE), 0, keepdims=True)
        m2 = jnp.max(jnp.where(r_g == i1, neg, blk), 0, keepdims=True)
        blocks.append(blk)
        group_score.append(m1 + m2)
    masked = []
    for g in range(N_GROUPS):
        rank = jnp.zeros((1, tm), jnp.int32)
        for o in range(N_GROUPS):
            if o == g:
                continue
            ahead = group_score[o] > group_score[g]
            if o < g:
                ahead = ahead | (group_score[o] == group_score[g])
            rank = rank + ahead.astype(jnp.int32)
        masked.append(jnp.where(rank < TOPK_GROUPS, blocks[g], neg))
    masked = jnp.concatenate(masked, 0)
    r_e = lax.broadcasted_iota(jnp.int32, (N_EXPERTS, tm), 0)
    idx_rows = []
    w_rows = []
    hits = []
    for _ in range(TOP_K):
        m = jnp.max(masked, 0, keepdims=True)
        idx = jnp.min(jnp.where(masked == m, r_e, N_EXPERTS), 0, keepdims=True)
        hit = r_e == idx
        w_rows.append(jnp.sum(jnp.where(hit, scores, 0.0), 0, keepdims=True))
        masked = jnp.where(hit, neg, masked)
        idx_rows.append(idx)
        hits.append(hit)
    total = w_rows[0]
    for wr in w_rows[1:]:
        total = total + wr
    top_w = jnp.concatenate(w_rows, 0) / total * ROUTED_SCALE

    chosen = jnp.zeros((N_EXPERTS, tm), F32)
    for hit in hits:
        chosen = chosen + hit.astype(F32)
    t_r = lax.broadcasted_iota(jnp.int32, (tm, tm), 0)
    t_c = lax.broadcasted_iota(jnp.int32, (tm, tm), 1)
    before = (t_r < t_c).astype(BF16)
    prior = _dot(chosen.astype(BF16), before) + count_ref[...]
    rank_rows = [jnp.sum(jnp.where(hit, prior, 0.0), 0, keepdims=True) for hit in hits]
    count_ref[...] = count_ref[...] + jnp.sum(chosen, 1, keepdims=True)
    rank = jnp.concatenate(rank_rows, 0).astype(jnp.int32)
    return jnp.concatenate(idx_rows, 0), top_w, rank


def _mix_kernel(np_tiles, gp_ref, sp_ref, xp_ref, gs_ref, ss_ref, xs_ref, wo_ref, g1_ref, b1_ref,
                wrt_ref, rbias_ref, x1_ref, tope_ref, topw_ref, rank_ref, cnt_ref, count_acc):
    i = pl.program_id(0)

    @pl.when(i == 0)
    def _():
        count_acc[...] = jnp.zeros_like(count_acc)

    is_prompt = i < np_tiles
    gd = jnp.where(is_prompt, gp_ref[...], gs_ref[...])
    sw = jnp.where(is_prompt, sp_ref[...], ss_ref[...])
    x = jnp.where(is_prompt, xp_ref[...], xs_ref[...])
    mix = _dot(gd, wo_ref[0:GDN_WIDTH, :]) + _dot(sw, wo_ref[GDN_WIDTH:, :])
    x1 = _layer_norm(ALPHA * x + mix, g1_ref[...], b1_ref[...])
    _store_token_tiles(x1_ref, x1)
    top_e, top_w, rank = _route(x1.astype(BF16), wrt_ref, rbias_ref, count_acc)
    tope_ref[...] = top_e
    topw_ref[...] = top_w
    rank_ref[...] = rank
    cnt_ref[...] = count_acc[...].astype(jnp.int32)


def _mix(gd_p, sw_p, x_p, gd_s, sw_s, x_s, wo, g1, b1, wrt, rbias):
    n_p, n_s = x_p.shape[0], x_s.shape[0]
    np_tiles = n_p // MIX_TILE
    nt = n_p + n_s
    pm = lambda i: (jnp.minimum(i, np_tiles - 1), 0)
    sm = lambda i: (jnp.maximum(i - np_tiles, 0), 0)
    const = lambda i: (0, 0)
    col = lambda i: (0, i)
    return pl.pallas_call(
        functools.partial(_mix_kernel, np_tiles),
        grid=(nt // MIX_TILE,),
        in_specs=[
            pl.BlockSpec((MIX_TILE, GDN_WIDTH), pm),
            pl.BlockSpec((MIX_TILE, SWA_WIDTH), pm),
            pl.BlockSpec((MIX_TILE, D_MODEL), pm),
            pl.BlockSpec((MIX_TILE, GDN_WIDTH), sm),
            pl.BlockSpec((MIX_TILE, SWA_WIDTH), sm),
            pl.BlockSpec((MIX_TILE, D_MODEL), sm),
            pl.BlockSpec((D_MODEL, D_MODEL), const),
            pl.BlockSpec((1, D_MODEL), const),
            pl.BlockSpec((1, D_MODEL), const),
            pl.BlockSpec((N_EXPERTS, D_MODEL), const),
            pl.BlockSpec((N_EXPERTS, 1), const),
        ],
        out_specs=(
            pl.BlockSpec((MIX_TILE * ROW_TILES, LANES), lambda i: (i, 0)),
            pl.BlockSpec((TOP_K, MIX_TILE), col),
            pl.BlockSpec((TOP_K, MIX_TILE), col),
            pl.BlockSpec((TOP_K, MIX_TILE), col),
            pl.BlockSpec((N_EXPERTS, 1), const),
        ),
        out_shape=(
            jax.ShapeDtypeStruct((nt * ROW_TILES, LANES), F32),
            jax.ShapeDtypeStruct((TOP_K, nt), jnp.int32),
            jax.ShapeDtypeStruct((TOP_K, nt), F32),
            jax.ShapeDtypeStruct((TOP_K, nt), jnp.int32),
            jax.ShapeDtypeStruct((N_EXPERTS, 1), jnp.int32),
        ),
        scratch_shapes=[pltpu.VMEM((N_EXPERTS, 1), F32)],
        compiler_params=pltpu.CompilerParams(
            dimension_semantics=("arbitrary",), vmem_limit_bytes=VMEM_LIMIT),
        name="mix",
    )(gd_p, sw_p, x_p, gd_s, sw_s, x_s, wo, g1, b1, wrt, rbias)


def _dest_kernel(tope_ref, rank_ref, pstart_ref, dest_ref):
    tm = tope_ref.shape[1]
    r_e = lax.broadcasted_iota(jnp.int32, (N_EXPERTS, tm), 0)
    pstart = pstart_ref[...]
    top_e = tope_ref[...]
    rows = [jnp.sum(jnp.where(r_e == top_e[j:j + 1, :], pstart, 0), 0, keepdims=True)
            for j in range(TOP_K)]
    dest_ref[...] = (jnp.concatenate(rows, 0) + rank_ref[...]) * ROW_TILES


def _dest(top_e_t, rank_t, pstart):
    nt = top_e_t.shape[1]
    col = lambda i: (0, i)
    return pl.pallas_call(
        _dest_kernel,
        grid=(nt // DEST_TILE,),
        in_specs=[
            pl.BlockSpec((TOP_K, DEST_TILE), col),
            pl.BlockSpec((TOP_K, DEST_TILE), col),
            pl.BlockSpec((N_EXPERTS, 1), lambda i: (0, 0)),
        ],
        out_specs=pl.BlockSpec((TOP_K, DEST_TILE), col),
        out_shape=jax.ShapeDtypeStruct((TOP_K, nt), jnp.int32),
        compiler_params=pltpu.CompilerParams(dimension_semantics=("arbitrary",)),
        name="dest",
    )(top_e_t, rank_t, pstart)


def _experts_kernel(be_ref, nused_ref, c0_ref, wpar_ref, wnext_ref, stok_ref, x1_ref,
                    wg_ref, wu_ref, wd_ref, y_ref,
                    idx0, idx1, xbuf0, xbuf1, idx_sem, row_sem, wg_f, wu_f, wd_f, w_sem,
                    wg_b, wu_b, wd_b):
    i = pl.program_id(0)
    n_used = nused_ref[0]
    idx = (idx0, idx1)
    xbuf = (xbuf0, xbuf1)

    def weight_copies(e, s):
        return (pltpu.make_async_copy(wg_ref.at[e], wg_f.at[s], w_sem.at[s]),
                pltpu.make_async_copy(wu_ref.at[e], wu_f.at[s], w_sem.at[s]),
                pltpu.make_async_copy(wd_ref.at[e], wd_f.at[s], w_sem.at[s]))

    def idx_copy(block, s):
        base = pl.multiple_of(c0_ref[block] // LANES * LANES, LANES)
        return pltpu.make_async_copy(stok_ref.at[pl.ds(base, IDX_WINDOW)], idx[s], idx_sem.at[s])

    def start_rows(block, s):
        off = lax.rem(c0_ref[block], LANES)
        for r in range(EXPERT_BLOCK):
            src = pl.multiple_of(idx[s][off + r], ROW_TILES)
            pltpu.make_async_copy(x1_ref.at[pl.ds(src, ROW_TILES), :],
                                  xbuf[s].at[pl.ds(r * ROW_TILES, ROW_TILES), :],
                                  row_sem.at[s]).start(priority=r % 2)

    def wait_rows(s):
        pltpu.make_async_copy(x1_ref.at[pl.ds(0, EXPERT_BLOCK * ROW_TILES), :], xbuf[s],
                              row_sem.at[s]).wait()

    @pl.when(i == 0)
    def _():
        for cp in weight_copies(be_ref[0], 0):
            cp.start()
        idx_copy(0, 0).start()
        idx_copy(0, 0).wait()
        start_rows(0, 0)
        idx_copy(1, 1).start()

    prev = be_ref[jnp.maximum(i - 1, 0)]

    @pl.when((i < n_used) & ((i == 0) | (be_ref[i] != prev)))
    def _():
        par = wpar_ref[i]
        for cp in weight_copies(be_ref[i], par):
            cp.wait()

        @pl.when(wnext_ref[i] >= 0)
        def _():
            for cp in weight_copies(wnext_ref[i], 1 - par):
                cp.start()

        wg_b[...] = wg_f[par].astype(BF16)
        wu_b[...] = wu_f[par].astype(BF16)
        wd_b[...] = wd_f[par].astype(BF16)

    for slot in range(2):
        other = 1 - slot

        @pl.when((i < n_used) & (lax.rem(i, 2) == slot))
        def _():
            idx_copy(i + 1, other).wait()
            wait_rows(slot)
            start_rows(i + 1, other)
            idx_copy(i + 2, slot).start()
            xb = _load_token_tiles(xbuf[slot], EXPERT_BLOCK).astype(BF16)
            hb = _silu(_dot(xb, wg_b[...])) * _dot(xb, wu_b[...])
            _store_token_tiles(y_ref, _dot(hb.astype(BF16), wd_b[...]))

        @pl.when((i == n_used) & (lax.rem(i, 2) == slot))
        def _():
            wait_rows(slot)
            idx_copy(i + 1, other).wait()

    @pl.when(i >= n_used)
    def _():
        y_ref[...] = jnp.zeros_like(y_ref)


def _experts(block_e, n_used, block_c0, w_par, w_next, stok, x1, w_gate_e, w_up_e, w_down_e):
    nb = block_e.shape[0]
    grid_spec = pltpu.PrefetchScalarGridSpec(
        num_scalar_prefetch=5,
        grid=(nb,),
        in_specs=[pl.BlockSpec(memory_space=pl.ANY)] * 5,
        out_specs=pl.BlockSpec((EXPERT_BLOCK * ROW_TILES, LANES), lambda i, *_: (i, 0)),
        scratch_shapes=[
            pltpu.SMEM((IDX_WINDOW,), jnp.int32),
            pltpu.SMEM((IDX_WINDOW,), jnp.int32),
            pltpu.VMEM((EXPERT_BLOCK * ROW_TILES, LANES), F32),
            pltpu.VMEM((EXPERT_BLOCK * ROW_TILES, LANES), F32),
            pltpu.SemaphoreType.DMA((2,)),
            pltpu.SemaphoreType.DMA((2,)),
            pltpu.VMEM((2, D_MODEL, EXPERT_FF), F32),
            pltpu.VMEM((2, D_MODEL, EXPERT_FF), F32),
            pltpu.VMEM((2, EXPERT_FF, D_MODEL), F32),
            pltpu.SemaphoreType.DMA((2,)),
            pltpu.VMEM((D_MODEL, EXPERT_FF), BF16),
            pltpu.VMEM((D_MODEL, EXPERT_FF), BF16),
            pltpu.VMEM((EXPERT_FF, D_MODEL), BF16),
        ],
    )
    return pl.pallas_call(
        _experts_kernel,
        grid_spec=grid_spec,
        out_shape=jax.ShapeDtypeStruct((nb * EXPERT_BLOCK * ROW_TILES, LANES), F32),
        compiler_params=pltpu.CompilerParams(
            dimension_semantics=("arbitrary",), vmem_limit_bytes=VMEM_LIMIT),
        name="experts",
    )(block_e, n_used, block_c0, w_par, w_next, stok, x1, w_gate_e, w_up_e, w_down_e)


def _final_kernel(np_tiles, dest_ref, dest_next_ref, x1_ref, topw_ref, yb_ref, wsg_ref, wsu_ref,
                  wsd_ref, g2_ref, b2_ref, yp_ref, ys_ref, ybuf, sem):
    i = pl.program_id(0)
    slot = lax.rem(i, 2)

    def issue_rows(d_ref, s):
        def body(t, carry):
            dst = pl.multiple_of(t * ROW_TILES, ROW_TILES)
            for j in range(TOP_K):
                src = pl.multiple_of(d_ref[j, t], ROW_TILES)
                pltpu.make_async_copy(yb_ref.at[pl.ds(src, ROW_TILES), :],
                                      ybuf.at[s, j, pl.ds(dst, ROW_TILES), :],
                                      sem.at[s]).start(priority=j % 2)
            return carry
        lax.fori_loop(0, MIX_TILE, body, 0)

    @pl.when(i == 0)
    def _():
        issue_rows(dest_ref, 0)

    @pl.when(i + 1 < pl.num_programs(0))
    def _():
        issue_rows(dest_next_ref, 1 - slot)

    x1 = _load_token_tiles(x1_ref, MIX_TILE)
    xb = x1.astype(BF16)
    hs = _silu(_dot(xb, wsg_ref[...])) * _dot(xb, wsu_ref[...])
    ff = _dot(hs.astype(BF16), wsd_ref[...])
    r = lax.broadcasted_iota(jnp.int32, (MIX_TILE, MIX_TILE), 0)
    c = lax.broadcasted_iota(jnp.int32, (MIX_TILE, MIX_TILE), 1)
    w_cols = _dot_nt((r == c).astype(F32), topw_ref[...], HIGHEST)
    for j in range(TOP_K):
        pltpu.make_async_copy(yb_ref.at[pl.ds(0, MIX_TILE * ROW_TILES), :], ybuf.at[slot, j],
                              sem.at[slot]).wait()
    for j in range(TOP_K):
        ff = ff + w_cols[:, j:j + 1] * _load_token_tiles(ybuf.at[slot, j], MIX_TILE)
    y = _layer_norm(ALPHA * x1 + ff, g2_ref[...], b2_ref[...])

    @pl.when(i < np_tiles)
    def _():
        yp_ref[...] = y

    @pl.when(i >= np_tiles)
    def _():
        ys_ref[...] = y


def _final(n_p, dest, x1, top_w_t, yb, wsg, wsu, wsd, g2, b2):
    nt = x1.shape[0] // ROW_TILES
    np_tiles = n_p // MIX_TILE
    row = lambda i: (i, 0)
    const = lambda i: (0, 0)
    n_tiles = nt // MIX_TILE
    return pl.pallas_call(
        functools.partial(_final_kernel, np_tiles),
        grid=(n_tiles,),
        in_specs=[
            pl.BlockSpec((TOP_K, MIX_TILE), lambda i: (0, i), memory_space=pltpu.SMEM),
            pl.BlockSpec((TOP_K, MIX_TILE), lambda i: (0, jnp.minimum(i + 1, n_tiles - 1)),
                         memory_space=pltpu.SMEM),
            pl.BlockSpec((MIX_TILE * ROW_TILES, LANES), row),
            pl.BlockSpec((TOP_K, MIX_TILE), lambda i: (0, i)),
            pl.BlockSpec(memory_space=pl.ANY),
            pl.BlockSpec((D_MODEL, EXPERT_FF), const),
            pl.BlockSpec((D_MODEL, EXPERT_FF), const),
            pl.BlockSpec((EXPERT_FF, D_MODEL), const),
            pl.BlockSpec((1, D_MODEL), const),
            pl.BlockSpec((1, D_MODEL), const),
        ],
        out_specs=(
            pl.BlockSpec((MIX_TILE, D_MODEL), lambda i: (jnp.minimum(i, np_tiles - 1), 0)),
            pl.BlockSpec((MIX_TILE, D_MODEL), lambda i: (jnp.maximum(i - np_tiles, 0), 0)),
        ),
        out_shape=(
            jax.ShapeDtypeStruct((n_p, D_MODEL), F32),
            jax.ShapeDtypeStruct((nt - n_p, D_MODEL), F32),
        ),
        scratch_shapes=[
            pltpu.VMEM((2, TOP_K, MIX_TILE * ROW_TILES, LANES), F32),
            pltpu.SemaphoreType.DMA((2,)),
        ],
        compiler_params=pltpu.CompilerParams(
            dimension_semantics=("arbitrary",), vmem_limit_bytes=VMEM_LIMIT),
        name="final",
    )(dest, dest, x1, top_w_t, yb, wsg, wsu, wsd, g2, b2)


def _expert_layout(counts, nk):
    nb = -(-nk // EXPERT_BLOCK) + N_EXPERTS
    padded = (counts + EXPERT_BLOCK - 1) // EXPERT_BLOCK * EXPERT_BLOCK
    pend = jnp.cumsum(padded)
    pstart = (pend - padded).astype(jnp.int32)
    block_start = jnp.arange(nb, dtype=jnp.int32) * EXPERT_BLOCK
    block_e = jnp.sum((pend[None, :] <= block_start[:, None]).astype(jnp.int32), 1)
    block_e = jnp.minimum(block_e, N_EXPERTS - 1)
    n_used = (pend[-1] // EXPERT_BLOCK).astype(jnp.int32).reshape(1)
    start = (jnp.cumsum(counts) - counts).astype(jnp.int32)
    block_c0 = start[block_e] + block_start - pstart[block_e]
    block_c0 = jnp.where(block_start < pend[-1], block_c0, 0)
    block_c0 = jnp.concatenate([block_c0, jnp.zeros((2,), jnp.int32)]).astype(jnp.int32)
    ids = jnp.arange(nb, dtype=jnp.int32)
    used = block_start < pend[-1]
    switch = used & ((ids == 0) | (block_e != jnp.roll(block_e, 1)))
    w_par = ((jnp.cumsum(switch.astype(jnp.int32)) - 1) % 2).astype(jnp.int32)
    later_switch = (ids[None, :] > ids[:, None]) & switch[None, :]
    next_pos = jnp.min(jnp.where(later_switch, ids[None, :], nb), axis=1)
    w_next = jnp.where(next_pos < nb, block_e[jnp.minimum(next_pos, nb - 1)], -1).astype(jnp.int32)
    return pstart, block_e, n_used, block_c0, w_par, w_next


def _pad_lanes(v, offset=0):
    out = jnp.zeros((1, LANES), F32)
    return out.at[0, offset:offset + v.shape[0]].set(v.astype(F32))


def kernel(x_prompt, x_sample, state_conv, state_gdn, cache_swa_k, cache_swa_v, w_in, conv_w, a_log, dt_bias, gdn_norm_w, attn_sinks, w_o, ln1_g, ln1_b, w_router, router_bias, w_gate_e, w_up_e, w_down_e, w_shared_gate, w_shared_up, w_shared_down, ln2_g, ln2_b):
    bp, tp, d = x_prompt.shape
    bs, ts, _ = x_sample.shape
    n_p, n_s = bp * tp, bs * ts

    wi = w_in[0]
    o1 = CONV_DIM
    o2 = o1 + GDN_WIDTH
    o4 = o2 + 2 * GDN_HEADS
    o5 = o4 + SWA_WIDTH
    o6 = o5 + SWA_KV_WIDTH
    ba_cols = jnp.zeros((d, LANES), F32).at[:, :2 * GDN_HEADS].set(wi[:, o2:o4])
    w_all = jnp.concatenate([wi[:, :o2], wi[:, o4:o5], wi[:, o5:o6], wi[:, o6:], ba_cols], 1).astype(BF16)
    wo = w_o[0].astype(BF16)
    wrt = w_router[0].T.astype(BF16)
    rbias = router_bias[0].astype(F32).reshape(N_EXPERTS, 1)
    alog_pad = _pad_lanes(a_log[0], GDN_HEADS)
    dtb_pad = _pad_lanes(dt_bias[0], GDN_HEADS)
    sinks_pad = _pad_lanes(attn_sinks[0])
    norm_w = gdn_norm_w[0].reshape(1, GDN_D)
    g1, b1 = ln1_g[0].reshape(1, d), ln1_b[0].reshape(1, d)
    g2, b2 = ln2_g[0].reshape(1, d), ln2_b[0].reshape(1, d)

    def front(x, pos0, L, conv_hist, s0, k_hist, v_hist):
        b, t, _ = x.shape
        period = max(t, IN_TILE)
        pos = pos0 + (jnp.arange(period, dtype=jnp.int32) % t).astype(F32)
        cos, s1, s2 = _rope_tables(pos)
        x2d = x.reshape(b * t, d)
        qkv, z, sq, sk, sv, ba = _in_proj(x2d, w_all, cos, s1, s2)
        qkv = qkv.reshape(b, t, CONV_DIM)
        hist8 = jnp.concatenate(
            [jnp.zeros((b, HIST_ROWS - (CONV_WIDTH - 1), CONV_DIM), F32), conv_hist], 1)
        gd, s_new = _gdn(L, qkv, z.reshape(b, t, GDN_WIDTH), ba.reshape(b, t, LANES), hist8, s0,
                         conv_w[0], alog_pad, dtb_pad, norm_w)
        k_ext = jnp.concatenate([k_hist.reshape(b, WINDOW, SWA_KV_WIDTH),
                                 sk.reshape(b, t, SWA_KV_WIDTH)], 1)
        v_ext = jnp.concatenate([v_hist.reshape(b, WINDOW, SWA_KV_WIDTH),
                                 sv.reshape(b, t, SWA_KV_WIDTH)], 1)
        sw = _swa(L, pos0, sq.reshape(b, t, SWA_WIDTH), k_ext, v_ext, sinks_pad)
        conv_new = jnp.concatenate([conv_hist, qkv], 1)[:, -(CONV_WIDTH - 1):]
        k_new = k_ext[:, -WINDOW:].reshape(b, WINDOW, SWA_KV_HEADS, SWA_HEAD_DIM)
        v_new = v_ext[:, -WINDOW:].reshape(b, WINDOW, SWA_KV_HEADS, SWA_HEAD_DIM)
        return (x2d, gd.reshape(b * t, GDN_WIDTH), sw.reshape(b * t, SWA_WIDTH),
                conv_new, s_new, k_new, v_new)

    zeros = lambda *s: jnp.zeros(s, F32)
    xp2, gd_p, sw_p, conv_p, gdn_p, k_p, v_p = front(
        x_prompt, 0, CHUNK, zeros(bp, CONV_WIDTH - 1, CONV_DIM),
        zeros(bp, GDN_HEADS, GDN_D, GDN_D), zeros(bp, WINDOW, SWA_KV_WIDTH),
        zeros(bp, WINDOW, SWA_KV_WIDTH))
    xs2, gd_s, sw_s, conv_s, gdn_s, k_s, v_s = front(
        x_sample, PAST_LEN, ts, state_conv[0], state_gdn[0], cache_swa_k[0], cache_swa_v[0])

    x1, top_e_t, top_w_t, rank_t, counts = _mix(
        gd_p, sw_p, xp2, gd_s, sw_s, xs2, wo, g1, b1, wrt, rbias)

    pstart, block_e, n_used, block_c0, w_par, w_next = _expert_layout(
        counts[:, 0], TOP_K * (n_p + n_s))
    dest = _dest(top_e_t, rank_t, pstart.reshape(N_EXPERTS, 1))
    order = jnp.argsort(top_e_t.T.reshape(-1))
    stok = jnp.concatenate([(order // TOP_K * ROW_TILES).astype(jnp.int32),
                            jnp.zeros((IDX_WINDOW,), jnp.int32)])
    yb = _experts(block_e, n_used, block_c0, w_par, w_next, stok, x1,
                  w_gate_e[0], w_up_e[0], w_down_e[0])

    y_p, y_s = _final(n_p, dest, x1, top_w_t, yb,
                      w_shared_gate[0].astype(BF16), w_shared_up[0].astype(BF16),
                      w_shared_down[0].astype(BF16), g2, b2)
    return (y_p.reshape(bp, tp, d), y_s.reshape(bs, ts, d),
            conv_p[None], gdn_p[None], k_p[None], v_p[None],
            conv_s[None], gdn_s[None], k_s[None], v_s[None])
```

```python
import functools
import math

import jax
import jax.numpy as jnp
from jax import lax
from jax.experimental import pallas as pl
from jax.experimental.pallas import tpu as pltpu

F32 = jnp.float32
BF16 = jnp.bfloat16
HIGHEST = lax.Precision.HIGHEST

D_MODEL = 1024
CHUNK = 64
GDN_HEADS = 4
GDN_D = 128
GDN_WIDTH = GDN_HEADS * GDN_D
CONV_DIM = 3 * GDN_WIDTH
CONV_WIDTH = 4
SWA_HEAD_DIM = 64
SWA_Q_HEADS = 8
SWA_KV_HEADS = 2
SWA_GROUP = SWA_Q_HEADS // SWA_KV_HEADS
SWA_WIDTH = SWA_Q_HEADS * SWA_HEAD_DIM
SWA_KV_WIDTH = SWA_KV_HEADS * SWA_HEAD_DIM
WINDOW = 128
ROPE_THETA = 500000.0
ROT_DIM = SWA_HEAD_DIM // 4
ROT_HALF = ROT_DIM // 2
N_EXPERTS = 256
TOP_K = 8
N_GROUPS = 8
GROUP_SIZE = N_EXPERTS // N_GROUPS
TOPK_GROUPS = 4
EXPERT_FF = 256
ROUTED_SCALE = 2.5
ALPHA = 2.0 ** 0.25
LN_EPS = 1e-5
NORM_EPS = 1e-6
PAST_LEN = 4096

LANES = 128
HIST_ROWS = 8
ROW_TILES = D_MODEL // LANES

C_QKV = 0
C_Z = C_QKV + CONV_DIM
C_SQ = C_Z + GDN_WIDTH
C_SK = C_SQ + SWA_WIDTH
C_SV = C_SK + SWA_KV_WIDTH
C_BA = C_SV + SWA_KV_WIDTH
C_END = C_BA + LANES

IN_TILE = 512
MIX_TILE = 256
DEST_TILE = 512
EXPERT_BLOCK = 256
IDX_WINDOW = EXPERT_BLOCK + LANES
GDN_BATCH = 4
SWA_CHUNKS = 4
VMEM_LIMIT = 56 * 1024 * 1024


def _dot(a, b, precision=None):
    return jnp.dot(a, b, preferred_element_type=F32, precision=precision)


def _dot_nt(a, b, precision=None):
    return lax.dot_general(a, b, (((1,), (1,)), ((), ())),
                           preferred_element_type=F32, precision=precision)


def _dot_tn(a, b):
    return lax.dot_general(a, b, (((0,), (0,)), ((), ())), preferred_element_type=F32)


def _split(x):
    hi = x.astype(BF16)
    return hi, (x - hi.astype(F32)).astype(BF16)


def _dot_split(a, b):
    return _dot(a[0], b[0]) + (_dot(a[1], b[0]) + _dot(a[0], b[1]))


def _silu(x):
    return x * jax.nn.sigmoid(x)


def _store_token_tiles(ref, x):
    m = x.shape[0]
    for s in range(ROW_TILES):
        ref[pl.ds(s, m, stride=ROW_TILES), :] = x[:, s * LANES:(s + 1) * LANES]


def _load_token_tiles(ref, m):
    return jnp.concatenate([ref[pl.ds(s, m, stride=ROW_TILES), :] for s in range(ROW_TILES)], 1)


def _in_proj_kernel(x_ref, w_ref, cos_ref, s1_ref, s2_ref,
                    qkv_ref, z_ref, sq_ref, sk_ref, sv_ref, ba_ref):
    xb = x_ref[...].astype(BF16)
    qkv_ref[...] = _dot(xb, w_ref[:, C_QKV:C_Z])
    z_ref[...] = _dot(xb, w_ref[:, C_Z:C_SQ])
    sv_ref[...] = _dot(xb, w_ref[:, C_SV:C_BA])
    ba_ref[...] = _dot(xb, w_ref[:, C_BA:C_END])
    cos = cos_ref[...]
    s1 = s1_ref[...]
    s2 = s2_ref[...]

    def rope(t):
        return (t * cos + pltpu.roll(t, LANES - ROT_HALF, 1) * s1
                + pltpu.roll(t, ROT_HALF, 1) * s2)

    sk_ref[...] = rope(_dot(xb, w_ref[:, C_SK:C_SV]))
    for g in range(SWA_WIDTH // LANES):
        lo = C_SQ + g * LANES
        sq_ref[:, g * LANES:(g + 1) * LANES] = rope(_dot(xb, w_ref[:, lo:lo + LANES])).astype(BF16)


def _in_proj(x2d, w_all, cos, s1, s2):
    n = x2d.shape[0]
    period_tiles = cos.shape[0] // IN_TILE
    row = lambda i: (i, 0)
    tab = lambda i: (i % period_tiles, 0)
    out_shapes = (
        jax.ShapeDtypeStruct((n, CONV_DIM), F32),
        jax.ShapeDtypeStruct((n, GDN_WIDTH), F32),
        jax.ShapeDtypeStruct((n, SWA_WIDTH), BF16),
        jax.ShapeDtypeStruct((n, SWA_KV_WIDTH), F32),
        jax.ShapeDtypeStruct((n, SWA_KV_WIDTH), F32),
        jax.ShapeDtypeStruct((n, LANES), F32),
    )
    return pl.pallas_call(
        _in_proj_kernel,
        grid=(n // IN_TILE,),
        in_specs=[
            pl.BlockSpec((IN_TILE, D_MODEL), row),
            pl.BlockSpec((D_MODEL, C_END), lambda i: (0, 0)),
            pl.BlockSpec((IN_TILE, LANES), tab),
            pl.BlockSpec((IN_TILE, LANES), tab),
            pl.BlockSpec((IN_TILE, LANES), tab),
        ],
        out_specs=(
            pl.BlockSpec((IN_TILE, CONV_DIM), row),
            pl.BlockSpec((IN_TILE, GDN_WIDTH), row),
            pl.BlockSpec((IN_TILE, SWA_WIDTH), row),
            pl.BlockSpec((IN_TILE, SWA_KV_WIDTH), row),
            pl.BlockSpec((IN_TILE, SWA_KV_WIDTH), row),
            pl.BlockSpec((IN_TILE, LANES), row),
        ),
        out_shape=out_shapes,
        compiler_params=pltpu.CompilerParams(
            dimension_semantics=("arbitrary",), vmem_limit_bytes=VMEM_LIMIT),
        name="in_proj",
    )(x2d, w_all, cos, s1, s2)


def _rope_tables(pos):
    p = pos.shape[0]
    inv = ROPE_THETA ** (-jnp.arange(0, ROT_DIM, 2, dtype=F32) / ROT_DIM)
    ang = pos[:, None] * inv[None, :]
    cos = jnp.cos(ang)
    sin = jnp.sin(ang)
    rest = SWA_HEAD_DIM - ROT_DIM
    head_c = jnp.concatenate([cos, cos, jnp.ones((p, rest), F32)], 1)
    head_s1 = jnp.concatenate([-sin, jnp.zeros((p, SWA_HEAD_DIM - ROT_HALF), F32)], 1)
    head_s2 = jnp.concatenate([jnp.zeros((p, ROT_HALF), F32), sin, jnp.zeros((p, rest), F32)], 1)
    two = lambda t: jnp.concatenate([t, t], 1)
    return two(head_c), two(head_s1), two(head_s2)


def _gdn_kernel(L, qkv_ref, z_ref, ba_ref, hist_ref, s0_ref, convw_ref, alog_ref, dtb_ref,
                normw_ref, o_ref, sout_ref, xbuf, state):
    c = pl.program_id(1)
    nbat = qkv_ref.shape[0]
    chains = [(b, h) for b in range(nbat) for h in range(GDN_HEADS)]

    @pl.when(c == 0)
    def _():
        xbuf[:, 0:HIST_ROWS, :] = hist_ref[...]
        state[...] = s0_ref[...]

    xbuf[:, HIST_ROWS:HIST_ROWS + L, :] = qkv_ref[...]

    row = lax.broadcasted_iota(jnp.int32, (L, L), 0)
    col = lax.broadcasted_iota(jnp.int32, (L, L), 1)
    incl = row >= col
    strict = row > col
    lower_ones = incl.astype(BF16)
    eye = (row == col).astype(F32)

    beta_all, g_all = [], []
    for b in range(nbat):
        ba = ba_ref[b]
        beta_all.append(jax.nn.sigmoid(ba))
        sp_in = ba + dtb_ref[...]
        softplus = jnp.maximum(sp_in, 0.0) + jnp.log1p(jnp.exp(-jnp.abs(sp_in)))
        g_all.append(-jnp.exp(alog_ref[...]) * softplus)

    def conv_silu(b, c0):
        acc = None
        for j in range(CONV_WIDTH):
            lo = HIST_ROWS - (CONV_WIDTH - 1) + j
            t = xbuf[b, lo:lo + L, c0:c0 + GDN_D] * convw_ref[j:j + 1, c0:c0 + GDN_D]
            acc = t if acc is None else acc + t
        return _silu(acc)

    def l2n(t):
        return t * lax.rsqrt(jnp.sum(t * t, -1, keepdims=True) + NORM_EPS)

    q = [l2n(conv_silu(b, h * GDN_D)) * (GDN_D ** -0.5) for b, h in chains]
    k = [l2n(conv_silu(b, GDN_WIDTH + h * GDN_D)) for b, h in chains]
    v = [conv_silu(b, 2 * GDN_WIDTH + h * GDN_D) for b, h in chains]
    beta = [beta_all[b][:, h:h + 1] for b, h in chains]
    g = [g_all[b][:, GDN_HEADS + h:GDN_HEADS + h + 1] for b, h in chains]

    def pieces(t):
        p1 = t.astype(BF16).astype(F32)
        r1 = t - p1
        p2 = r1.astype(BF16).astype(F32)
        return p1, p2, r1 - p2

    g_parts = [pieces(t) for t in g]
    diff = [sum(_dot(lower_ones, jnp.where(strict, p, 0.0).astype(BF16)) for p in gp)
            for gp in g_parts]
    gcum3 = [_dot(lower_ones, jnp.concatenate(
        [jnp.broadcast_to(p, (L, GDN_D)) for p in gp], 1).astype(BF16)) for gp in g_parts]
    gcum = [t[:, :GDN_D] + t[:, GDN_D:2 * GDN_D] + t[:, 2 * GDN_D:] for t in gcum3]
    dec = [jnp.where(incl, jnp.exp(jnp.where(incl, d, 0.0)), 0.0) for d in diff]
    exp_g = [jnp.exp(t) for t in gcum]
    g_end = [t[L - 1:L, :] for t in gcum]
    kb = [t.astype(BF16) for t in k]
    kk = [_dot_nt(t, t) for t in kb]
    aqk = [_dot_nt(qq.astype(BF16), t) * d for qq, t, d in zip(q, kb, dec)]
    a = [jnp.where(strict, d * t, 0.0) * bt for d, t, bt in zip(dec, kk, beta)]

    n = [-t for t in a]
    n_s = [_split(t) for t in n]
    z = [jnp.concatenate([eye + t, _dot_split(ts, ts)], 1) for t, ts in zip(n, n_s)]
    keep_s = lax.broadcasted_iota(jnp.int32, (L, 2 * L), 1) < L
    for _ in range(int(math.log2(L)) - 2):
        pz = [_dot_split(_split(t[:, L:]), _split(t)) for t in z]
        z = [jnp.where(keep_s, t, 0.0) + d for t, d in zip(z, pz)]
    inv = [t[:, :L] + _dot_split(_split(t[:, L:]), _split(t[:, :L])) for t in z]
    inv_s = [_split(t) for t in inv]
    wy = [_dot_split(t, _split(jnp.concatenate([bt * vv, (bt * eg) * kk_], 1)))
          for t, bt, vv, eg, kk_ in zip(inv_s, beta, v, exp_g, k)]
    w = [t[:, :GDN_D] for t in wy]
    y = [t[:, GDN_D:] for t in wy]

    s = [state[b, h] for b, h in chains]
    sb = [t.astype(BF16) for t in s]
    yq = [_dot(jnp.concatenate([yy.astype(BF16), (eg * qq).astype(BF16)], 0), t)
          for yy, eg, qq, t in zip(y, exp_g, q, sb)]
    u = [ww - t[:L] for ww, t in zip(w, yq)]
    ub = [t.astype(BF16) for t in u]
    o = [t[L:] + _dot(aa.astype(BF16), uu) for t, aa, uu in zip(yq, aqk, ub)]
    k_dec = [(jnp.exp(ge - gc) * kk_).astype(BF16) for ge, gc, kk_ in zip(g_end, gcum, k)]
    s_new = [jnp.exp(ge) * t + _dot_tn(kd, uu) for ge, t, kd, uu in zip(g_end, s, k_dec, ub)]

    for (b, h), t, oo in zip(chains, s_new, o):
        state[b, h] = t
        on = oo * lax.rsqrt(jnp.mean(oo * oo, -1, keepdims=True) + NORM_EPS)
        zz = z_ref[b, :, h * GDN_D:(h + 1) * GDN_D]
        o_ref[b, :, h * GDN_D:(h + 1) * GDN_D] = (on * normw_ref[...] * _silu(zz)).astype(BF16)

    xbuf[:, 0:HIST_ROWS, :] = xbuf[:, L:L + HIST_ROWS, :]

    @pl.when(c == pl.num_programs(1) - 1)
    def _():
        sout_ref[...] = state[...]


def _gdn(L, qkv, z, ba, hist8, s0, conv_w, alog_pad, dtb_pad, norm_w):
    b, t, _ = qkv.shape
    nc = t // L
    nbat = GDN_BATCH
    tok = lambda i, c: (i, c, 0)
    per_b3 = lambda i, c: (i, 0, 0)
    per_b4 = lambda i, c: (i, 0, 0, 0)
    const2 = lambda i, c: (0, 0)
    return pl.pallas_call(
        functools.partial(_gdn_kernel, L),
        grid=(b // nbat, nc),
        in_specs=[
            pl.BlockSpec((nbat, L, CONV_DIM), tok),
            pl.BlockSpec((nbat, L, GDN_WIDTH), tok),
            pl.BlockSpec((nbat, L, LANES), tok),
            pl.BlockSpec((nbat, HIST_ROWS, CONV_DIM), per_b3),
            pl.BlockSpec((nbat, GDN_HEADS, GDN_D, GDN_D), per_b4),
            pl.BlockSpec((CONV_WIDTH, CONV_DIM), const2),
            pl.BlockSpec((1, LANES), const2),
            pl.BlockSpec((1, LANES), const2),
            pl.BlockSpec((1, GDN_D), const2),
        ],
        out_specs=(
            pl.BlockSpec((nbat, L, GDN_WIDTH), tok),
            pl.BlockSpec((nbat, GDN_HEADS, GDN_D, GDN_D), per_b4),
        ),
        out_shape=(
            jax.ShapeDtypeStruct((b, t, GDN_WIDTH), BF16),
            jax.ShapeDtypeStruct((b, GDN_HEADS, GDN_D, GDN_D), F32),
        ),
        scratch_shapes=[
            pltpu.VMEM((nbat, HIST_ROWS + L, CONV_DIM), F32),
            pltpu.VMEM((nbat, GDN_HEADS, GDN_D, GDN_D), F32),
        ],
        compiler_params=pltpu.CompilerParams(
            dimension_semantics=("arbitrary", "arbitrary"), vmem_limit_bytes=VMEM_LIMIT),
        name="gdn",
    )(qkv, z, ba, hist8, s0, conv_w, alog_pad, dtb_pad, norm_w)


def _swa_kernel(L, pos0, q_ref, k_ref, v_ref, sink_ref, o_ref):
    c = pl.program_id(1)
    nch = q_ref.shape[1] // L
    span = WINDOW + L
    rows = WINDOW + nch * L
    start = pl.multiple_of(c * (nch * L), L)
    kx = k_ref[0, pl.ds(start, rows), :]
    vx = v_ref[0, pl.ds(start, rows), :]
    kx_sw = pltpu.roll(kx, SWA_HEAD_DIM, 1)
    vx_sw = pltpu.roll(vx, SWA_HEAD_DIM, 1)
    low = lax.broadcasted_iota(jnp.int32, (rows, LANES), 1) < SWA_HEAD_DIM

    def halves(x, x_sw, kh):
        src_lo, src_hi = (x, x_sw) if kh == 0 else (x_sw, x)
        return (jnp.where(low, src_lo, 0.0).astype(BF16), jnp.where(low, 0.0, src_hi).astype(BF16))

    k_halves = [halves(kx, kx_sw, kh) for kh in range(SWA_KV_HEADS)]
    v_halves = [halves(vx, vx_sw, kh) for kh in range(SWA_KV_HEADS)]
    sinks = sink_ref[...]
    top_rows = lax.broadcasted_iota(jnp.int32, (2 * L, 1), 0) < L
    first_pos = pos0 - WINDOW + c * (nch * L)

    chains = [(u, kh, half) for u in range(nch) for kh in range(SWA_KV_HEADS) for half in range(2)]
    valid = [first_pos + u * L + lax.broadcasted_iota(jnp.int32, (1, span), 1) >= 0
             for u in range(nch)]
    q4 = {}
    for u in range(nch):
        for kh in range(SWA_KV_HEADS):
            c0 = kh * 2 * LANES
            q4[u, kh] = jnp.concatenate([q_ref[0, u * L:(u + 1) * L, c0:c0 + LANES],
                                         q_ref[0, u * L:(u + 1) * L, c0 + LANES:c0 + 2 * LANES]], 0)
    scores = [jnp.where(valid[u],
                        _dot_nt(q4[u, kh], k_halves[kh][half][u * L:u * L + span])
                        * (SWA_HEAD_DIM ** -0.5), -jnp.inf)
              for u, kh, half in chains]
    sink_cols = [jnp.where(top_rows, sinks[:, kh * SWA_GROUP + half:kh * SWA_GROUP + half + 1],
                           sinks[:, kh * SWA_GROUP + half + 2:kh * SWA_GROUP + half + 3])
                 for _, kh, half in chains]
    m = [jnp.maximum(jnp.max(s, -1, keepdims=True), sk) for s, sk in zip(scores, sink_cols)]
    p = [jnp.exp(s - mm) for s, mm in zip(scores, m)]
    den = [jnp.sum(pp, -1, keepdims=True) + jnp.exp(sk - mm) for pp, sk, mm in zip(p, sink_cols, m)]
    pv = [_dot(pp.astype(BF16), v_halves[kh][half][u * L:u * L + span]) / dd
          for pp, dd, (u, kh, half) in zip(p, den, chains)]
    for n in range(0, len(chains), 2):
        u, kh, _ = chains[n]
        o = pv[n] + pv[n + 1]
        c0 = kh * 2 * LANES
        o_ref[0, u * L:(u + 1) * L, c0:c0 + LANES] = o[0:L].astype(BF16)
        o_ref[0, u * L:(u + 1) * L, c0 + LANES:c0 + 2 * LANES] = o[L:2 * L].astype(BF16)


def _swa(L, pos0, q, k_ext, v_ext, sinks_pad):
    b, t, _ = q.shape
    ext = k_ext.shape[1]
    nch = SWA_CHUNKS if (t // L) % SWA_CHUNKS == 0 else 1
    return pl.pallas_call(
        functools.partial(_swa_kernel, L, pos0),
        grid=(b, t // (nch * L)),
        in_specs=[
            pl.BlockSpec((1, nch * L, SWA_WIDTH), lambda i, c: (i, c, 0)),
            pl.BlockSpec((1, ext, SWA_KV_WIDTH), lambda i, c: (i, 0, 0)),
            pl.BlockSpec((1, ext, SWA_KV_WIDTH), lambda i, c: (i, 0, 0)),
            pl.BlockSpec((1, LANES), lambda i, c: (0, 0)),
        ],
        out_specs=pl.BlockSpec((1, nch * L, SWA_WIDTH), lambda i, c: (i, c, 0)),
        out_shape=jax.ShapeDtypeStruct((b, t, SWA_WIDTH), BF16),
        compiler_params=pltpu.CompilerParams(
            dimension_semantics=("arbitrary", "arbitrary"), vmem_limit_bytes=VMEM_LIMIT),
        name="swa",
    )(q, k_ext, v_ext, sinks_pad)


def _layer_norm(r, g, b):
    mu = jnp.mean(r, -1, keepdims=True)
    d = r - mu
    var = jnp.mean(d * d, -1, keepdims=True)
    return d * lax.rsqrt(var + LN_EPS) * g + b


def _route(x1b, wrt_ref, rbias_ref, count_ref):
    tm = x1b.shape[0]
    scores = jax.nn.sigmoid(_dot_nt(wrt_ref[...], x1b))
    biased = scores + rbias_ref[...]
    neg = -jnp.inf
    r_g = lax.broadcasted_iota(jnp.int32, (GROUP_SIZE, tm), 0)
    blocks = []
    group_score = []
    for g in range(N_GROUPS):
        blk = biased[g * GROUP_SIZE:(g + 1) * GROUP_SIZE]
        m1 = jnp.max(blk, 0, keepdims=True)
        i1 = jnp.min(jnp.where(blk == m1, r_g, GROUP_SIZE), 0, keepdims=True)
        m2 = jnp.max(jnp.where(r_g == i1, neg, blk), 0, keepdims=True)
        blocks.append(blk)
        group_score.append(m1 + m2)
    masked = []
    for g in range(N_GROUPS):
        rank = jnp.zeros((1, tm), jnp.int32)
        for o in range(N_GROUPS):
            if o == g:
                continue
            ahead = group_score[o] > group_score[g]
            if o < g:
                ahead = ahead | (group_score[o] == group_score[g])
            rank = rank + ahead.astype(jnp.int32)
        masked.append(jnp.where(rank < TOPK_GROUPS, blocks[g], neg))
    masked = jnp.concatenate(masked, 0)
    r_e = lax.broadcasted_iota(jnp.int32, (N_EXPERTS, tm), 0)
    idx_rows = []
    w_rows = []
    hits = []
    for _ in range(TOP_K):
        m = jnp.max(masked, 0, keepdims=True)
        idx = jnp.min(jnp.where(masked == m, r_e, N_EXPERTS), 0, keepdims=True)
        hit = r_e == idx
        w_rows.append(jnp.sum(jnp.where(hit, scores, 0.0), 0, keepdims=True))
        masked = jnp.where(hit, neg, masked)
        idx_rows.append(idx)
        hits.append(hit)
    total = w_rows[0]
    for wr in w_rows[1:]:
        total = total + wr
    top_w = jnp.concatenate(w_rows, 0) / total * ROUTED_SCALE

    chosen = jnp.zeros((N_EXPERTS, tm), F32)
    for hit in hits:
        chosen = chosen + hit.astype(F32)
    t_r = lax.broadcasted_iota(jnp.int32, (tm, tm), 0)
    t_c = lax.broadcasted_iota(jnp.int32, (tm, tm), 1)
    before = (t_r < t_c).astype(BF16)
    prior = _dot(chosen.astype(BF16), before) + count_ref[...]
    rank_rows = [jnp.sum(jnp.where(hit, prior, 0.0), 0, keepdims=True) for hit in hits]
    count_ref[...] = count_ref[...] + jnp.sum(chosen, 1, keepdims=True)
    rank = jnp.concatenate(rank_rows, 0).astype(jnp.int32)
    return jnp.concatenate(idx_rows, 0), top_w, rank


def _mix_kernel(np_tiles, gp_ref, sp_ref, xp_ref, gs_ref, ss_ref, xs_ref, wo_ref, g1_ref, b1_ref,
                wrt_ref, rbias_ref, x1_ref, tope_ref, topw_ref, rank_ref, cnt_ref, count_acc):
    i = pl.program_id(0)

    @pl.when(i == 0)
    def _():
        count_acc[...] = jnp.zeros_like(count_acc)

    is_prompt = i < np_tiles
    gd = jnp.where(is_prompt, gp_ref[...], gs_ref[...])
    sw = jnp.where(is_prompt, sp_ref[...], ss_ref[...])
    x = jnp.where(is_prompt, xp_ref[...], xs_ref[...])
    mix = _dot(gd, wo_ref[0:GDN_WIDTH, :]) + _dot(sw, wo_ref[GDN_WIDTH:, :])
    x1 = _layer_norm(ALPHA * x + mix, g1_ref[...], b1_ref[...])
    _store_token_tiles(x1_ref, x1)
    top_e, top_w, rank = _route(x1.astype(BF16), wrt_ref, rbias_ref, count_acc)
    tope_ref[...] = top_e
    topw_ref[...] = top_w
    rank_ref[...] = rank
    cnt_ref[...] = count_acc[...].astype(jnp.int32)


def _mix(gd_p, sw_p, x_p, gd_s, sw_s, x_s, wo, g1, b1, wrt, rbias):
    n_p, n_s = x_p.shape[0], x_s.shape[0]
    np_tiles = n_p // MIX_TILE
    nt = n_p + n_s
    pm = lambda i: (jnp.minimum(i, np_tiles - 1), 0)
    sm = lambda i: (jnp.maximum(i - np_tiles, 0), 0)
    const = lambda i: (0, 0)
    col = lambda i: (0, i)
    return pl.pallas_call(
        functools.partial(_mix_kernel, np_tiles),
        grid=(nt // MIX_TILE,),
        in_specs=[
            pl.BlockSpec((MIX_TILE, GDN_WIDTH), pm),
            pl.BlockSpec((MIX_TILE, SWA_WIDTH), pm),
            pl.BlockSpec((MIX_TILE, D_MODEL), pm),
            pl.BlockSpec((MIX_TILE, GDN_WIDTH), sm),
            pl.BlockSpec((MIX_TILE, SWA_WIDTH), sm),
            pl.BlockSpec((MIX_TILE, D_MODEL), sm),
            pl.BlockSpec((D_MODEL, D_MODEL), const),
            pl.BlockSpec((1, D_MODEL), const),
            pl.BlockSpec((1, D_MODEL), const),
            pl.BlockSpec((N_EXPERTS, D_MODEL), const),
            pl.BlockSpec((N_EXPERTS, 1), const),
        ],
        out_specs=(
            pl.BlockSpec((MIX_TILE * ROW_TILES, LANES), lambda i: (i, 0)),
            pl.BlockSpec((TOP_K, MIX_TILE), col),
            pl.BlockSpec((TOP_K, MIX_TILE), col),
            pl.BlockSpec((TOP_K, MIX_TILE), col),
            pl.BlockSpec((N_EXPERTS, 1), const),
        ),
        out_shape=(
            jax.ShapeDtypeStruct((nt * ROW_TILES, LANES), F32),
            jax.ShapeDtypeStruct((TOP_K, nt), jnp.int32),
            jax.ShapeDtypeStruct((TOP_K, nt), F32),
            jax.ShapeDtypeStruct((TOP_K, nt), jnp.int32),
            jax.ShapeDtypeStruct((N_EXPERTS, 1), jnp.int32),
        ),
        scratch_shapes=[pltpu.VMEM((N_EXPERTS, 1), F32)],
        compiler_params=pltpu.CompilerParams(
            dimension_semantics=("arbitrary",), vmem_limit_bytes=VMEM_LIMIT),
        name="mix",
    )(gd_p, sw_p, x_p, gd_s, sw_s, x_s, wo, g1, b1, wrt, rbias)


def _dest_kernel(tope_ref, rank_ref, pstart_ref, dest_ref):
    tm = tope_ref.shape[1]
    r_e = lax.broadcasted_iota(jnp.int32, (N_EXPERTS, tm), 0)
    pstart = pstart_ref[...]
    top_e = tope_ref[...]
    rows = [jnp.sum(jnp.where(r_e == top_e[j:j + 1, :], pstart, 0), 0, keepdims=True)
            for j in range(TOP_K)]
    dest_ref[...] = (jnp.concatenate(rows, 0) + rank_ref[...]) * ROW_TILES


def _dest(top_e_t, rank_t, pstart):
    nt = top_e_t.shape[1]
    col = lambda i: (0, i)
    return pl.pallas_call(
        _dest_kernel,
        grid=(nt // DEST_TILE,),
        in_specs=[
            pl.BlockSpec((TOP_K, DEST_TILE), col),
            pl.BlockSpec((TOP_K, DEST_TILE), col),
            pl.BlockSpec((N_EXPERTS, 1), lambda i: (0, 0)),
        ],
        out_specs=pl.BlockSpec((TOP_K, DEST_TILE), col),
        out_shape=jax.ShapeDtypeStruct((TOP_K, nt), jnp.int32),
        compiler_params=pltpu.CompilerParams(dimension_semantics=("arbitrary",)),
        name="dest",
    )(top_e_t, rank_t, pstart)


def _experts_kernel(be_ref, nused_ref, c0_ref, wpar_ref, wnext_ref, stok_ref, x1_ref,
                    wg_ref, wu_ref, wd_ref, y_ref,
                    idx0, idx1, xbuf0, xbuf1, idx_sem, row_sem, wg_f, wu_f, wd_f, w_sem,
                    wg_b, wu_b, wd_b):
    i = pl.program_id(0)
    n_used = nused_ref[0]
    idx = (idx0, idx1)
    xbuf = (xbuf0, xbuf1)

    def weight_copies(e, s):
        return (pltpu.make_async_copy(wg_ref.at[e], wg_f.at[s], w_sem.at[s]),
                pltpu.make_async_copy(wu_ref.at[e], wu_f.at[s], w_sem.at[s]),
                pltpu.make_async_copy(wd_ref.at[e], wd_f.at[s], w_sem.at[s]))

    def idx_copy(block, s):
        base = pl.multiple_of(c0_ref[block] // LANES * LANES, LANES)
        return pltpu.make_async_copy(stok_ref.at[pl.ds(base, IDX_WINDOW)], idx[s], idx_sem.at[s])

    def start_rows(block, s):
        off = lax.rem(c0_ref[block], LANES)
        for r in range(EXPERT_BLOCK):
            src = pl.multiple_of(idx[s][off + r], ROW_TILES)
            pltpu.make_async_copy(x1_ref.at[pl.ds(src, ROW_TILES), :],
                                  xbuf[s].at[pl.ds(r * ROW_TILES, ROW_TILES), :],
                                  row_sem.at[s]).start(priority=r % 2)

    def wait_rows(s):
        pltpu.make_async_copy(x1_ref.at[pl.ds(0, EXPERT_BLOCK * ROW_TILES), :], xbuf[s],
                              row_sem.at[s]).wait()

    @pl.when(i == 0)
    def _():
        for cp in weight_copies(be_ref[0], 0):
            cp.start()
        idx_copy(0, 0).start()
        idx_copy(0, 0).wait()
        start_rows(0, 0)
        idx_copy(1, 1).start()

    prev = be_ref[jnp.maximum(i - 1, 0)]

    @pl.when((i < n_used) & ((i == 0) | (be_ref[i] != prev)))
    def _():
        par = wpar_ref[i]
        for cp in weight_copies(be_ref[i], par):
            cp.wait()

        @pl.when(wnext_ref[i] >= 0)
        def _():
            for cp in weight_copies(wnext_ref[i], 1 - par):
                cp.start()

        wg_b[...] = wg_f[par].astype(BF16)
        wu_b[...] = wu_f[par].astype(BF16)
        wd_b[...] = wd_f[par].astype(BF16)

    for slot in range(2):
        other = 1 - slot

        @pl.when((i < n_used) & (lax.rem(i, 2) == slot))
        def _():
            idx_copy(i + 1, other).wait()
            wait_rows(slot)
            start_rows(i + 1, other)
            idx_copy(i + 2, slot).start()
            xb = _load_token_tiles(xbuf[slot], EXPERT_BLOCK).astype(BF16)
            hb = _silu(_dot(xb, wg_b[...])) * _dot(xb, wu_b[...])
            _store_token_tiles(y_ref, _dot(hb.astype(BF16), wd_b[...]))

        @pl.when((i == n_used) & (lax.rem(i, 2) == slot))
        def _():
            wait_rows(slot)
            idx_copy(i + 1, other).wait()

    @pl.when(i >= n_used)
    def _():
        y_ref[...] = jnp.zeros_like(y_ref)


def _experts(block_e, n_used, block_c0, w_par, w_next, stok, x1, w_gate_e, w_up_e, w_down_e):
    nb = block_e.shape[0]
    grid_spec = pltpu.PrefetchScalarGridSpec(
        num_scalar_prefetch=5,
        grid=(nb,),
        in_specs=[pl.BlockSpec(memory_space=pl.ANY)] * 5,
        out_specs=pl.BlockSpec((EXPERT_BLOCK * ROW_TILES, LANES), lambda i, *_: (i, 0)),
        scratch_shapes=[
            pltpu.SMEM((IDX_WINDOW,), jnp.int32),
            pltpu.SMEM((IDX_WINDOW,), jnp.int32),
            pltpu.VMEM((EXPERT_BLOCK * ROW_TILES, LANES), F32),
            pltpu.VMEM((EXPERT_BLOCK * ROW_TILES, LANES), F32),
            pltpu.SemaphoreType.DMA((2,)),
            pltpu.SemaphoreType.DMA((2,)),
            pltpu.VMEM((2, D_MODEL, EXPERT_FF), F32),
            pltpu.VMEM((2, D_MODEL, EXPERT_FF), F32),
            pltpu.VMEM((2, EXPERT_FF, D_MODEL), F32),
            pltpu.SemaphoreType.DMA((2,)),
            pltpu.VMEM((D_MODEL, EXPERT_FF), BF16),
            pltpu.VMEM((D_MODEL, EXPERT_FF), BF16),
            pltpu.VMEM((EXPERT_FF, D_MODEL), BF16),
        ],
    )
    return pl.pallas_call(
        _experts_kernel,
        grid_spec=grid_spec,
        out_shape=jax.ShapeDtypeStruct((nb * EXPERT_BLOCK * ROW_TILES, LANES), F32),
        compiler_params=pltpu.CompilerParams(
            dimension_semantics=("arbitrary",), vmem_limit_bytes=VMEM_LIMIT),
        name="experts",
    )(block_e, n_used, block_c0, w_par, w_next, stok, x1, w_gate_e, w_up_e, w_down_e)


def _final_kernel(np_tiles, dest_ref, dest_next_ref, x1_ref, topw_ref, yb_ref, wsg_ref, wsu_ref,
                  wsd_ref, g2_ref, b2_ref, yp_ref, ys_ref, ybuf, sem):
    i = pl.program_id(0)
    slot = lax.rem(i, 2)

    def issue_rows(d_ref, s):
        def body(t, carry):
            dst = pl.multiple_of(t * ROW_TILES, ROW_TILES)
            for j in range(TOP_K):
                src = pl.multiple_of(d_ref[j, t], ROW_TILES)
                pltpu.make_async_copy(yb_ref.at[pl.ds(src, ROW_TILES), :],
                                      ybuf.at[s, j, pl.ds(dst, ROW_TILES), :],
                                      sem.at[s]).start(priority=j % 2)
            return carry
        lax.fori_loop(0, MIX_TILE, body, 0)

    @pl.when(i == 0)
    def _():
        issue_rows(dest_ref, 0)

    @pl.when(i + 1 < pl.num_programs(0))
    def _():
        issue_rows(dest_next_ref, 1 - slot)

    x1 = _load_token_tiles(x1_ref, MIX_TILE)
    xb = x1.astype(BF16)
    hs = _silu(_dot(xb, wsg_ref[...])) * _dot(xb, wsu_ref[...])
    ff = _dot(hs.astype(BF16), wsd_ref[...])
    r = lax.broadcasted_iota(jnp.int32, (MIX_TILE, MIX_TILE), 0)
    c = lax.broadcasted_iota(jnp.int32, (MIX_TILE, MIX_TILE), 1)
    w_cols = _dot_nt((r == c).astype(F32), topw_ref[...], HIGHEST)
    for j in range(TOP_K):
        pltpu.make_async_copy(yb_ref.at[pl.ds(0, MIX_TILE * ROW_TILES), :], ybuf.at[slot, j],
                              sem.at[slot]).wait()
    for j in range(TOP_K):
        ff = ff + w_cols[:, j:j + 1] * _load_token_tiles(ybuf.at[slot, j], MIX_TILE)
    y = _layer_norm(ALPHA * x1 + ff, g2_ref[...], b2_ref[...])

    @pl.when(i < np_tiles)
    def _():
        yp_ref[...] = y

    @pl.when(i >= np_tiles)
    def _():
        ys_ref[...] = y


def _final(n_p, dest, x1, top_w_t, yb, wsg, wsu, wsd, g2, b2):
    nt = x1.shape[0] // ROW_TILES
    np_tiles = n_p // MIX_TILE
    row = lambda i: (i, 0)
    const = lambda i: (0, 0)
    n_tiles = nt // MIX_TILE
    return pl.pallas_call(
        functools.partial(_final_kernel, np_tiles),
        grid=(n_tiles,),
        in_specs=[
            pl.BlockSpec((TOP_K, MIX_TILE), lambda i: (0, i), memory_space=pltpu.SMEM),
            pl.BlockSpec((TOP_K, MIX_TILE), lambda i: (0, jnp.minimum(i + 1, n_tiles - 1)),
                         memory_space=pltpu.SMEM),
            pl.BlockSpec((MIX_TILE * ROW_TILES, LANES), row),
            pl.BlockSpec((TOP_K, MIX_TILE), lambda i: (0, i)),
            pl.BlockSpec(memory_space=pl.ANY),
            pl.BlockSpec((D_MODEL, EXPERT_FF), const),
            pl.BlockSpec((D_MODEL, EXPERT_FF), const),
            pl.BlockSpec((EXPERT_FF, D_MODEL), const),
            pl.BlockSpec((1, D_MODEL), const),
            pl.BlockSpec((1, D_MODEL), const),
        ],
        out_specs=(
            pl.BlockSpec((MIX_TILE, D_MODEL), lambda i: (jnp.minimum(i, np_tiles - 1), 0)),
            pl.BlockSpec((MIX_TILE, D_MODEL), lambda i: (jnp.maximum(i - np_tiles, 0), 0)),
        ),
        out_shape=(
            jax.ShapeDtypeStruct((n_p, D_MODEL), F32),
            jax.ShapeDtypeStruct((nt - n_p, D_MODEL), F32),
        ),
        scratch_shapes=[
            pltpu.VMEM((2, TOP_K, MIX_TILE * ROW_TILES, LANES), F32),
            pltpu.SemaphoreType.DMA((2,)),
        ],
        compiler_params=pltpu.CompilerParams(
            dimension_semantics=("arbitrary",), vmem_limit_bytes=VMEM_LIMIT),
        name="final",
    )(dest, dest, x1, top_w_t, yb, wsg, wsu, wsd, g2, b2)


def _expert_layout(counts, nk):
    nb = -(-nk // EXPERT_BLOCK) + N_EXPERTS
    padded = (counts + EXPERT_BLOCK - 1) // EXPERT_BLOCK * EXPERT_BLOCK
    pend = jnp.cumsum(padded)
    pstart = (pend - padded).astype(jnp.int32)
    block_start = jnp.arange(nb, dtype=jnp.int32) * EXPERT_BLOCK
    block_e = jnp.sum((pend[None, :] <= block_start[:, None]).astype(jnp.int32), 1)
    block_e = jnp.minimum(block_e, N_EXPERTS - 1)
    n_used = (pend[-1] // EXPERT_BLOCK).astype(jnp.int32).reshape(1)
    start = (jnp.cumsum(counts) - counts).astype(jnp.int32)
    block_c0 = start[block_e] + block_start - pstart[block_e]
    block_c0 = jnp.where(block_start < pend[-1], block_c0, 0)
    block_c0 = jnp.concatenate([block_c0, jnp.zeros((2,), jnp.int32)]).astype(jnp.int32)
    ids = jnp.arange(nb, dtype=jnp.int32)
    used = block_start < pend[-1]
    switch = used & ((ids == 0) | (block_e != jnp.roll(block_e, 1)))
    w_par = ((jnp.cumsum(switch.astype(jnp.int32)) - 1) % 2).astype(jnp.int32)
    later_switch = (ids[None, :] > ids[:, None]) & switch[None, :]
    next_pos = jnp.min(jnp.where(later_switch, ids[None, :], nb), axis=1)
    w_next = jnp.where(next_pos < nb, block_e[jnp.minimum(next_pos, nb - 1)], -1).astype(jnp.int32)
    return pstart, block_e, n_used, block_c0, w_par, w_next


def _pad_lanes(v, offset=0):
    out = jnp.zeros((1, LANES), F32)
    return out.at[0, offset:offset + v.shape[0]].set(v.astype(F32))


def kernel(x_prompt, x_sample, state_conv, state_gdn, cache_swa_k, cache_swa_v, w_in, conv_w, a_log, dt_bias, gdn_norm_w, attn_sinks, w_o, ln1_g, ln1_b, w_router, router_bias, w_gate_e, w_up_e, w_down_e, w_shared_gate, w_shared_up, w_shared_down, ln2_g, ln2_b):
    bp, tp, d = x_prompt.shape
    bs, ts, _ = x_sample.shape
    n_p, n_s = bp * tp, bs * ts

    wi = w_in[0]
    o1 = CONV_DIM
    o2 = o1 + GDN_WIDTH
    o4 = o2 + 2 * GDN_HEADS
    o5 = o4 + SWA_WIDTH
    o6 = o5 + SWA_KV_WIDTH
    ba_cols = jnp.zeros((d, LANES), F32).at[:, :2 * GDN_HEADS].set(wi[:, o2:o4])
    w_all = jnp.concatenate([wi[:, :o2], wi[:, o4:o5], wi[:, o5:o6], wi[:, o6:], ba_cols], 1).astype(BF16)
    wo = w_o[0].astype(BF16)
    wrt = w_router[0].T.astype(BF16)
    rbias = router_bias[0].astype(F32).reshape(N_EXPERTS, 1)
    alog_pad = _pad_lanes(a_log[0], GDN_HEADS)
    dtb_pad = _pad_lanes(dt_bias[0], GDN_HEADS)
    sinks_pad = _pad_lanes(attn_sinks[0])
    norm_w = gdn_norm_w[0].reshape(1, GDN_D)
    g1, b1 = ln1_g[0].reshape(1, d), ln1_b[0].reshape(1, d)
    g2, b2 = ln2_g[0].reshape(1, d), ln2_b[0].reshape(1, d)

    def front(x, pos0, L, conv_hist, s0, k_hist, v_hist):
        b, t, _ = x.shape
        period = max(t, IN_TILE)
        pos = pos0 + (jnp.arange(period, dtype=jnp.int32) % t).astype(F32)
        cos, s1, s2 = _rope_tables(pos)
        x2d = x.reshape(b * t, d)
        qkv, z, sq, sk, sv, ba = _in_proj(x2d, w_all, cos, s1, s2)
        qkv = qkv.reshape(b, t, CONV_DIM)
        hist8 = jnp.concatenate(
            [jnp.zeros((b, HIST_ROWS - (CONV_WIDTH - 1), CONV_DIM), F32), conv_hist], 1)
        gd, s_new = _gdn(L, qkv, z.reshape(b, t, GDN_WIDTH), ba.reshape(b, t, LANES), hist8, s0,
                         conv_w[0], alog_pad, dtb_pad, norm_w)
        k_ext = jnp.concatenate([k_hist.reshape(b, WINDOW, SWA_KV_WIDTH),
                                 sk.reshape(b, t, SWA_KV_WIDTH)], 1)
        v_ext = jnp.concatenate([v_hist.reshape(b, WINDOW, SWA_KV_WIDTH),
                                 sv.reshape(b, t, SWA_KV_WIDTH)], 1)
        sw = _swa(L, pos0, sq.reshape(b, t, SWA_WIDTH), k_ext, v_ext, sinks_pad)
        conv_new = jnp.concatenate([conv_hist, qkv], 1)[:, -(CONV_WIDTH - 1):]
        k_new = k_ext[:, -WINDOW:].reshape(b, WINDOW, SWA_KV_HEADS, SWA_HEAD_DIM)
        v_new = v_ext[:, -WINDOW:].reshape(b, WINDOW, SWA_KV_HEADS, SWA_HEAD_DIM)
        return (x2d, gd.reshape(b * t, GDN_WIDTH), sw.reshape(b * t, SWA_WIDTH),
                conv_new, s_new, k_new, v_new)

    zeros = lambda *s: jnp.zeros(s, F32)
    xp2, gd_p, sw_p, conv_p, gdn_p, k_p, v_p = front(
        x_prompt, 0, CHUNK, zeros(bp, CONV_WIDTH - 1, CONV_DIM),
        zeros(bp, GDN_HEADS, GDN_D, GDN_D), zeros(bp, WINDOW, SWA_KV_WIDTH),
        zeros(bp, WINDOW, SWA_KV_WIDTH))
    xs2, gd_s, sw_s, conv_s, gdn_s, k_s, v_s = front(
        x_sample, PAST_LEN, ts, state_conv[0], state_gdn[0], cache_swa_k[0], cache_swa_v[0])

    x1, top_e_t, top_w_t, rank_t, counts = _mix(
        gd_p, sw_p, xp2, gd_s, sw_s, xs2, wo, g1, b1, wrt, rbias)

    pstart, block_e, n_used, block_c0, w_par, w_next = _expert_layout(
        counts[:, 0], TOP_K * (n_p + n_s))
    dest = _dest(top_e_t, rank_t, pstart.reshape(N_EXPERTS, 1))
    order = jnp.argsort(top_e_t.T.reshape(-1))
    stok = jnp.concatenate([(order // TOP_K * ROW_TILES).astype(jnp.int32),
                            jnp.zeros((IDX_WINDOW,), jnp.int32)])
    yb = _experts(block_e, n_used, block_c0, w_par, w_next, stok, x1,
                  w_gate_e[0], w_up_e[0], w_down_e[0])

    y_p, y_s = _final(n_p, dest, x1, top_w_t, yb,
                      w_shared_gate[0].astype(BF16), w_shared_up[0].astype(BF16),
                      w_shared_down[0].astype(BF16), g2, b2)
    return (y_p.reshape(bp, tp, d), y_s.reshape(bs, ts, d),
            conv_p[None], gdn_p[None], k_p[None], v_p[None],
            conv_s[None], gdn_s[None], k_s[None], v_s[None])
```

```python
import functools
import math

import jax
import jax.numpy as jnp
from jax import lax
from jax.experimental import pallas as pl
from jax.experimental.pallas import tpu as pltpu

F32 = jnp.float32
BF16 = jnp.bfloat16
HIGHEST = lax.Precision.HIGHEST

D_MODEL = 1024
CHUNK = 64
GDN_HEADS = 4
GDN_D = 128
GDN_WIDTH = GDN_HEADS * GDN_D
CONV_DIM = 3 * GDN_WIDTH
CONV_WIDTH = 4
SWA_HEAD_DIM = 64
SWA_Q_HEADS = 8
SWA_KV_HEADS = 2
SWA_GROUP = SWA_Q_HEADS // SWA_KV_HEADS
SWA_WIDTH = SWA_Q_HEADS * SWA_HEAD_DIM
SWA_KV_WIDTH = SWA_KV_HEADS * SWA_HEAD_DIM
WINDOW = 128
ROPE_THETA = 500000.0
ROT_DIM = SWA_HEAD_DIM // 4
ROT_HALF = ROT_DIM // 2
N_EXPERTS = 256
TOP_K = 8
N_GROUPS = 8
GROUP_SIZE = N_EXPERTS // N_GROUPS
TOPK_GROUPS = 4
EXPERT_FF = 256
ROUTED_SCALE = 2.5
ALPHA = 2.0 ** 0.25
LN_EPS = 1e-5
NORM_EPS = 1e-6
PAST_LEN = 4096

LANES = 128
HIST_ROWS = 8
ROW_TILES = D_MODEL // LANES

C_QKV = 0
C_Z = C_QKV + CONV_DIM
C_SQ = C_Z + GDN_WIDTH
C_SK = C_SQ + SWA_WIDTH
C_SV = C_SK + SWA_KV_WIDTH
C_BA = C_SV + SWA_KV_WIDTH
C_END = C_BA + LANES

IN_TILE = 512
MIX_TILE = 256
DEST_TILE = 512
EXPERT_BLOCK = 256
IDX_WINDOW = EXPERT_BLOCK + LANES
GDN_BATCH = 4
SWA_CHUNKS = 4
VMEM_LIMIT = 56 * 1024 * 1024


def _dot(a, b, precision=None):
    return jnp.dot(a, b, preferred_element_type=F32, precision=precision)


def _dot_nt(a, b, precision=None):
    return lax.dot_general(a, b, (((1,), (1,)), ((), ())),
                           preferred_element_type=F32, precision=precision)


def _dot_tn(a, b):
    return lax.dot_general(a, b, (((0,), (0,)), ((), ())), preferred_element_type=F32)


def _split(x):
    hi = x.astype(BF16)
    return hi, (x - hi.astype(F32)).astype(BF16)


def _dot_split(a, b):
    return _dot(a[0], b[0]) + (_dot(a[1], b[0]) + _dot(a[0], b[1]))


def _silu(x):
    return x * jax.nn.sigmoid(x)


def _store_token_tiles(ref, x):
    m = x.shape[0]
    for s in range(ROW_TILES):
        ref[pl.ds(s, m, stride=ROW_TILES), :] = x[:, s * LANES:(s + 1) * LANES]


def _load_token_tiles(ref, m):
    return jnp.concatenate([ref[pl.ds(s, m, stride=ROW_TILES), :] for s in range(ROW_TILES)], 1)


def _in_proj_kernel(x_ref, w_ref, cos_ref, s1_ref, s2_ref,
                    qkv_ref, z_ref, sq_ref, sk_ref, sv_ref, ba_ref):
    xb = x_ref[...].astype(BF16)
    qkv_ref[...] = _dot(xb, w_ref[:, C_QKV:C_Z])
    z_ref[...] = _dot(xb, w_ref[:, C_Z:C_SQ])
    sv_ref[...] = _dot(xb, w_ref[:, C_SV:C_BA])
    ba_ref[...] = _dot(xb, w_ref[:, C_BA:C_END])
    cos = cos_ref[...]
    s1 = s1_ref[...]
    s2 = s2_ref[...]

    def rope(t):
        return (t * cos + pltpu.roll(t, LANES - ROT_HALF, 1) * s1
                + pltpu.roll(t, ROT_HALF, 1) * s2)

    sk_ref[...] = rope(_dot(xb, w_ref[:, C_SK:C_SV]))
    for g in range(SWA_WIDTH // LANES):
        lo = C_SQ + g * LANES
        sq_ref[:, g * LANES:(g + 1) * LANES] = rope(_dot(xb, w_ref[:, lo:lo + LANES])).astype(BF16)


def _in_proj(x2d, w_all, cos, s1, s2):
    n = x2d.shape[0]
    period_tiles = cos.shape[0] // IN_TILE
    row = lambda i: (i, 0)
    tab = lambda i: (i % period_tiles, 0)
    out_shapes = (
        jax.ShapeDtypeStruct((n, CONV_DIM), F32),
        jax.ShapeDtypeStruct((n, GDN_WIDTH), F32),
        jax.ShapeDtypeStruct((n, SWA_WIDTH), BF16),
        jax.ShapeDtypeStruct((n, SWA_KV_WIDTH), F32),
        jax.ShapeDtypeStruct((n, SWA_KV_WIDTH), F32),
        jax.ShapeDtypeStruct((n, LANES), F32),
    )
    return pl.pallas_call(
        _in_proj_kernel,
        grid=(n // IN_TILE,),
        in_specs=[
            pl.BlockSpec((IN_TILE, D_MODEL), row),
            pl.BlockSpec((D_MODEL, C_END), lambda i: (0, 0)),
            pl.BlockSpec((IN_TILE, LANES), tab),
            pl.BlockSpec((IN_TILE, LANES), tab),
            pl.BlockSpec((IN_TILE, LANES), tab),
        ],
        out_specs=(
            pl.BlockSpec((IN_TILE, CONV_DIM), row),
            pl.BlockSpec((IN_TILE, GDN_WIDTH), row),
            pl.BlockSpec((IN_TILE, SWA_WIDTH), row),
            pl.BlockSpec((IN_TILE, SWA_KV_WIDTH), row),
            pl.BlockSpec((IN_TILE, SWA_KV_WIDTH), row),
            pl.BlockSpec((IN_TILE, LANES), row),
        ),
        out_shape=out_shapes,
        compiler_params=pltpu.CompilerParams(
            dimension_semantics=("arbitrary",), vmem_limit_bytes=VMEM_LIMIT),
        name="in_proj",
    )(x2d, w_all, cos, s1, s2)


def _rope_tables(pos):
    p = pos.shape[0]
    inv = ROPE_THETA ** (-jnp.arange(0, ROT_DIM, 2, dtype=F32) / ROT_DIM)
    ang = pos[:, None] * inv[None, :]
    cos = jnp.cos(ang)
    sin = jnp.sin(ang)
    rest = SWA_HEAD_DIM - ROT_DIM
    head_c = jnp.concatenate([cos, cos, jnp.ones((p, rest), F32)], 1)
    head_s1 = jnp.concatenate([-sin, jnp.zeros((p, SWA_HEAD_DIM - ROT_HALF), F32)], 1)
    head_s2 = jnp.concatenate([jnp.zeros((p, ROT_HALF), F32), sin, jnp.zeros((p, rest), F32)], 1)
    two = lambda t: jnp.concatenate([t, t], 1)
    return two(head_c), two(head_s1), two(head_s2)


def _gdn_kernel(L, qkv_ref, z_ref, ba_ref, hist_ref, s0_ref, convw_ref, alog_ref, dtb_ref,
                normw_ref, o_ref, sout_ref, xbuf, state):
    c = pl.program_id(1)
    nbat = qkv_ref.shape[0]
    chains = [(b, h) for b in range(nbat) for h in range(GDN_HEADS)]

    @pl.when(c == 0)
    def _():
        xbuf[:, 0:HIST_ROWS, :] = hist_ref[...]
        state[...] = s0_ref[...]

    xbuf[:, HIST_ROWS:HIST_ROWS + L, :] = qkv_ref[...]

    row = lax.broadcasted_iota(jnp.int32, (L, L), 0)
    col = lax.broadcasted_iota(jnp.int32, (L, L), 1)
    incl = row >= col
    strict = row > col
    lower_ones = incl.astype(BF16)
    eye = (row == col).astype(F32)

    beta_all, g_all = [], []
    for b in range(nbat):
        ba = ba_ref[b]
        beta_all.append(jax.nn.sigmoid(ba))
        sp_in = ba + dtb_ref[...]
        softplus = jnp.maximum(sp_in, 0.0) + jnp.log1p(jnp.exp(-jnp.abs(sp_in)))
        g_all.append(-jnp.exp(alog_ref[...]) * softplus)

    def conv_silu(b, c0):
        acc = None
        for j in range(CONV_WIDTH):
            lo = HIST_ROWS - (CONV_WIDTH - 1) + j
            t = xbuf[b, lo:lo + L, c0:c0 + GDN_D] * convw_ref[j:j + 1, c0:c0 + GDN_D]
            acc = t if acc is None else acc + t
        return _silu(acc)

    def l2n(t):
        return t * lax.rsqrt(jnp.sum(t * t, -1, keepdims=True) + NORM_EPS)

    q = [l2n(conv_silu(b, h * GDN_D)) * (GDN_D ** -0.5) for b, h in chains]
    k = [l2n(conv_silu(b, GDN_WIDTH + h * GDN_D)) for b, h in chains]
    v = [conv_silu(b, 2 * GDN_WIDTH + h * GDN_D) for b, h in chains]
    beta = [beta_all[b][:, h:h + 1] for b, h in chains]
    g = [g_all[b][:, GDN_HEADS + h:GDN_HEADS + h + 1] for b, h in chains]

    def pieces(t):
        p1 = t.astype(BF16).astype(F32)
        r1 = t - p1
        p2 = r1.astype(BF16).astype(F32)
        return p1, p2, r1 - p2

    g_parts = [pieces(t) for t in g]
    diff = [sum(_dot(lower_ones, jnp.where(strict, p, 0.0).astype(BF16)) for p in gp)
            for gp in g_parts]
    gcum3 = [_dot(lower_ones, jnp.concatenate(
        [jnp.broadcast_to(p, (L, GDN_D)) for p in gp], 1).astype(BF16)) for gp in g_parts]
    gcum = [t[:, :GDN_D] + t[:, GDN_D:2 * GDN_D] + t[:, 2 * GDN_D:] for t in gcum3]
    dec = [jnp.where(incl, jnp.exp(jnp.where(incl, d, 0.0)), 0.0) for d in diff]
    exp_g = [jnp.exp(t) for t in gcum]
    g_end = [t[L - 1:L, :] for t in gcum]
    kb = [t.astype(BF16) for t in k]
    kk = [_dot_nt(t, t) for t in kb]
    aqk = [_dot_nt(qq.astype(BF16), t) * d for qq, t, d in zip(q, kb, dec)]
    a = [jnp.where(strict, d * t, 0.0) * bt for d, t, bt in zip(dec, kk, beta)]

    n = [-t for t in a]
    n_s = [_split(t) for t in n]
    z = [jnp.concatenate([eye + t, _dot_split(ts, ts)], 1) for t, ts in zip(n, n_s)]
    keep_s = lax.broadcasted_iota(jnp.int32, (L, 2 * L), 1) < L
    for _ in range(int(math.log2(L)) - 2):
        pz = [_dot_split(_split(t[:, L:]), _split(t)) for t in z]
        z = [jnp.where(keep_s, t, 0.0) + d for t, d in zip(z, pz)]
    inv = [t[:, :L] + _dot_split(_split(t[:, L:]), _split(t[:, :L])) for t in z]
    inv_s = [_split(t) for t in inv]
    wy = [_dot_split(t, _split(jnp.concatenate([bt * vv, (bt * eg) * kk_], 1)))
          for t, bt, vv, eg, kk_ in zip(inv_s, beta, v, exp_g, k)]
    w = [t[:, :GDN_D] for t in wy]
    y = [t[:, GDN_D:] for t in wy]

    s = [state[b, h] for b, h in chains]
    sb = [t.astype(BF16) for t in s]
    yq = [_dot(jnp.concatenate([yy.astype(BF16), (eg * qq).astype(BF16)], 0), t)
          for yy, eg, qq, t in zip(y, exp_g, q, sb)]
    u = [ww - t[:L] for ww, t in zip(w, yq)]
    ub = [t.astype(BF16) for t in u]
    o = [t[L:] + _dot(aa.astype(BF16), uu) for t, aa, uu in zip(yq, aqk, ub)]
    k_dec = [(jnp.exp(ge - gc) * kk_).astype(BF16) for ge, gc, kk_ in zip(g_end, gcum, k)]
    s_new = [jnp.exp(ge) * t + _dot_tn(kd, uu) for ge, t, kd, uu in zip(g_end, s, k_dec, ub)]

    for (b, h), t, oo in zip(chains, s_new, o):
        state[b, h] = t
        on = oo * lax.rsqrt(jnp.mean(oo * oo, -1, keepdims=True) + NORM_EPS)
        zz = z_ref[b, :, h * GDN_D:(h + 1) * GDN_D]
        o_ref[b, :, h * GDN_D:(h + 1) * GDN_D] = (on * normw_ref[...] * _silu(zz)).astype(BF16)

    xbuf[:, 0:HIST_ROWS, :] = xbuf[:, L:L + HIST_ROWS, :]

    @pl.when(c == pl.num_programs(1) - 1)
    def _():
        sout_ref[...] = state[...]


def _gdn(L, qkv, z, ba, hist8, s0, conv_w, alog_pad, dtb_pad, norm_w):
    b, t, _ = qkv.shape
    nc = t // L
    nbat = GDN_BATCH
    tok = lambda i, c: (i, c, 0)
    per_b3 = lambda i, c: (i, 0, 0)
    per_b4 = lambda i, c: (i, 0, 0, 0)
    const2 = lambda i, c: (0, 0)
    return pl.pallas_call(
        functools.partial(_gdn_kernel, L),
        grid=(b // nbat, nc),
        in_specs=[
            pl.BlockSpec((nbat, L, CONV_DIM), tok),
            pl.BlockSpec((nbat, L, GDN_WIDTH), tok),
            pl.BlockSpec((nbat, L, LANES), tok),
            pl.BlockSpec((nbat, HIST_ROWS, CONV_DIM), per_b3),
            pl.BlockSpec((nbat, GDN_HEADS, GDN_D, GDN_D), per_b4),
            pl.BlockSpec((CONV_WIDTH, CONV_DIM), const2),
            pl.BlockSpec((1, LANES), const2),
            pl.BlockSpec((1, LANES), const2),
            pl.BlockSpec((1, GDN_D), const2),
        ],
        out_specs=(
            pl.BlockSpec((nbat, L, GDN_WIDTH), tok),
            pl.BlockSpec((nbat, GDN_HEADS, GDN_D, GDN_D), per_b4),
        ),
        out_shape=(
            jax.ShapeDtypeStruct((b, t, GDN_WIDTH), BF16),
            jax.ShapeDtypeStruct((b, GDN_HEADS, GDN_D, GDN_D), F32),
        ),
        scratch_shapes=[
            pltpu.VMEM((nbat, HIST_ROWS + L, CONV_DIM), F32),
            pltpu.VMEM((nbat, GDN_HEADS, GDN_D, GDN_D), F32),
        ],
        compiler_params=pltpu.CompilerParams(
            dimension_semantics=("arbitrary", "arbitrary"), vmem_limit_bytes=VMEM_LIMIT),
        name="gdn",
    )(qkv, z, ba, hist8, s0, conv_w, alog_pad, dtb_pad, norm_w)


def _swa_kernel(L, pos0, q_ref, k_ref, v_ref, sink_ref, o_ref):
    c = pl.program_id(1)
    nch = q_ref.shape[1] // L
    span = WINDOW + L
    rows = WINDOW + nch * L
    start = pl.multiple_of(c * (nch * L), L)
    kx = k_ref[0, pl.ds(start, rows), :]
    vx = v_ref[0, pl.ds(start, rows), :]
    kx_sw = pltpu.roll(kx, SWA_HEAD_DIM, 1)
    vx_sw = pltpu.roll(vx, SWA_HEAD_DIM, 1)
    low = lax.broadcasted_iota(jnp.int32, (rows, LANES), 1) < SWA_HEAD_DIM

    def halves(x, x_sw, kh):
        src_lo, src_hi = (x, x_sw) if kh == 0 else (x_sw, x)
        return (jnp.where(low, src_lo, 0.0).astype(BF16), jnp.where(low, 0.0, src_hi).astype(BF16))

    k_halves = [halves(kx, kx_sw, kh) for kh in range(SWA_KV_HEADS)]
    v_halves = [halves(vx, vx_sw, kh) for kh in range(SWA_KV_HEADS)]
    sinks = sink_ref[...]
    top_rows = lax.broadcasted_iota(jnp.int32, (2 * L, 1), 0) < L
    first_pos = pos0 - WINDOW + c * (nch * L)

    chains = [(u, kh, half) for u in range(nch) for kh in range(SWA_KV_HEADS) for half in range(2)]
    valid = [first_pos + u * L + lax.broadcasted_iota(jnp.int32, (1, span), 1) >= 0
             for u in range(nch)]
    q4 = {}
    for u in range(nch):
        for kh in range(SWA_KV_HEADS):
            c0 = kh * 2 * LANES
            q4[u, kh] = jnp.concatenate([q_ref[0, u * L:(u + 1) * L, c0:c0 + LANES],
                                         q_ref[0, u * L:(u + 1) * L, c0 + LANES:c0 + 2 * LANES]], 0)
    scores = [jnp.where(valid[u],
                        _dot_nt(q4[u, kh], k_halves[kh][half][u * L:u * L + span])
                        * (SWA_HEAD_DIM ** -0.5), -jnp.inf)
              for u, kh, half in chains]
    sink_cols = [jnp.where(top_rows, sinks[:, kh * SWA_GROUP + half:kh * SWA_GROUP + half + 1],
                           sinks[:, kh * SWA_GROUP + half + 2:kh * SWA_GROUP + half + 3])
                 for _, kh, half in chains]
    m = [jnp.maximum(jnp.max(s, -1, keepdims=True), sk) for s, sk in zip(scores, sink_cols)]
    p = [jnp.exp(s - mm) for s, mm in zip(scores, m)]
    den = [jnp.sum(pp, -1, keepdims=True) + jnp.exp(sk - mm) for pp, sk, mm in zip(p, sink_cols, m)]
    pv = [_dot(pp.astype(BF16), v_halves[kh][half][u * L:u * L + span]) / dd
          for pp, dd, (u, kh, half) in zip(p, den, chains)]
    for n in range(0, len(chains), 2):
        u, kh, _ = chains[n]
        o = pv[n] + pv[n + 1]
        c0 = kh * 2 * LANES
        o_ref[0, u * L:(u + 1) * L, c0:c0 + LANES] = o[0:L].astype(BF16)
        o_ref[0, u * L:(u + 1) * L, c0 + LANES:c0 + 2 * LANES] = o[L:2 * L].astype(BF16)


def _swa(L, pos0, q, k_ext, v_ext, sinks_pad):
    b, t, _ = q.shape
    ext = k_ext.shape[1]
    nch = SWA_CHUNKS if (t // L) % SWA_CHUNKS == 0 else 1
    return pl.pallas_call(
        functools.partial(_swa_kernel, L, pos0),
        grid=(b, t // (nch * L)),
        in_specs=[
            pl.BlockSpec((1, nch * L, SWA_WIDTH), lambda i, c: (i, c, 0)),
            pl.BlockSpec((1, ext, SWA_KV_WIDTH), lambda i, c: (i, 0, 0)),
            pl.BlockSpec((1, ext, SWA_KV_WIDTH), lambda i, c: (i, 0, 0)),
            pl.BlockSpec((1, LANES), lambda i, c: (0, 0)),
        ],
        out_specs=pl.BlockSpec((1, nch * L, SWA_WIDTH), lambda i, c: (i, c, 0)),
        out_shape=jax.ShapeDtypeStruct((b, t, SWA_WIDTH), BF16),
        compiler_params=pltpu.CompilerParams(
            dimension_semantics=("arbitrary", "arbitrary"), vmem_limit_bytes=VMEM_LIMIT),
        name="swa",
    )(q, k_ext, v_ext, sinks_pad)


def _layer_norm(r, g, b):
    mu = jnp.mean(r, -1, keepdims=True)
    d = r - mu
    var = jnp.mean(d * d, -1, keepdims=True)
    return d * lax.rsqrt(var + LN_EPS) * g + b


def _route(x1b, wrt_ref, rbias_ref, count_ref):
    tm = x1b.shape[0]
    scores = jax.nn.sigmoid(_dot_nt(wrt_ref[...], x1b))
    biased = scores + rbias_ref[...]
    neg = -jnp.inf
    r_g = lax.broadcasted_iota(jnp.int32, (GROUP_SIZE, tm), 0)
    blocks = []
    group_score = []
    for g in range(N_GROUPS):
        blk = biased[g * GROUP_SIZE:(g + 1) * GROUP_SIZE]
        m1 = jnp.max(blk, 0, keepdims=True)
        i1 = jnp.min(jnp.where(blk == m1, r_g, GROUP_SIZE), 0, keepdims=True)
        m2 = jnp.max(jnp.where(r_g == i1, neg, blk), 0, keepdims=True)
        blocks.append(blk)
        group_score.append(m1 + m2)
    masked = []
    for g in range(N_GROUPS):
        rank = jnp.zeros((1, tm), jnp.int32)
        for o in range(N_GROUPS):
            if o == g:
                continue
            ahead = group_score[o] > group_score[g]
            if o < g:
                ahead = ahead | (group_score[o] == group_score[g])
            rank = rank + ahead.astype(jnp.int32)
        masked.append(jnp.where(rank < TOPK_GROUPS, blocks[g], neg))
    masked = jnp.concatenate(masked, 0)
    r_e = lax.broadcasted_iota(jnp.int32, (N_EXPERTS, tm), 0)
    idx_rows = []
    w_rows = []
    hits = []
    for _ in range(TOP_K):
        m = jnp.max(masked, 0, keepdims=True)
        idx = jnp.min(jnp.where(masked == m, r_e, N_EXPERTS), 0, keepdims=True)
        hit = r_e == idx
        w_rows.append(jnp.sum(jnp.where(hit, scores, 0.0), 0, keepdims=True))
        masked = jnp.where(hit, neg, masked)
        idx_rows.append(idx)
        hits.append(hit)
    total = w_rows[0]
    for wr in w_rows[1:]:
        total = total + wr
    top_w = jnp.concatenate(w_rows, 0) / total * ROUTED_SCALE

    chosen = jnp.zeros((N_EXPERTS, tm), F32)
    for hit in hits:
        chosen = chosen + hit.astype(F32)
    t_r = lax.broadcasted_iota(jnp.int32, (tm, tm), 0)
    t_c = lax.broadcasted_iota(jnp.int32, (tm, tm), 1)
    before = (t_r < t_c).astype(BF16)
    prior = _dot(chosen.astype(BF16), before) + count_ref[...]
    rank_rows = [jnp.sum(jnp.where(hit, prior, 0.0), 0, keepdims=True) for hit in hits]
    count_ref[...] = count_ref[...] + jnp.sum(chosen, 1, keepdims=True)
    rank = jnp.concatenate(rank_rows, 0).astype(jnp.int32)
    return jnp.concatenate(idx_rows, 0), top_w, rank


def _mix_kernel(np_tiles, gp_ref, sp_ref, xp_ref, gs_ref, ss_ref, xs_ref, wo_ref, g1_ref, b1_ref,
                wrt_ref, rbias_ref, x1_ref, tope_ref, topw_ref, rank_ref, cnt_ref, count_acc):
    i = pl.program_id(0)

    @pl.when(i == 0)
    def _():
        count_acc[...] = jnp.zeros_like(count_acc)

    is_prompt = i < np_tiles
    gd = jnp.where(is_prompt, gp_ref[...], gs_ref[...])
    sw = jnp.where(is_prompt, sp_ref[...], ss_ref[...])
    x = jnp.where(is_prompt, xp_ref[...], xs_ref[...])
    mix = _dot(gd, wo_ref[0:GDN_WIDTH, :]) + _dot(sw, wo_ref[GDN_WIDTH:, :])
    x1 = _layer_norm(ALPHA * x + mix, g1_ref[...], b1_ref[...])
    _store_token_tiles(x1_ref, x1)
    top_e, top_w, rank = _route(x1.astype(BF16), wrt_ref, rbias_ref, count_acc)
    tope_ref[...] = top_e
    topw_ref[...] = top_w
    rank_ref[...] = rank
    cnt_ref[...] = count_acc[...].astype(jnp.int32)


def _mix(gd_p, sw_p, x_p, gd_s, sw_s, x_s, wo, g1, b1, wrt, rbias):
    n_p, n_s = x_p.shape[0], x_s.shape[0]
    np_tiles = n_p // MIX_TILE
    nt = n_p + n_s
    pm = lambda i: (jnp.minimum(i, np_tiles - 1), 0)
    sm = lambda i: (jnp.maximum(i - np_tiles, 0), 0)
    const = lambda i: (0, 0)
    col = lambda i: (0, i)
    return pl.pallas_call(
        functools.partial(_mix_kernel, np_tiles),
        grid=(nt // MIX_TILE,),
        in_specs=[
            pl.BlockSpec((MIX_TILE, GDN_WIDTH), pm),
            pl.BlockSpec((MIX_TILE, SWA_WIDTH), pm),
            pl.BlockSpec((MIX_TILE, D_MODEL), pm),
            pl.BlockSpec((MIX_TILE, GDN_WIDTH), sm),
            pl.BlockSpec((MIX_TILE, SWA_WIDTH), sm),
            pl.BlockSpec((MIX_TILE, D_MODEL), sm),
            pl.BlockSpec((D_MODEL, D_MODEL), const),
            pl.BlockSpec((1, D_MODEL), const),
            pl.BlockSpec((1, D_MODEL), const),
            pl.BlockSpec((N_EXPERTS, D_MODEL), const),
            pl.BlockSpec((N_EXPERTS, 1), const),
        ],
        out_specs=(
            pl.BlockSpec((MIX_TILE * ROW_TILES, LANES), lambda i: (i, 0)),
            pl.BlockSpec((TOP_K, MIX_TILE), col),
            pl.BlockSpec((TOP_K, MIX_TILE), col),
            pl.BlockSpec((TOP_K, MIX_TILE), col),
            pl.BlockSpec((N_EXPERTS, 1), const),
        ),
        out_shape=(
            jax.ShapeDtypeStruct((nt * ROW_TILES, LANES), F32),
            jax.ShapeDtypeStruct((TOP_K, nt), jnp.int32),
            jax.ShapeDtypeStruct((TOP_K, nt), F32),
            jax.ShapeDtypeStruct((TOP_K, nt), jnp.int32),
            jax.ShapeDtypeStruct((N_EXPERTS, 1), jnp.int32),
        ),
        scratch_shapes=[pltpu.VMEM((N_EXPERTS, 1), F32)],
        compiler_params=pltpu.CompilerParams(
            dimension_semantics=("arbitrary",), vmem_limit_bytes=VMEM_LIMIT),
        name="mix",
    )(gd_p, sw_p, x_p, gd_s, sw_s, x_s, wo, g1, b1, wrt, rbias)


def _dest_kernel(tope_ref, rank_ref, pstart_ref, dest_ref):
    tm = tope_ref.shape[1]
    r_e = lax.broadcasted_iota(jnp.int32, (N_EXPERTS, tm), 0)
    pstart = pstart_ref[...]
    top_e = tope_ref[...]
    rows = [jnp.sum(jnp.where(r_e == top_e[j:j + 1, :], pstart, 0), 0, keepdims=True)
            for j in range(TOP_K)]
    dest_ref[...] = (jnp.concatenate(rows, 0) + rank_ref[...]) * ROW_TILES


def _dest(top_e_t, rank_t, pstart):
    nt = top_e_t.shape[1]
    col = lambda i: (0, i)
    return pl.pallas_call(
        _dest_kernel,
        grid=(nt // DEST_TILE,),
        in_specs=[
            pl.BlockSpec((TOP_K, DEST_TILE), col),
            pl.BlockSpec((TOP_K, DEST_TILE), col),
            pl.BlockSpec((N_EXPERTS, 1), lambda i: (0, 0)),
        ],
        out_specs=pl.BlockSpec((TOP_K, DEST_TILE), col),
        out_shape=jax.ShapeDtypeStruct((TOP_K, nt), jnp.int32),
        compiler_params=pltpu.CompilerParams(dimension_semantics=("arbitrary",)),
        name="dest",
    )(top_e_t, rank_t, pstart)


def _experts_kernel(be_ref, nused_ref, c0_ref, wpar_ref, wnext_ref, stok_ref, x1_ref,
                    wg_ref, wu_ref, wd_ref, y_ref,
                    idx0, idx1, idx2, xbuf0, xbuf1, xbuf2, ybuf0, ybuf1, ybuf2, zbuf,
                    idx_sem, row_sem, out_sem, z_sem, wg_f, wu_f, wd_f, w_sem,
                    wg_b, wu_b, wd_b):
    i = pl.program_id(0)
    n_used = nused_ref[0]
    n_steps = pl.num_programs(0)
    idx = (idx0, idx1, idx2)
    xbuf = (xbuf0, xbuf1, xbuf2)
    ybuf = (ybuf0, ybuf1, ybuf2)
    block_rows = EXPERT_BLOCK * ROW_TILES

    def out_rows(block):
        return y_ref.at[pl.ds(pl.multiple_of(block * block_rows, block_rows), block_rows), :]

    def out_copy(block, s):
        return pltpu.make_async_copy(ybuf[s], out_rows(block), out_sem.at[s])

    def zero_copy(block):
        return pltpu.make_async_copy(zbuf, out_rows(block), z_sem)

    def weight_copies(e, s):
        return (pltpu.make_async_copy(wg_ref.at[e], wg_f.at[s], w_sem.at[s]),
                pltpu.make_async_copy(wu_ref.at[e], wu_f.at[s], w_sem.at[s]),
                pltpu.make_async_copy(wd_ref.at[e], wd_f.at[s], w_sem.at[s]))

    def idx_copy(block, s):
        base = pl.multiple_of(c0_ref[block] // LANES * LANES, LANES)
        return pltpu.make_async_copy(stok_ref.at[pl.ds(base, IDX_WINDOW)], idx[s], idx_sem.at[s])

    def start_rows(block, s):
        off = lax.rem(c0_ref[block], LANES)
        for r in range(EXPERT_BLOCK):
            src = pl.multiple_of(idx[s][off + r], ROW_TILES)
            pltpu.make_async_copy(x1_ref.at[pl.ds(src, ROW_TILES), :],
                                  xbuf[s].at[pl.ds(r * ROW_TILES, ROW_TILES), :],
                                  row_sem.at[s]).start(priority=r % 2)

    def wait_rows(s):
        pltpu.make_async_copy(x1_ref.at[pl.ds(0, EXPERT_BLOCK * ROW_TILES), :], xbuf[s],
                              row_sem.at[s]).wait()

    @pl.when(i == 0)
    def _():
        for cp in weight_copies(be_ref[0], 0):
            cp.start()
        idx_copy(0, 0).start()
        idx_copy(1, 1).start()
        idx_copy(0, 0).wait()
        idx_copy(1, 1).wait()
        start_rows(0, 0)
        start_rows(1, 1)
        idx_copy(2, 2).start()

    prev = be_ref[jnp.maximum(i - 1, 0)]

    @pl.when((i < n_used) & ((i == 0) | (be_ref[i] != prev)))
    def _():
        par = wpar_ref[i]
        for cp in weight_copies(be_ref[i], par):
            cp.wait()

        @pl.when(wnext_ref[i] >= 0)
        def _():
            for cp in weight_copies(wnext_ref[i], 1 - par):
                cp.start()

        wg_b[...] = wg_f[par].astype(BF16)
        wu_b[...] = wu_f[par].astype(BF16)
        wd_b[...] = wd_f[par].astype(BF16)

    for slot in range(3):
        ahead = (slot + 2) % 3

        @pl.when((i < n_used) & (i >= 3) & (lax.rem(i, 3) == slot))
        def _():
            out_copy(i - 3, slot).wait()

        @pl.when((i < n_used) & (lax.rem(i, 3) == slot))
        def _():
            idx_copy(i + 2, ahead).wait()
            wait_rows(slot)
            start_rows(i + 2, ahead)
            idx_copy(i + 3, slot).start()
            xb = _load_token_tiles(xbuf[slot], EXPERT_BLOCK).astype(BF16)
            hb = _silu(_dot(xb, wg_b[...])) * _dot(xb, wu_b[...])
            _store_token_tiles(ybuf[slot], _dot(hb.astype(BF16), wd_b[...]))
            out_copy(i, slot).start()

        @pl.when((i == n_used) & (lax.rem(i, 3) == slot))
        def _():
            wait_rows(slot)
            wait_rows((slot + 1) % 3)
            idx_copy(i + 2, ahead).wait()
            for back in range(1, 4):
                @pl.when(i >= back)
                def _():
                    out_copy(i - back, (slot - back) % 3).wait()
            zbuf[...] = jnp.zeros_like(zbuf)

    @pl.when(i >= n_used)
    def _():
        zero_copy(i).start()

    @pl.when(i == n_steps - 1)
    def _():
        def drain(_, carry):
            zero_copy(0).wait()
            return carry
        lax.fori_loop(0, n_steps - n_used, drain, 0)


def _experts(block_e, n_used, block_c0, w_par, w_next, stok, x1, w_gate_e, w_up_e, w_down_e):
    nb = block_e.shape[0]
    grid_spec = pltpu.PrefetchScalarGridSpec(
        num_scalar_prefetch=5,
        grid=(nb,),
        in_specs=[pl.BlockSpec(memory_space=pl.ANY)] * 5,
        out_specs=pl.BlockSpec(memory_space=pl.ANY),
        scratch_shapes=(
            [pltpu.SMEM((IDX_WINDOW,), jnp.int32)] * 3
            + [pltpu.VMEM((EXPERT_BLOCK * ROW_TILES, LANES), F32)] * 7
        ) + [
            pltpu.SemaphoreType.DMA((3,)),
            pltpu.SemaphoreType.DMA((3,)),
            pltpu.SemaphoreType.DMA((3,)),
            pltpu.SemaphoreType.DMA(()),
            pltpu.VMEM((2, D_MODEL, EXPERT_FF), F32),
            pltpu.VMEM((2, D_MODEL, EXPERT_FF), F32),
            pltpu.VMEM((2, EXPERT_FF, D_MODEL), F32),
            pltpu.SemaphoreType.DMA((2,)),
            pltpu.VMEM((D_MODEL, EXPERT_FF), BF16),
            pltpu.VMEM((D_MODEL, EXPERT_FF), BF16),
            pltpu.VMEM((EXPERT_FF, D_MODEL), BF16),
        ],
    )
    return pl.pallas_call(
        _experts_kernel,
        grid_spec=grid_spec,
        out_shape=jax.ShapeDtypeStruct((nb * EXPERT_BLOCK * ROW_TILES, LANES), F32),
        compiler_params=pltpu.CompilerParams(
            dimension_semantics=("arbitrary",), vmem_limit_bytes=VMEM_LIMIT),
        name="experts",
    )(block_e, n_used, block_c0, w_par, w_next, stok, x1, w_gate_e, w_up_e, w_down_e)


def _final_kernel(np_tiles, dest_ref, dest_next_ref, x1_ref, topw_ref, yb_ref, wsg_ref, wsu_ref,
                  wsd_ref, g2_ref, b2_ref, yp_ref, ys_ref, ybuf, sem):
    i = pl.program_id(0)
    slot = lax.rem(i, 2)

    def issue_rows(d_ref, s):
        def body(t, carry):
            dst = pl.multiple_of(t * ROW_TILES, ROW_TILES)
            for j in range(TOP_K):
                src = pl.multiple_of(d_ref[j, t], ROW_TILES)
                pltpu.make_async_copy(yb_ref.at[pl.ds(src, ROW_TILES), :],
                                      ybuf.at[s, j, pl.ds(dst, ROW_TILES), :],
                                      sem.at[s]).start(priority=j % 2)
            return carry
        lax.fori_loop(0, MIX_TILE, body, 0)

    @pl.when(i == 0)
    def _():
        issue_rows(dest_ref, 0)

    @pl.when(i + 1 < pl.num_programs(0))
    def _():
        issue_rows(dest_next_ref, 1 - slot)

    x1 = _load_token_tiles(x1_ref, MIX_TILE)
    xb = x1.astype(BF16)
    hs = _silu(_dot(xb, wsg_ref[...])) * _dot(xb, wsu_ref[...])
    ff = _dot(hs.astype(BF16), wsd_ref[...])
    r = lax.broadcasted_iota(jnp.int32, (MIX_TILE, MIX_TILE), 0)
    c = lax.broadcasted_iota(jnp.int32, (MIX_TILE, MIX_TILE), 1)
    w_cols = _dot_nt((r == c).astype(F32), topw_ref[...], HIGHEST)
    for j in range(TOP_K):
        pltpu.make_async_copy(yb_ref.at[pl.ds(0, MIX_TILE * ROW_TILES), :], ybuf.at[slot, j],
                              sem.at[slot]).wait()
    for j in range(TOP_K):
        ff = ff + w_cols[:, j:j + 1] * _load_token_tiles(ybuf.at[slot, j], MIX_TILE)
    y = _layer_norm(ALPHA * x1 + ff, g2_ref[...], b2_ref[...])

    @pl.when(i < np_tiles)
    def _():
        yp_ref[...] = y

    @pl.when(i >= np_tiles)
    def _():
        ys_ref[...] = y


def _final(n_p, dest, x1, top_w_t, yb, wsg, wsu, wsd, g2, b2):
    nt = x1.shape[0] // ROW_TILES
    np_tiles = n_p // MIX_TILE
    row = lambda i: (i, 0)
    const = lambda i: (0, 0)
    n_tiles = nt // MIX_TILE
    return pl.pallas_call(
        functools.partial(_final_kernel, np_tiles),
        grid=(n_tiles,),
        in_specs=[
            pl.BlockSpec((TOP_K, MIX_TILE), lambda i: (0, i), memory_space=pltpu.SMEM),
            pl.BlockSpec((TOP_K, MIX_TILE), lambda i: (0, jnp.minimum(i + 1, n_tiles - 1)),
                         memory_space=pltpu.SMEM),
            pl.BlockSpec((MIX_TILE * ROW_TILES, LANES), row),
            pl.BlockSpec((TOP_K, MIX_TILE), lambda i: (0, i)),
            pl.BlockSpec(memory_space=pl.ANY),
            pl.BlockSpec((D_MODEL, EXPERT_FF), const),
            pl.BlockSpec((D_MODEL, EXPERT_FF), const),
            pl.BlockSpec((EXPERT_FF, D_MODEL), const),
            pl.BlockSpec((1, D_MODEL), const),
            pl.BlockSpec((1, D_MODEL), const),
        ],
        out_specs=(
            pl.BlockSpec((MIX_TILE, D_MODEL), lambda i: (jnp.minimum(i, np_tiles - 1), 0)),
            pl.BlockSpec((MIX_TILE, D_MODEL), lambda i: (jnp.maximum(i - np_tiles, 0), 0)),
        ),
        out_shape=(
            jax.ShapeDtypeStruct((n_p, D_MODEL), F32),
            jax.ShapeDtypeStruct((nt - n_p, D_MODEL), F32),
        ),
        scratch_shapes=[
            pltpu.VMEM((2, TOP_K, MIX_TILE * ROW_TILES, LANES), F32),
            pltpu.SemaphoreType.DMA((2,)),
        ],
        compiler_params=pltpu.CompilerParams(
            dimension_semantics=("arbitrary",), vmem_limit_bytes=VMEM_LIMIT),
        name="final",
    )(dest, dest, x1, top_w_t, yb, wsg, wsu, wsd, g2, b2)


def _expert_layout(counts, nk):
    nb = -(-nk // EXPERT_BLOCK) + N_EXPERTS
    padded = (counts + EXPERT_BLOCK - 1) // EXPERT_BLOCK * EXPERT_BLOCK
    pend = jnp.cumsum(padded)
    pstart = (pend - padded).astype(jnp.int32)
    block_start = jnp.arange(nb, dtype=jnp.int32) * EXPERT_BLOCK
    block_e = jnp.sum((pend[None, :] <= block_start[:, None]).astype(jnp.int32), 1)
    block_e = jnp.minimum(block_e, N_EXPERTS - 1)
    n_used = (pend[-1] // EXPERT_BLOCK).astype(jnp.int32).reshape(1)
    start = (jnp.cumsum(counts) - counts).astype(jnp.int32)
    block_c0 = start[block_e] + block_start - pstart[block_e]
    block_c0 = jnp.where(block_start < pend[-1], block_c0, 0)
    block_c0 = jnp.concatenate([block_c0, jnp.zeros((2,), jnp.int32)]).astype(jnp.int32)
    ids = jnp.arange(nb, dtype=jnp.int32)
    used = block_start < pend[-1]
    switch = used & ((ids == 0) | (block_e != jnp.roll(block_e, 1)))
    w_par = ((jnp.cumsum(switch.astype(jnp.int32)) - 1) % 2).astype(jnp.int32)
    later_switch = (ids[None, :] > ids[:, None]) & switch[None, :]
    next_pos = jnp.min(jnp.where(later_switch, ids[None, :], nb), axis=1)
    w_next = jnp.where(next_pos < nb, block_e[jnp.minimum(next_pos, nb - 1)], -1).astype(jnp.int32)
    return pstart, block_e, n_used, block_c0, w_par, w_next


def _pad_lanes(v, offset=0):
    out = jnp.zeros((1, LANES), F32)
    return out.at[0, offset:offset + v.shape[0]].set(v.astype(F32))


def kernel(x_prompt, x_sample, state_conv, state_gdn, cache_swa_k, cache_swa_v, w_in, conv_w, a_log, dt_bias, gdn_norm_w, attn_sinks, w_o, ln1_g, ln1_b, w_router, router_bias, w_gate_e, w_up_e, w_down_e, w_shared_gate, w_shared_up, w_shared_down, ln2_g, ln2_b):
    bp, tp, d = x_prompt.shape
    bs, ts, _ = x_sample.shape
    n_p, n_s = bp * tp, bs * ts

    wi = w_in[0]
    o1 = CONV_DIM
    o2 = o1 + GDN_WIDTH
    o4 = o2 + 2 * GDN_HEADS
    o5 = o4 + SWA_WIDTH
    o6 = o5 + SWA_KV_WIDTH
    ba_cols = jnp.zeros((d, LANES), F32).at[:, :2 * GDN_HEADS].set(wi[:, o2:o4])
    w_all = jnp.concatenate([wi[:, :o2], wi[:, o4:o5], wi[:, o5:o6], wi[:, o6:], ba_cols], 1).astype(BF16)
    wo = w_o[0].astype(BF16)
    wrt = w_router[0].T.astype(BF16)
    rbias = router_bias[0].astype(F32).reshape(N_EXPERTS, 1)
    alog_pad = _pad_lanes(a_log[0], GDN_HEADS)
    dtb_pad = _pad_lanes(dt_bias[0], GDN_HEADS)
    sinks_pad = _pad_lanes(attn_sinks[0])
    norm_w = gdn_norm_w[0].reshape(1, GDN_D)
    g1, b1 = ln1_g[0].reshape(1, d), ln1_b[0].reshape(1, d)
    g2, b2 = ln2_g[0].reshape(1, d), ln2_b[0].reshape(1, d)

    def front(x, pos0, L, conv_hist, s0, k_hist, v_hist):
        b, t, _ = x.shape
        period = max(t, IN_TILE)
        pos = pos0 + (jnp.arange(period, dtype=jnp.int32) % t).astype(F32)
        cos, s1, s2 = _rope_tables(pos)
        x2d = x.reshape(b * t, d)
        qkv, z, sq, sk, sv, ba = _in_proj(x2d, w_all, cos, s1, s2)
        qkv = qkv.reshape(b, t, CONV_DIM)
        hist8 = jnp.concatenate(
            [jnp.zeros((b, HIST_ROWS - (CONV_WIDTH - 1), CONV_DIM), F32), conv_hist], 1)
        gd, s_new = _gdn(L, qkv, z.reshape(b, t, GDN_WIDTH), ba.reshape(b, t, LANES), hist8, s0,
                         conv_w[0], alog_pad, dtb_pad, norm_w)
        k_ext = jnp.concatenate([k_hist.reshape(b, WINDOW, SWA_KV_WIDTH),
                                 sk.reshape(b, t, SWA_KV_WIDTH)], 1)
        v_ext = jnp.concatenate([v_hist.reshape(b, WINDOW, SWA_KV_WIDTH),
                                 sv.reshape(b, t, SWA_KV_WIDTH)], 1)
        sw = _swa(L, pos0, sq.reshape(b, t, SWA_WIDTH), k_ext, v_ext, sinks_pad)
        conv_new = jnp.concatenate([conv_hist, qkv], 1)[:, -(CONV_WIDTH - 1):]
        k_new = k_ext[:, -WINDOW:].reshape(b, WINDOW, SWA_KV_HEADS, SWA_HEAD_DIM)
        v_new = v_ext[:, -WINDOW:].reshape(b, WINDOW, SWA_KV_HEADS, SWA_HEAD_DIM)
        return (x2d, gd.reshape(b * t, GDN_WIDTH), sw.reshape(b * t, SWA_WIDTH),
                conv_new, s_new, k_new, v_new)

    zeros = lambda *s: jnp.zeros(s, F32)
    xp2, gd_p, sw_p, conv_p, gdn_p, k_p, v_p = front(
        x_prompt, 0, CHUNK, zeros(bp, CONV_WIDTH - 1, CONV_DIM),
        zeros(bp, GDN_HEADS, GDN_D, GDN_D), zeros(bp, WINDOW, SWA_KV_WIDTH),
        zeros(bp, WINDOW, SWA_KV_WIDTH))
    xs2, gd_s, sw_s, conv_s, gdn_s, k_s, v_s = front(
        x_sample, PAST_LEN, ts, state_conv[0], state_gdn[0], cache_swa_k[0], cache_swa_v[0])

    x1, top_e_t, top_w_t, rank_t, counts = _mix(
        gd_p, sw_p, xp2, gd_s, sw_s, xs2, wo, g1, b1, wrt, rbias)

    pstart, block_e, n_used, block_c0, w_par, w_next = _expert_layout(
        counts[:, 0], TOP_K * (n_p + n_s))
    dest = _dest(top_e_t, rank_t, pstart.reshape(N_EXPERTS, 1))
    order = jnp.argsort(top_e_t.T.reshape(-1))
    stok = jnp.concatenate([(order // TOP_K * ROW_TILES).astype(jnp.int32),
                            jnp.zeros((IDX_WINDOW,), jnp.int32)])
    yb = _experts(block_e, n_used, block_c0, w_par, w_next, stok, x1,
                  w_gate_e[0], w_up_e[0], w_down_e[0])

    y_p, y_s = _final(n_p, dest, x1, top_w_t, yb,
                      w_shared_gate[0].astype(BF16), w_shared_up[0].astype(BF16),
                      w_shared_down[0].astype(BF16), g2, b2)
    return (y_p.reshape(bp, tp, d), y_s.reshape(bs, ts, d),
            conv_p[None], gdn_p[None], k_p[None], v_p[None],
            conv_s[None], gdn_s[None], k_s[None], v_s[None])
```

```python
import functools
import math

import jax
import jax.numpy as jnp
from jax import lax
from jax.experimental import pallas as pl
from jax.experimental.pallas import tpu as pltpu

F32 = jnp.float32
BF16 = jnp.bfloat16
HIGHEST = lax.Precision.HIGHEST

D_MODEL = 1024
CHUNK = 64
GDN_HEADS = 4
GDN_D = 128
GDN_WIDTH = GDN_HEADS * GDN_D
CONV_DIM = 3 * GDN_WIDTH
CONV_WIDTH = 4
SWA_HEAD_DIM = 64
SWA_Q_HEADS = 8
SWA_KV_HEADS = 2
SWA_GROUP = SWA_Q_HEADS // SWA_KV_HEADS
SWA_WIDTH = SWA_Q_HEADS * SWA_HEAD_DIM
SWA_KV_WIDTH = SWA_KV_HEADS * SWA_HEAD_DIM
WINDOW = 128
ROPE_THETA = 500000.0
ROT_DIM = SWA_HEAD_DIM // 4
ROT_HALF = ROT_DIM // 2
N_EXPERTS = 256
TOP_K = 8
N_GROUPS = 8
GROUP_SIZE = N_EXPERTS // N_GROUPS
TOPK_GROUPS = 4
EXPERT_FF = 256
ROUTED_SCALE = 2.5
ALPHA = 2.0 ** 0.25
LN_EPS = 1e-5
NORM_EPS = 1e-6
PAST_LEN = 4096

LANES = 128
HIST_ROWS = 8
ROW_TILES = D_MODEL // LANES

C_QKV = 0
C_Z = C_QKV + CONV_DIM
C_SQ = C_Z + GDN_WIDTH
C_SK = C_SQ + SWA_WIDTH
C_SV = C_SK + SWA_KV_WIDTH
C_BA = C_SV + SWA_KV_WIDTH
C_END = C_BA + LANES

IN_TILE = 512
MIX_TILE = 256
DEST_TILE = 512
EXPERT_BLOCK = 256
IDX_WINDOW = EXPERT_BLOCK + LANES
GDN_BATCH = 8
SWA_CHUNKS = 8
VMEM_LIMIT = 56 * 1024 * 1024


def _dot(a, b, precision=None):
    return jnp.dot(a, b, preferred_element_type=F32, precision=precision)


def _dot_nt(a, b, precision=None):
    return lax.dot_general(a, b, (((1,), (1,)), ((), ())),
                           preferred_element_type=F32, precision=precision)


def _dot_tn(a, b):
    return lax.dot_general(a, b, (((0,), (0,)), ((), ())), preferred_element_type=F32)


def _split(x):
    hi = x.astype(BF16)
    return hi, (x - hi.astype(F32)).astype(BF16)


def _dot_split(a, b):
    return _dot(a[0], b[0]) + (_dot(a[1], b[0]) + _dot(a[0], b[1]))


def _silu(x):
    return x * jax.nn.sigmoid(x)


def _store_token_tiles(ref, x):
    m = x.shape[0]
    for s in range(ROW_TILES):
        ref[pl.ds(s, m, stride=ROW_TILES), :] = x[:, s * LANES:(s + 1) * LANES]


def _load_token_tiles(ref, m):
    return jnp.concatenate([ref[pl.ds(s, m, stride=ROW_TILES), :] for s in range(ROW_TILES)], 1)


def _in_proj_kernel(x_ref, w_ref, cos_ref, s1_ref, s2_ref,
                    qkv_ref, z_ref, sq_ref, sk_ref, sv_ref, ba_ref):
    xb = x_ref[...].astype(BF16)
    qkv_ref[...] = _dot(xb, w_ref[:, C_QKV:C_Z])
    z_ref[...] = _dot(xb, w_ref[:, C_Z:C_SQ])
    sv_ref[...] = _dot(xb, w_ref[:, C_SV:C_BA])
    ba_ref[...] = _dot(xb, w_ref[:, C_BA:C_END])
    cos = cos_ref[...]
    s1 = s1_ref[...]
    s2 = s2_ref[...]

    def rope(t):
        return (t * cos + pltpu.roll(t, LANES - ROT_HALF, 1) * s1
                + pltpu.roll(t, ROT_HALF, 1) * s2)

    sk_ref[...] = rope(_dot(xb, w_ref[:, C_SK:C_SV]))
    for g in range(SWA_WIDTH // LANES):
        lo = C_SQ + g * LANES
        sq_ref[:, g * LANES:(g + 1) * LANES] = rope(_dot(xb, w_ref[:, lo:lo + LANES])).astype(BF16)


def _in_proj(x2d, w_all, cos, s1, s2):
    n = x2d.shape[0]
    period_tiles = cos.shape[0] // IN_TILE
    row = lambda i: (i, 0)
    tab = lambda i: (i % period_tiles, 0)
    out_shapes = (
        jax.ShapeDtypeStruct((n, CONV_DIM), F32),
        jax.ShapeDtypeStruct((n, GDN_WIDTH), F32),
        jax.ShapeDtypeStruct((n, SWA_WIDTH), BF16),
        jax.ShapeDtypeStruct((n, SWA_KV_WIDTH), F32),
        jax.ShapeDtypeStruct((n, SWA_KV_WIDTH), F32),
        jax.ShapeDtypeStruct((n, LANES), F32),
    )
    return pl.pallas_call(
        _in_proj_kernel,
        grid=(n // IN_TILE,),
        in_specs=[
            pl.BlockSpec((IN_TILE, D_MODEL), row),
            pl.BlockSpec((D_MODEL, C_END), lambda i: (0, 0)),
            pl.BlockSpec((IN_TILE, LANES), tab),
            pl.BlockSpec((IN_TILE, LANES), tab),
            pl.BlockSpec((IN_TILE, LANES), tab),
        ],
        out_specs=(
            pl.BlockSpec((IN_TILE, CONV_DIM), row),
            pl.BlockSpec((IN_TILE, GDN_WIDTH), row),
            pl.BlockSpec((IN_TILE, SWA_WIDTH), row),
            pl.BlockSpec((IN_TILE, SWA_KV_WIDTH), row),
            pl.BlockSpec((IN_TILE, SWA_KV_WIDTH), row),
            pl.BlockSpec((IN_TILE, LANES), row),
        ),
        out_shape=out_shapes,
        compiler_params=pltpu.CompilerParams(
            dimension_semantics=("arbitrary",), vmem_limit_bytes=VMEM_LIMIT),
        name="in_proj",
    )(x2d, w_all, cos, s1, s2)


def _rope_tables(pos):
    p = pos.shape[0]
    inv = ROPE_THETA ** (-jnp.arange(0, ROT_DIM, 2, dtype=F32) / ROT_DIM)
    ang = pos[:, None] * inv[None, :]
    cos = jnp.cos(ang)
    sin = jnp.sin(ang)
    rest = SWA_HEAD_DIM - ROT_DIM
    head_c = jnp.concatenate([cos, cos, jnp.ones((p, rest), F32)], 1)
    head_s1 = jnp.concatenate([-sin, jnp.zeros((p, SWA_HEAD_DIM - ROT_HALF), F32)], 1)
    head_s2 = jnp.concatenate([jnp.zeros((p, ROT_HALF), F32), sin, jnp.zeros((p, rest), F32)], 1)
    two = lambda t: jnp.concatenate([t, t], 1)
    return two(head_c), two(head_s1), two(head_s2)


def _gdn_kernel(L, qkv_ref, z_ref, ba_ref, hist_ref, s0_ref, convw_ref, alog_ref, dtb_ref,
                normw_ref, o_ref, sout_ref, xbuf, state):
    c = pl.program_id(1)
    nbat = qkv_ref.shape[0]
    chains = [(b, h) for b in range(nbat) for h in range(GDN_HEADS)]

    @pl.when(c == 0)
    def _():
        xbuf[:, 0:HIST_ROWS, :] = hist_ref[...]
        state[...] = s0_ref[...]

    xbuf[:, HIST_ROWS:HIST_ROWS + L, :] = qkv_ref[...]

    row = lax.broadcasted_iota(jnp.int32, (L, L), 0)
    col = lax.broadcasted_iota(jnp.int32, (L, L), 1)
    incl = row >= col
    strict = row > col
    lower_ones = incl.astype(BF16)
    eye = (row == col).astype(F32)

    beta_all, g_all = [], []
    for b in range(nbat):
        ba = ba_ref[b]
        beta_all.append(jax.nn.sigmoid(ba))
        sp_in = ba + dtb_ref[...]
        softplus = jnp.maximum(sp_in, 0.0) + jnp.log1p(jnp.exp(-jnp.abs(sp_in)))
        g_all.append(-jnp.exp(alog_ref[...]) * softplus)

    def conv_silu(b, c0):
        acc = None
        for j in range(CONV_WIDTH):
            lo = HIST_ROWS - (CONV_WIDTH - 1) + j
            t = xbuf[b, lo:lo + L, c0:c0 + GDN_D] * convw_ref[j:j + 1, c0:c0 + GDN_D]
            acc = t if acc is None else acc + t
        return _silu(acc)

    def l2n(t):
        return t * lax.rsqrt(jnp.sum(t * t, -1, keepdims=True) + NORM_EPS)

    q = [l2n(conv_silu(b, h * GDN_D)) * (GDN_D ** -0.5) for b, h in chains]
    k = [l2n(conv_silu(b, GDN_WIDTH + h * GDN_D)) for b, h in chains]
    v = [conv_silu(b, 2 * GDN_WIDTH + h * GDN_D) for b, h in chains]
    beta = [beta_all[b][:, h:h + 1] for b, h in chains]
    g = [g_all[b][:, GDN_HEADS + h:GDN_HEADS + h + 1] for b, h in chains]

    def pieces(t):
        p1 = t.astype(BF16).astype(F32)
        r1 = t - p1
        p2 = r1.astype(BF16).astype(F32)
        return p1, p2, r1 - p2

    g_parts = [pieces(t) for t in g]
    diff = [sum(_dot(lower_ones, jnp.where(strict, p, 0.0).astype(BF16)) for p in gp)
            for gp in g_parts]
    gcum3 = [_dot(lower_ones, jnp.concatenate(
        [jnp.broadcast_to(p, (L, GDN_D)) for p in gp], 1).astype(BF16)) for gp in g_parts]
    gcum = [t[:, :GDN_D] + t[:, GDN_D:2 * GDN_D] + t[:, 2 * GDN_D:] for t in gcum3]
    dec = [jnp.where(incl, jnp.exp(jnp.where(incl, d, 0.0)), 0.0) for d in diff]
    exp_g = [jnp.exp(t) for t in gcum]
    g_end = [t[L - 1:L, :] for t in gcum]
    kb = [t.astype(BF16) for t in k]
    kk = [_dot_nt(t, t) for t in kb]
    aqk = [_dot_nt(qq.astype(BF16), t) * d for qq, t, d in zip(q, kb, dec)]
    a = [jnp.where(strict, d * t, 0.0) * bt for d, t, bt in zip(dec, kk, beta)]

    n = [-t for t in a]
    n_s = [_split(t) for t in n]
    z = [jnp.concatenate([eye + t, _dot_split(ts, ts)], 1) for t, ts in zip(n, n_s)]
    keep_s = lax.broadcasted_iota(jnp.int32, (L, 2 * L), 1) < L
    for _ in range(int(math.log2(L)) - 2):
        pz = [_dot_split(_split(t[:, L:]), _split(t)) for t in z]
        z = [jnp.where(keep_s, t, 0.0) + d for t, d in zip(z, pz)]
    inv = [t[:, :L] + _dot_split(_split(t[:, L:]), _split(t[:, :L])) for t in z]
    inv_s = [_split(t) for t in inv]
    wy = [_dot_split(t, _split(jnp.concatenate([bt * vv, (bt * eg) * kk_], 1)))
          for t, bt, vv, eg, kk_ in zip(inv_s, beta, v, exp_g, k)]
    w = [t[:, :GDN_D] for t in wy]
    y = [t[:, GDN_D:] for t in wy]

    s = [state[b, h] for b, h in chains]
    sb = [t.astype(BF16) for t in s]
    yq = [_dot(jnp.concatenate([yy.astype(BF16), (eg * qq).astype(BF16)], 0), t)
          for yy, eg, qq, t in zip(y, exp_g, q, sb)]
    u = [ww - t[:L] for ww, t in zip(w, yq)]
    ub = [t.astype(BF16) for t in u]
    o = [t[L:] + _dot(aa.astype(BF16), uu) for t, aa, uu in zip(yq, aqk, ub)]
    k_dec = [(jnp.exp(ge - gc) * kk_).astype(BF16) for ge, gc, kk_ in zip(g_end, gcum, k)]
    s_new = [jnp.exp(ge) * t + _dot_tn(kd, uu) for ge, t, kd, uu in zip(g_end, s, k_dec, ub)]

    for (b, h), t, oo in zip(chains, s_new, o):
        state[b, h] = t
        on = oo * lax.rsqrt(jnp.mean(oo * oo, -1, keepdims=True) + NORM_EPS)
        zz = z_ref[b, :, h * GDN_D:(h + 1) * GDN_D]
        o_ref[b, :, h * GDN_D:(h + 1) * GDN_D] = (on * normw_ref[...] * _silu(zz)).astype(BF16)

    xbuf[:, 0:HIST_ROWS, :] = xbuf[:, L:L + HIST_ROWS, :]

    @pl.when(c == pl.num_programs(1) - 1)
    def _():
        sout_ref[...] = state[...]


def _gdn(L, qkv, z, ba, hist8, s0, conv_w, alog_pad, dtb_pad, norm_w):
    b, t, _ = qkv.shape
    nc = t // L
    nbat = GDN_BATCH
    tok = lambda i, c: (i, c, 0)
    per_b3 = lambda i, c: (i, 0, 0)
    per_b4 = lambda i, c: (i, 0, 0, 0)
    const2 = lambda i, c: (0, 0)
    return pl.pallas_call(
        functools.partial(_gdn_kernel, L),
        grid=(b // nbat, nc),
        in_specs=[
            pl.BlockSpec((nbat, L, CONV_DIM), tok),
            pl.BlockSpec((nbat, L, GDN_WIDTH), tok),
            pl.BlockSpec((nbat, L, LANES), tok),
            pl.BlockSpec((nbat, HIST_ROWS, CONV_DIM), per_b3),
            pl.BlockSpec((nbat, GDN_HEADS, GDN_D, GDN_D), per_b4),
            pl.BlockSpec((CONV_WIDTH, CONV_DIM), const2),
            pl.BlockSpec((1, LANES), const2),
            pl.BlockSpec((1, LANES), const2),
            pl.BlockSpec((1, GDN_D), const2),
        ],
        out_specs=(
            pl.BlockSpec((nbat, L, GDN_WIDTH), tok),
            pl.BlockSpec((nbat, GDN_HEADS, GDN_D, GDN_D), per_b4),
        ),
        out_shape=(
            jax.ShapeDtypeStruct((b, t, GDN_WIDTH), BF16),
            jax.ShapeDtypeStruct((b, GDN_HEADS, GDN_D, GDN_D), F32),
        ),
        scratch_shapes=[
            pltpu.VMEM((nbat, HIST_ROWS + L, CONV_DIM), F32),
            pltpu.VMEM((nbat, GDN_HEADS, GDN_D, GDN_D), F32),
        ],
        compiler_params=pltpu.CompilerParams(
            dimension_semantics=("arbitrary", "arbitrary"), vmem_limit_bytes=VMEM_LIMIT),
        name="gdn",
    )(qkv, z, ba, hist8, s0, conv_w, alog_pad, dtb_pad, norm_w)


def _swa_kernel(L, pos0, q_ref, k_ref, v_ref, sink_ref, o_ref):
    c = pl.program_id(1)
    nch = q_ref.shape[1] // L
    span = WINDOW + L
    rows = WINDOW + nch * L
    start = pl.multiple_of(c * (nch * L), L)
    kx = k_ref[0, pl.ds(start, rows), :]
    vx = v_ref[0, pl.ds(start, rows), :]
    kx_sw = pltpu.roll(kx, SWA_HEAD_DIM, 1)
    vx_sw = pltpu.roll(vx, SWA_HEAD_DIM, 1)
    low = lax.broadcasted_iota(jnp.int32, (rows, LANES), 1) < SWA_HEAD_DIM

    def halves(x, x_sw, kh):
        src_lo, src_hi = (x, x_sw) if kh == 0 else (x_sw, x)
        return (jnp.where(low, src_lo, 0.0).astype(BF16), jnp.where(low, 0.0, src_hi).astype(BF16))

    k_halves = [halves(kx, kx_sw, kh) for kh in range(SWA_KV_HEADS)]
    v_halves = [halves(vx, vx_sw, kh) for kh in range(SWA_KV_HEADS)]
    sinks = sink_ref[...]
    top_rows = lax.broadcasted_iota(jnp.int32, (2 * L, 1), 0) < L
    first_pos = pos0 - WINDOW + c * (nch * L)

    chains = [(u, kh, half) for u in range(nch) for kh in range(SWA_KV_HEADS) for half in range(2)]
    valid = [first_pos + u * L + lax.broadcasted_iota(jnp.int32, (1, span), 1) >= 0
             for u in range(nch)]
    q4 = {}
    for u in range(nch):
        for kh in range(SWA_KV_HEADS):
            c0 = kh * 2 * LANES
            q4[u, kh] = jnp.concatenate([q_ref[0, u * L:(u + 1) * L, c0:c0 + LANES],
                                         q_ref[0, u * L:(u + 1) * L, c0 + LANES:c0 + 2 * LANES]], 0)
    scores = [jnp.where(valid[u],
                        _dot_nt(q4[u, kh], k_halves[kh][half][u * L:u * L + span])
                        * (SWA_HEAD_DIM ** -0.5), -jnp.inf)
              for u, kh, half in chains]
    sink_cols = [jnp.where(top_rows, sinks[:, kh * SWA_GROUP + half:kh * SWA_GROUP + half + 1],
                           sinks[:, kh * SWA_GROUP + half + 2:kh * SWA_GROUP + half + 3])
                 for _, kh, half in chains]
    m = [jnp.maximum(jnp.max(s, -1, keepdims=True), sk) for s, sk in zip(scores, sink_cols)]
    p = [jnp.exp(s - mm) for s, mm in zip(scores, m)]
    den = [jnp.sum(pp, -1, keepdims=True) + jnp.exp(sk - mm) for pp, sk, mm in zip(p, sink_cols, m)]
    pv = [_dot(pp.astype(BF16), v_halves[kh][half][u * L:u * L + span]) / dd
          for pp, dd, (u, kh, half) in zip(p, den, chains)]
    for n in range(0, len(chains), 2):
        u, kh, _ = chains[n]
        o = pv[n] + pv[n + 1]
        c0 = kh * 2 * LANES
        o_ref[0, u * L:(u + 1) * L, c0:c0 + LANES] = o[0:L].astype(BF16)
        o_ref[0, u * L:(u + 1) * L, c0 + LANES:c0 + 2 * LANES] = o[L:2 * L].astype(BF16)


def _swa(L, pos0, q, k_ext, v_ext, sinks_pad):
    b, t, _ = q.shape
    ext = k_ext.shape[1]
    nch = SWA_CHUNKS if (t // L) % SWA_CHUNKS == 0 else 1
    return pl.pallas_call(
        functools.partial(_swa_kernel, L, pos0),
        grid=(b, t // (nch * L)),
        in_specs=[
            pl.BlockSpec((1, nch * L, SWA_WIDTH), lambda i, c: (i, c, 0)),
            pl.BlockSpec((1, ext, SWA_KV_WIDTH), lambda i, c: (i, 0, 0)),
            pl.BlockSpec((1, ext, SWA_KV_WIDTH), lambda i, c: (i, 0, 0)),
            pl.BlockSpec((1, LANES), lambda i, c: (0, 0)),
        ],
        out_specs=pl.BlockSpec((1, nch * L, SWA_WIDTH), lambda i, c: (i, c, 0)),
        out_shape=jax.ShapeDtypeStruct((b, t, SWA_WIDTH), BF16),
        compiler_params=pltpu.CompilerParams(
            dimension_semantics=("arbitrary", "arbitrary"), vmem_limit_bytes=VMEM_LIMIT),
        name="swa",
    )(q, k_ext, v_ext, sinks_pad)


def _layer_norm(r, g, b):
    mu = jnp.mean(r, -1, keepdims=True)
    d = r - mu
    var = jnp.mean(d * d, -1, keepdims=True)
    return d * lax.rsqrt(var + LN_EPS) * g + b


def _route(x1b, wrt_ref, rbias_ref, count_ref):
    tm = x1b.shape[0]
    scores = jax.nn.sigmoid(_dot_nt(wrt_ref[...], x1b))
    biased = scores + rbias_ref[...]
    neg = -jnp.inf
    r_g = lax.broadcasted_iota(jnp.int32, (GROUP_SIZE, tm), 0)
    blocks = []
    group_score = []
    for g in range(N_GROUPS):
        blk = biased[g * GROUP_SIZE:(g + 1) * GROUP_SIZE]
        m1 = jnp.max(blk, 0, keepdims=True)
        i1 = jnp.min(jnp.where(blk == m1, r_g, GROUP_SIZE), 0, keepdims=True)
        m2 = jnp.max(jnp.where(r_g == i1, neg, blk), 0, keepdims=True)
        blocks.append(blk)
        group_score.append(m1 + m2)
    masked = []
    for g in range(N_GROUPS):
        rank = jnp.zeros((1, tm), jnp.int32)
        for o in range(N_GROUPS):
            if o == g:
                continue
            ahead = group_score[o] > group_score[g]
            if o < g:
                ahead = ahead | (group_score[o] == group_score[g])
            rank = rank + ahead.astype(jnp.int32)
        masked.append(jnp.where(rank < TOPK_GROUPS, blocks[g], neg))
    masked = jnp.concatenate(masked, 0)
    r_e = lax.broadcasted_iota(jnp.int32, (N_EXPERTS, tm), 0)
    idx_rows = []
    w_rows = []
    hits = []
    for _ in range(TOP_K):
        m = jnp.max(masked, 0, keepdims=True)
        idx = jnp.min(jnp.where(masked == m, r_e, N_EXPERTS), 0, keepdims=True)
        hit = r_e == idx
        w_rows.append(jnp.sum(jnp.where(hit, scores, 0.0), 0, keepdims=True))
        masked = jnp.where(hit, neg, masked)
        idx_rows.append(idx)
        hits.append(hit)
    total = w_rows[0]
    for wr in w_rows[1:]:
        total = total + wr
    top_w = jnp.concatenate(w_rows, 0) / total * ROUTED_SCALE

    chosen = jnp.zeros((N_EXPERTS, tm), F32)
    for hit in hits:
        chosen = chosen + hit.astype(F32)
    t_r = lax.broadcasted_iota(jnp.int32, (tm, tm), 0)
    t_c = lax.broadcasted_iota(jnp.int32, (tm, tm), 1)
    before = (t_r < t_c).astype(BF16)
    prior = _dot(chosen.astype(BF16), before) + count_ref[...]
    rank_rows = [jnp.sum(jnp.where(hit, prior, 0.0), 0, keepdims=True) for hit in hits]
    count_ref[...] = count_ref[...] + jnp.sum(chosen, 1, keepdims=True)
    rank = jnp.concatenate(rank_rows, 0).astype(jnp.int32)
    return jnp.concatenate(idx_rows, 0), top_w, rank


def _mix_kernel(np_tiles, gp_ref, sp_ref, xp_ref, gs_ref, ss_ref, xs_ref, wo_ref, g1_ref, b1_ref,
                wrt_ref, rbias_ref, x1_ref, tope_ref, topw_ref, rank_ref, cnt_ref, count_acc):
    i = pl.program_id(0)

    @pl.when(i == 0)
    def _():
        count_acc[...] = jnp.zeros_like(count_acc)

    is_prompt = i < np_tiles
    gd = jnp.where(is_prompt, gp_ref[...], gs_ref[...])
    sw = jnp.where(is_prompt, sp_ref[...], ss_ref[...])
    x = jnp.where(is_prompt, xp_ref[...], xs_ref[...])
    mix = _dot(gd, wo_ref[0:GDN_WIDTH, :]) + _dot(sw, wo_ref[GDN_WIDTH:, :])
    x1 = _layer_norm(ALPHA * x + mix, g1_ref[...], b1_ref[...])
    _store_token_tiles(x1_ref, x1)
    top_e, top_w, rank = _route(x1.astype(BF16), wrt_ref, rbias_ref, count_acc)
    tope_ref[...] = top_e
    topw_ref[...] = top_w
    rank_ref[...] = rank
    cnt_ref[...] = count_acc[...].astype(jnp.int32)


def _mix(gd_p, sw_p, x_p, gd_s, sw_s, x_s, wo, g1, b1, wrt, rbias):
    n_p, n_s = x_p.shape[0], x_s.shape[0]
    np_tiles = n_p // MIX_TILE
    nt = n_p + n_s
    pm = lambda i: (jnp.minimum(i, np_tiles - 1), 0)
    sm = lambda i: (jnp.maximum(i - np_tiles, 0), 0)
    const = lambda i: (0, 0)
    col = lambda i: (0, i)
    return pl.pallas_call(
        functools.partial(_mix_kernel, np_tiles),
        grid=(nt // MIX_TILE,),
        in_specs=[
            pl.BlockSpec((MIX_TILE, GDN_WIDTH), pm),
            pl.BlockSpec((MIX_TILE, SWA_WIDTH), pm),
            pl.BlockSpec((MIX_TILE, D_MODEL), pm),
            pl.BlockSpec((MIX_TILE, GDN_WIDTH), sm),
            pl.BlockSpec((MIX_TILE, SWA_WIDTH), sm),
            pl.BlockSpec((MIX_TILE, D_MODEL), sm),
            pl.BlockSpec((D_MODEL, D_MODEL), const),
            pl.BlockSpec((1, D_MODEL), const),
            pl.BlockSpec((1, D_MODEL), const),
            pl.BlockSpec((N_EXPERTS, D_MODEL), const),
            pl.BlockSpec((N_EXPERTS, 1), const),
        ],
        out_specs=(
            pl.BlockSpec((MIX_TILE * ROW_TILES, LANES), lambda i: (i, 0)),
            pl.BlockSpec((TOP_K, MIX_TILE), col),
            pl.BlockSpec((TOP_K, MIX_TILE), col),
            pl.BlockSpec((TOP_K, MIX_TILE), col),
            pl.BlockSpec((N_EXPERTS, 1), const),
        ),
        out_shape=(
            jax.ShapeDtypeStruct((nt * ROW_TILES, LANES), F32),
            jax.ShapeDtypeStruct((TOP_K, nt), jnp.int32),
            jax.ShapeDtypeStruct((TOP_K, nt), F32),
            jax.ShapeDtypeStruct((TOP_K, nt), jnp.int32),
            jax.ShapeDtypeStruct((N_EXPERTS, 1), jnp.int32),
        ),
        scratch_shapes=[pltpu.VMEM((N_EXPERTS, 1), F32)],
        compiler_params=pltpu.CompilerParams(
            dimension_semantics=("arbitrary",), vmem_limit_bytes=VMEM_LIMIT),
        name="mix",
    )(gd_p, sw_p, x_p, gd_s, sw_s, x_s, wo, g1, b1, wrt, rbias)


def _dest_kernel(tope_ref, rank_ref, pstart_ref, dest_ref):
    tm = tope_ref.shape[1]
    r_e = lax.broadcasted_iota(jnp.int32, (N_EXPERTS, tm), 0)
    pstart = pstart_ref[...]
    top_e = tope_ref[...]
    rows = [jnp.sum(jnp.where(r_e == top_e[j:j + 1, :], pstart, 0), 0, keepdims=True)
            for j in range(TOP_K)]
    dest_ref[...] = (jnp.concatenate(rows, 0) + rank_ref[...]) * ROW_TILES


def _dest(top_e_t, rank_t, pstart):
    nt = top_e_t.shape[1]
    col = lambda i: (0, i)
    return pl.pallas_call(
        _dest_kernel,
        grid=(nt // DEST_TILE,),
        in_specs=[
            pl.BlockSpec((TOP_K, DEST_TILE), col),
            pl.BlockSpec((TOP_K, DEST_TILE), col),
            pl.BlockSpec((N_EXPERTS, 1), lambda i: (0, 0)),
        ],
        out_specs=pl.BlockSpec((TOP_K, DEST_TILE), col),
        out_shape=jax.ShapeDtypeStruct((TOP_K, nt), jnp.int32),
        compiler_params=pltpu.CompilerParams(dimension_semantics=("arbitrary",)),
        name="dest",
    )(top_e_t, rank_t, pstart)


def _experts_kernel(be_ref, nused_ref, c0_ref, wpar_ref, wnext_ref, stok_ref, x1_ref,
                    wg_ref, wu_ref, wd_ref, y_ref,
                    idx0, idx1, idx2, xbuf0, xbuf1, xbuf2, ybuf0, ybuf1, ybuf2, zbuf,
                    idx_sem, row_sem, out_sem, z_sem, wg_f, wu_f, wd_f, w_sem,
                    wg_b, wu_b, wd_b):
    i = pl.program_id(0)
    n_used = nused_ref[0]
    n_steps = pl.num_programs(0)
    idx = (idx0, idx1, idx2)
    xbuf = (xbuf0, xbuf1, xbuf2)
    ybuf = (ybuf0, ybuf1, ybuf2)
    block_rows = EXPERT_BLOCK * ROW_TILES

    def out_rows(block):
        return y_ref.at[pl.ds(pl.multiple_of(block * block_rows, block_rows), block_rows), :]

    def out_copy(block, s):
        return pltpu.make_async_copy(ybuf[s], out_rows(block), out_sem.at[s])

    def zero_copy(block):
        return pltpu.make_async_copy(zbuf, out_rows(block), z_sem)

    def weight_copies(e, s):
        return (pltpu.make_async_copy(wg_ref.at[e], wg_f.at[s], w_sem.at[s]),
                pltpu.make_async_copy(wu_ref.at[e], wu_f.at[s], w_sem.at[s]),
                pltpu.make_async_copy(wd_ref.at[e], wd_f.at[s], w_sem.at[s]))

    def idx_copy(block, s):
        base = pl.multiple_of(c0_ref[block] // LANES * LANES, LANES)
        return pltpu.make_async_copy(stok_ref.at[pl.ds(base, IDX_WINDOW)], idx[s], idx_sem.at[s])

    def start_rows(block, s):
        off = lax.rem(c0_ref[block], LANES)
        for r in range(EXPERT_BLOCK):
            src = pl.multiple_of(idx[s][off + r], ROW_TILES)
            pltpu.make_async_copy(x1_ref.at[pl.ds(src, ROW_TILES), :],
                                  xbuf[s].at[pl.ds(r * ROW_TILES, ROW_TILES), :],
                                  row_sem.at[s]).start(priority=r % 2)

    def wait_rows(s):
        pltpu.make_async_copy(x1_ref.at[pl.ds(0, EXPERT_BLOCK * ROW_TILES), :], xbuf[s],
                              row_sem.at[s]).wait()

    @pl.when(i == 0)
    def _():
        for cp in weight_copies(be_ref[0], 0):
            cp.start()
        idx_copy(0, 0).start()
        idx_copy(1, 1).start()
        idx_copy(0, 0).wait()
        idx_copy(1, 1).wait()
        start_rows(0, 0)
        start_rows(1, 1)
        idx_copy(2, 2).start()

    prev = be_ref[jnp.maximum(i - 1, 0)]

    @pl.when((i < n_used) & ((i == 0) | (be_ref[i] != prev)))
    def _():
        par = wpar_ref[i]
        for cp in weight_copies(be_ref[i], par):
            cp.wait()

        @pl.when(wnext_ref[i] >= 0)
        def _():
            for cp in weight_copies(wnext_ref[i], 1 - par):
                cp.start()

        wg_b[...] = wg_f[par].astype(BF16)
        wu_b[...] = wu_f[par].astype(BF16)
        wd_b[...] = wd_f[par].astype(BF16)

    for slot in range(3):
        ahead = (slot + 2) % 3

        @pl.when((i < n_used) & (i >= 3) & (lax.rem(i, 3) == slot))
        def _():
            out_copy(i - 3, slot).wait()

        @pl.when((i < n_used) & (lax.rem(i, 3) == slot))
        def _():
            idx_copy(i + 2, ahead).wait()
            wait_rows(slot)
            start_rows(i + 2, ahead)
            idx_copy(i + 3, slot).start()
            xb = _load_token_tiles(xbuf[slot], EXPERT_BLOCK).astype(BF16)
            hb = _silu(_dot(xb, wg_b[...])) * _dot(xb, wu_b[...])
            _store_token_tiles(ybuf[slot], _dot(hb.astype(BF16), wd_b[...]))
            out_copy(i, slot).start()

        @pl.when((i == n_used) & (lax.rem(i, 3) == slot))
        def _():
            wait_rows(slot)
            wait_rows((slot + 1) % 3)
            idx_copy(i + 2, ahead).wait()
            for back in range(1, 4):
                @pl.when(i >= back)
                def _():
                    out_copy(i - back, (slot - back) % 3).wait()
            zbuf[...] = jnp.zeros_like(zbuf)

    @pl.when(i >= n_used)
    def _():
        zero_copy(i).start()

    @pl.when(i == n_steps - 1)
    def _():
        def drain(_, carry):
            zero_copy(0).wait()
            return carry
        lax.fori_loop(0, n_steps - n_used, drain, 0)


def _experts(block_e, n_used, block_c0, w_par, w_next, stok, x1, w_gate_e, w_up_e, w_down_e):
    nb = block_e.shape[0]
    grid_spec = pltpu.PrefetchScalarGridSpec(
        num_scalar_prefetch=5,
        grid=(nb,),
        in_specs=[pl.BlockSpec(memory_space=pl.ANY)] * 5,
        out_specs=pl.BlockSpec(memory_space=pl.ANY),
        scratch_shapes=(
            [pltpu.SMEM((IDX_WINDOW,), jnp.int32)] * 3
            + [pltpu.VMEM((EXPERT_BLOCK * ROW_TILES, LANES), F32)] * 7
        ) + [
            pltpu.SemaphoreType.DMA((3,)),
            pltpu.SemaphoreType.DMA((3,)),
            pltpu.SemaphoreType.DMA((3,)),
            pltpu.SemaphoreType.DMA(()),
            pltpu.VMEM((2, D_MODEL, EXPERT_FF), F32),
            pltpu.VMEM((2, D_MODEL, EXPERT_FF), F32),
            pltpu.VMEM((2, EXPERT_FF, D_MODEL), F32),
            pltpu.SemaphoreType.DMA((2,)),
            pltpu.VMEM((D_MODEL, EXPERT_FF), BF16),
            pltpu.VMEM((D_MODEL, EXPERT_FF), BF16),
            pltpu.VMEM((EXPERT_FF, D_MODEL), BF16),
        ],
    )
    return pl.pallas_call(
        _experts_kernel,
        grid_spec=grid_spec,
        out_shape=jax.ShapeDtypeStruct((nb * EXPERT_BLOCK * ROW_TILES, LANES), F32),
        compiler_params=pltpu.CompilerParams(
            dimension_semantics=("arbitrary",), vmem_limit_bytes=VMEM_LIMIT),
        name="experts",
    )(block_e, n_used, block_c0, w_par, w_next, stok, x1, w_gate_e, w_up_e, w_down_e)


def _final_kernel(np_tiles, dest_ref, dest_next_ref, x1_ref, topw_ref, yb_ref, wsg_ref, wsu_ref,
                  wsd_ref, g2_ref, b2_ref, yp_ref, ys_ref, ybuf, sem):
    i = pl.program_id(0)
    slot = lax.rem(i, 2)

    def issue_rows(d_ref, s):
        def body(t, carry):
            dst = pl.multiple_of(t * ROW_TILES, ROW_TILES)
            for j in range(TOP_K):
                src = pl.multiple_of(d_ref[j, t], ROW_TILES)
                pltpu.make_async_copy(yb_ref.at[pl.ds(src, ROW_TILES), :],
                                      ybuf.at[s, j, pl.ds(dst, ROW_TILES), :],
                                      sem.at[s]).start(priority=j % 2)
            return carry
        lax.fori_loop(0, MIX_TILE, body, 0, unroll=4)

    @pl.when(i == 0)
    def _():
        issue_rows(dest_ref, 0)

    @pl.when(i + 1 < pl.num_programs(0))
    def _():
        issue_rows(dest_next_ref, 1 - slot)

    x1 = _load_token_tiles(x1_ref, MIX_TILE)
    xb = x1.astype(BF16)
    hs = _silu(_dot(xb, wsg_ref[...])) * _dot(xb, wsu_ref[...])
    ff = _dot(hs.astype(BF16), wsd_ref[...])
    r = lax.broadcasted_iota(jnp.int32, (MIX_TILE, MIX_TILE), 0)
    c = lax.broadcasted_iota(jnp.int32, (MIX_TILE, MIX_TILE), 1)
    w_cols = _dot_nt((r == c).astype(F32), topw_ref[...], HIGHEST)
    for j in range(TOP_K):
        pltpu.make_async_copy(yb_ref.at[pl.ds(0, MIX_TILE * ROW_TILES), :], ybuf.at[slot, j],
                              sem.at[slot]).wait()
    for j in range(TOP_K):
        ff = ff + w_cols[:, j:j + 1] * _load_token_tiles(ybuf.at[slot, j], MIX_TILE)
    y = _layer_norm(ALPHA * x1 + ff, g2_ref[...], b2_ref[...])

    @pl.when(i < np_tiles)
    def _():
        yp_ref[...] = y

    @pl.when(i >= np_tiles)
    def _():
        ys_ref[...] = y


def _final(n_p, dest, x1, top_w_t, yb, wsg, wsu, wsd, g2, b2):
    nt = x1.shape[0] // ROW_TILES
    np_tiles = n_p // MIX_TILE
    row = lambda i: (i, 0)
    const = lambda i: (0, 0)
    n_tiles = nt // MIX_TILE
    return pl.pallas_call(
        functools.partial(_final_kernel, np_tiles),
        grid=(n_tiles,),
        in_specs=[
            pl.BlockSpec((TOP_K, MIX_TILE), lambda i: (0, i), memory_space=pltpu.SMEM),
            pl.BlockSpec((TOP_K, MIX_TILE), lambda i: (0, jnp.minimum(i + 1, n_tiles - 1)),
                         memory_space=pltpu.SMEM),
            pl.BlockSpec((MIX_TILE * ROW_TILES, LANES), row),
            pl.BlockSpec((TOP_K, MIX_TILE), lambda i: (0, i)),
            pl.BlockSpec(memory_space=pl.ANY),
            pl.BlockSpec((D_MODEL, EXPERT_FF), const),
            pl.BlockSpec((D_MODEL, EXPERT_FF), const),
            pl.BlockSpec((EXPERT_FF, D_MODEL), const),
            pl.BlockSpec((1, D_MODEL), const),
            pl.BlockSpec((1, D_MODEL), const),
        ],
        out_specs=(
            pl.BlockSpec((MIX_TILE, D_MODEL), lambda i: (jnp.minimum(i, np_tiles - 1), 0)),
            pl.BlockSpec((MIX_TILE, D_MODEL), lambda i: (jnp.maximum(i - np_tiles, 0), 0)),
        ),
        out_shape=(
            jax.ShapeDtypeStruct((n_p, D_MODEL), F32),
            jax.ShapeDtypeStruct((nt - n_p, D_MODEL), F32),
        ),
        scratch_shapes=[
            pltpu.VMEM((2, TOP_K, MIX_TILE * ROW_TILES, LANES), F32),
            pltpu.SemaphoreType.DMA((2,)),
        ],
        compiler_params=pltpu.CompilerParams(
            dimension_semantics=("arbitrary",), vmem_limit_bytes=VMEM_LIMIT),
        name="final",
    )(dest, dest, x1, top_w_t, yb, wsg, wsu, wsd, g2, b2)


def _expert_layout(counts, nk):
    nb = -(-nk // EXPERT_BLOCK) + N_EXPERTS
    padded = (counts + EXPERT_BLOCK - 1) // EXPERT_BLOCK * EXPERT_BLOCK
    pend = jnp.cumsum(padded)
    pstart = (pend - padded).astype(jnp.int32)
    block_start = jnp.arange(nb, dtype=jnp.int32) * EXPERT_BLOCK
    block_e = jnp.sum((pend[None, :] <= block_start[:, None]).astype(jnp.int32), 1)
    block_e = jnp.minimum(block_e, N_EXPERTS - 1)
    n_used = (pend[-1] // EXPERT_BLOCK).astype(jnp.int32).reshape(1)
    start = (jnp.cumsum(counts) - counts).astype(jnp.int32)
    block_c0 = start[block_e] + block_start - pstart[block_e]
    block_c0 = jnp.where(block_start < pend[-1], block_c0, 0)
    block_c0 = jnp.concatenate([block_c0, jnp.zeros((2,), jnp.int32)]).astype(jnp.int32)
    ids = jnp.arange(nb, dtype=jnp.int32)
    used = block_start < pend[-1]
    switch = used & ((ids == 0) | (block_e != jnp.roll(block_e, 1)))
    w_par = ((jnp.cumsum(switch.astype(jnp.int32)) - 1) % 2).astype(jnp.int32)
    later_switch = (ids[None, :] > ids[:, None]) & switch[None, :]
    next_pos = jnp.min(jnp.where(later_switch, ids[None, :], nb), axis=1)
    w_next = jnp.where(next_pos < nb, block_e[jnp.minimum(next_pos, nb - 1)], -1).astype(jnp.int32)
    return pstart, block_e, n_used, block_c0, w_par, w_next


def _pad_lanes(v, offset=0):
    out = jnp.zeros((1, LANES), F32)
    return out.at[0, offset:offset + v.shape[0]].set(v.astype(F32))


def kernel(x_prompt, x_sample, state_conv, state_gdn, cache_swa_k, cache_swa_v, w_in, conv_w, a_log, dt_bias, gdn_norm_w, attn_sinks, w_o, ln1_g, ln1_b, w_router, router_bias, w_gate_e, w_up_e, w_down_e, w_shared_gate, w_shared_up, w_shared_down, ln2_g, ln2_b):
    bp, tp, d = x_prompt.shape
    bs, ts, _ = x_sample.shape
    n_p, n_s = bp * tp, bs * ts

    wi = w_in[0]
    o1 = CONV_DIM
    o2 = o1 + GDN_WIDTH
    o4 = o2 + 2 * GDN_HEADS
    o5 = o4 + SWA_WIDTH
    o6 = o5 + SWA_KV_WIDTH
    ba_cols = jnp.zeros((d, LANES), F32).at[:, :2 * GDN_HEADS].set(wi[:, o2:o4])
    w_all = jnp.concatenate([wi[:, :o2], wi[:, o4:o5], wi[:, o5:o6], wi[:, o6:], ba_cols], 1).astype(BF16)
    wo = w_o[0].astype(BF16)
    wrt = w_router[0].T.astype(BF16)
    rbias = router_bias[0].astype(F32).reshape(N_EXPERTS, 1)
    alog_pad = _pad_lanes(a_log[0], GDN_HEADS)
    dtb_pad = _pad_lanes(dt_bias[0], GDN_HEADS)
    sinks_pad = _pad_lanes(attn_sinks[0])
    norm_w = gdn_norm_w[0].reshape(1, GDN_D)
    g1, b1 = ln1_g[0].reshape(1, d), ln1_b[0].reshape(1, d)
    g2, b2 = ln2_g[0].reshape(1, d), ln2_b[0].reshape(1, d)

    def front(x, pos0, L, conv_hist, s0, k_hist, v_hist):
        b, t, _ = x.shape
        period = max(t, IN_TILE)
        pos = pos0 + (jnp.arange(period, dtype=jnp.int32) % t).astype(F32)
        cos, s1, s2 = _rope_tables(pos)
        x2d = x.reshape(b * t, d)
        qkv, z, sq, sk, sv, ba = _in_proj(x2d, w_all, cos, s1, s2)
        qkv = qkv.reshape(b, t, CONV_DIM)
        hist8 = jnp.concatenate(
            [jnp.zeros((b, HIST_ROWS - (CONV_WIDTH - 1), CONV_DIM), F32), conv_hist], 1)
        gd, s_new = _gdn(L, qkv, z.reshape(b, t, GDN_WIDTH), ba.reshape(b, t, LANES), hist8, s0,
                         conv_w[0], alog_pad, dtb_pad, norm_w)
        k_ext = jnp.concatenate([k_hist.reshape(b, WINDOW, SWA_KV_WIDTH),
                                 sk.reshape(b, t, SWA_KV_WIDTH)], 1)
        v_ext = jnp.concatenate([v_hist.reshape(b, WINDOW, SWA_KV_WIDTH),
                                 sv.reshape(b, t, SWA_KV_WIDTH)], 1)
        sw = _swa(L, pos0, sq.reshape(b, t, SWA_WIDTH), k_ext, v_ext, sinks_pad)
        conv_new = jnp.concatenate([conv_hist, qkv], 1)[:, -(CONV_WIDTH - 1):]
        k_new = k_ext[:, -WINDOW:].reshape(b, WINDOW, SWA_KV_HEADS, SWA_HEAD_DIM)
        v_new = v_ext[:, -WINDOW:].reshape(b, WINDOW, SWA_KV_HEADS, SWA_HEAD_DIM)
        return (x2d, gd.reshape(b * t, GDN_WIDTH), sw.reshape(b * t, SWA_WIDTH),
                conv_new, s_new, k_new, v_new)

    zeros = lambda *s: jnp.zeros(s, F32)
    xp2, gd_p, sw_p, conv_p, gdn_p, k_p, v_p = front(
        x_prompt, 0, CHUNK, zeros(bp, CONV_WIDTH - 1, CONV_DIM),
        zeros(bp, GDN_HEADS, GDN_D, GDN_D), zeros(bp, WINDOW, SWA_KV_WIDTH),
        zeros(bp, WINDOW, SWA_KV_WIDTH))
    xs2, gd_s, sw_s, conv_s, gdn_s, k_s, v_s = front(
        x_sample, PAST_LEN, ts, state_conv[0], state_gdn[0], cache_swa_k[0], cache_swa_v[0])

    x1, top_e_t, top_w_t, rank_t, counts = _mix(
        gd_p, sw_p, xp2, gd_s, sw_s, xs2, wo, g1, b1, wrt, rbias)

    pstart, block_e, n_used, block_c0, w_par, w_next = _expert_layout(
        counts[:, 0], TOP_K * (n_p + n_s))
    dest = _dest(top_e_t, rank_t, pstart.reshape(N_EXPERTS, 1))
    order = jnp.argsort(top_e_t.T.reshape(-1))
    stok = jnp.concatenate([(order // TOP_K * ROW_TILES).astype(jnp.int32),
                            jnp.zeros((IDX_WINDOW,), jnp.int32)])
    yb = _experts(block_e, n_used, block_c0, w_par, w_next, stok, x1,
                  w_gate_e[0], w_up_e[0], w_down_e[0])

    y_p, y_s = _final(n_p, dest, x1, top_w_t, yb,
                      w_shared_gate[0].astype(BF16), w_shared_up[0].astype(BF16),
                      w_shared_down[0].astype(BF16), g2, b2)
    return (y_p.reshape(bp, tp, d), y_s.reshape(bs, ts, d),
            conv_p[None], gdn_p[None], k_p[None], v_p[None],
            conv_s[None], gdn_s[None], k_s[None], v_s[None])
```

```python
import functools
import math

import jax
import jax.numpy as jnp
from jax import lax
from jax.experimental import pallas as pl
from jax.experimental.pallas import tpu as pltpu

F32 = jnp.float32
BF16 = jnp.bfloat16
HIGHEST = lax.Precision.HIGHEST

D_MODEL = 1024
CHUNK = 64
GDN_HEADS = 4
GDN_D = 128
GDN_WIDTH = GDN_HEADS * GDN_D
CONV_DIM = 3 * GDN_WIDTH
CONV_WIDTH = 4
SWA_HEAD_DIM = 64
SWA_Q_HEADS = 8
SWA_KV_HEADS = 2
SWA_GROUP = SWA_Q_HEADS // SWA_KV_HEADS
SWA_WIDTH = SWA_Q_HEADS * SWA_HEAD_DIM
SWA_KV_WIDTH = SWA_KV_HEADS * SWA_HEAD_DIM
WINDOW = 128
ROPE_THETA = 500000.0
ROT_DIM = SWA_HEAD_DIM // 4
ROT_HALF = ROT_DIM // 2
N_EXPERTS = 256
TOP_K = 8
N_GROUPS = 8
GROUP_SIZE = N_EXPERTS // N_GROUPS
TOPK_GROUPS = 4
EXPERT_FF = 256
ROUTED_SCALE = 2.5
ALPHA = 2.0 ** 0.25
LN_EPS = 1e-5
NORM_EPS = 1e-6
PAST_LEN = 4096

LANES = 128
HIST_ROWS = 8
ROW_TILES = D_MODEL // LANES

C_QKV = 0
C_Z = C_QKV + CONV_DIM
C_SQ = C_Z + GDN_WIDTH
C_SK = C_SQ + SWA_WIDTH
C_SV = C_SK + SWA_KV_WIDTH
C_BA = C_SV + SWA_KV_WIDTH
C_END = C_BA + LANES

IN_TILE = 512
MIX_TILE = 256
DEST_TILE = 512
EXPERT_BLOCK = 256
IDX_WINDOW = EXPERT_BLOCK + LANES
GDN_BATCH = 8
SWA_CHUNKS = 8
VMEM_LIMIT = 56 * 1024 * 1024


def _dot(a, b, precision=None):
    return jnp.dot(a, b, preferred_element_type=F32, precision=precision)


def _dot_nt(a, b, precision=None):
    return lax.dot_general(a, b, (((1,), (1,)), ((), ())),
                           preferred_element_type=F32, precision=precision)


def _dot_tn(a, b):
    return lax.dot_general(a, b, (((0,), (0,)), ((), ())), preferred_element_type=F32)


def _split(x):
    hi = x.astype(BF16)
    return hi, (x - hi.astype(F32)).astype(BF16)


def _dot_split(a, b):
    return _dot(a[0], b[0]) + (_dot(a[1], b[0]) + _dot(a[0], b[1]))


def _silu(x):
    return x * jax.nn.sigmoid(x)


def _store_token_tiles(ref, x):
    m = x.shape[0]
    for s in range(ROW_TILES):
        ref[pl.ds(s, m, stride=ROW_TILES), :] = x[:, s * LANES:(s + 1) * LANES]


def _load_token_tiles(ref, m):
    return jnp.concatenate([ref[pl.ds(s, m, stride=ROW_TILES), :] for s in range(ROW_TILES)], 1)


def _in_proj_kernel(x_ref, w_ref, cos_ref, s1_ref, s2_ref,
                    qkv_ref, z_ref, sq_ref, sk_ref, sv_ref, ba_ref):
    xb = x_ref[...].astype(BF16)
    qkv_ref[...] = _dot(xb, w_ref[:, C_QKV:C_Z])
    z_ref[...] = _dot(xb, w_ref[:, C_Z:C_SQ])
    sv_ref[...] = _dot(xb, w_ref[:, C_SV:C_BA])
    ba_ref[...] = _dot(xb, w_ref[:, C_BA:C_END])
    cos = cos_ref[...]
    s1 = s1_ref[...]
    s2 = s2_ref[...]

    def rope(t):
        return (t * cos + pltpu.roll(t, LANES - ROT_HALF, 1) * s1
                + pltpu.roll(t, ROT_HALF, 1) * s2)

    sk_ref[...] = rope(_dot(xb, w_ref[:, C_SK:C_SV]))
    for g in range(SWA_WIDTH // LANES):
        lo = C_SQ + g * LANES
        sq_ref[:, g * LANES:(g + 1) * LANES] = rope(_dot(xb, w_ref[:, lo:lo + LANES])).astype(BF16)


def _in_proj(x2d, w_all, cos, s1, s2):
    n = x2d.shape[0]
    period_tiles = cos.shape[0] // IN_TILE
    row = lambda i: (i, 0)
    tab = lambda i: (i % period_tiles, 0)
    out_shapes = (
        jax.ShapeDtypeStruct((n, CONV_DIM), F32),
        jax.ShapeDtypeStruct((n, GDN_WIDTH), F32),
        jax.ShapeDtypeStruct((n, SWA_WIDTH), BF16),
        jax.ShapeDtypeStruct((n, SWA_KV_WIDTH), F32),
        jax.ShapeDtypeStruct((n, SWA_KV_WIDTH), F32),
        jax.ShapeDtypeStruct((n, LANES), F32),
    )
    return pl.pallas_call(
        _in_proj_kernel,
        grid=(n // IN_TILE,),
        in_specs=[
            pl.BlockSpec((IN_TILE, D_MODEL), row),
            pl.BlockSpec((D_MODEL, C_END), lambda i: (0, 0)),
            pl.BlockSpec((IN_TILE, LANES), tab),
            pl.BlockSpec((IN_TILE, LANES), tab),
            pl.BlockSpec((IN_TILE, LANES), tab),
        ],
        out_specs=(
            pl.BlockSpec((IN_TILE, CONV_DIM), row),
            pl.BlockSpec((IN_TILE, GDN_WIDTH), row),
            pl.BlockSpec((IN_TILE, SWA_WIDTH), row),
            pl.BlockSpec((IN_TILE, SWA_KV_WIDTH), row),
            pl.BlockSpec((IN_TILE, SWA_KV_WIDTH), row),
            pl.BlockSpec((IN_TILE, LANES), row),
        ),
        out_shape=out_shapes,
        compiler_params=pltpu.CompilerParams(
            dimension_semantics=("arbitrary",), vmem_limit_bytes=VMEM_LIMIT),
        name="in_proj",
    )(x2d, w_all, cos, s1, s2)


def _rope_tables(pos):
    p = pos.shape[0]
    inv = ROPE_THETA ** (-jnp.arange(0, ROT_DIM, 2, dtype=F32) / ROT_DIM)
    ang = pos[:, None] * inv[None, :]
    cos = jnp.cos(ang)
    sin = jnp.sin(ang)
    rest = SWA_HEAD_DIM - ROT_DIM
    head_c = jnp.concatenate([cos, cos, jnp.ones((p, rest), F32)], 1)
    head_s1 = jnp.concatenate([-sin, jnp.zeros((p, SWA_HEAD_DIM - ROT_HALF), F32)], 1)
    head_s2 = jnp.concatenate([jnp.zeros((p, ROT_HALF), F32), sin, jnp.zeros((p, rest), F32)], 1)
    two = lambda t: jnp.concatenate([t, t], 1)
    return two(head_c), two(head_s1), two(head_s2)


def _gdn_kernel(L, qkv_ref, z_ref, ba_ref, hist_ref, s0_ref, convw_ref, alog_ref, dtb_ref,
                normw_ref, o_ref, sout_ref, xbuf, state):
    c = pl.program_id(1)
    nbat = qkv_ref.shape[0]
    chains = [(b, h) for b in range(nbat) for h in range(GDN_HEADS)]

    @pl.when(c == 0)
    def _():
        xbuf[:, 0:HIST_ROWS, :] = hist_ref[...]
        state[...] = s0_ref[...]

    xbuf[:, HIST_ROWS:HIST_ROWS + L, :] = qkv_ref[...]

    row = lax.broadcasted_iota(jnp.int32, (L, L), 0)
    col = lax.broadcasted_iota(jnp.int32, (L, L), 1)
    incl = row >= col
    strict = row > col
    lower_ones = incl.astype(BF16)
    eye = (row == col).astype(F32)

    beta_all, g_all = [], []
    for b in range(nbat):
        ba = ba_ref[b]
        beta_all.append(jax.nn.sigmoid(ba))
        sp_in = ba + dtb_ref[...]
        softplus = jnp.maximum(sp_in, 0.0) + jnp.log1p(jnp.exp(-jnp.abs(sp_in)))
        g_all.append(-jnp.exp(alog_ref[...]) * softplus)

    def conv_silu(b, c0):
        acc = None
        for j in range(CONV_WIDTH):
            lo = HIST_ROWS - (CONV_WIDTH - 1) + j
            t = xbuf[b, lo:lo + L, c0:c0 + GDN_D] * convw_ref[j:j + 1, c0:c0 + GDN_D]
            acc = t if acc is None else acc + t
        return _silu(acc)

    def l2n(t):
        return t * lax.rsqrt(jnp.sum(t * t, -1, keepdims=True) + NORM_EPS)

    q = [l2n(conv_silu(b, h * GDN_D)) * (GDN_D ** -0.5) for b, h in chains]
    k = [l2n(conv_silu(b, GDN_WIDTH + h * GDN_D)) for b, h in chains]
    v = [conv_silu(b, 2 * GDN_WIDTH + h * GDN_D) for b, h in chains]
    beta = [beta_all[b][:, h:h + 1] for b, h in chains]
    g = [g_all[b][:, GDN_HEADS + h:GDN_HEADS + h + 1] for b, h in chains]

    def pieces(t):
        p1 = t.astype(BF16).astype(F32)
        r1 = t - p1
        p2 = r1.astype(BF16).astype(F32)
        return p1, p2, r1 - p2

    g_parts = [pieces(t) for t in g]
    diff = [sum(_dot(lower_ones, jnp.where(strict, p, 0.0).astype(BF16)) for p in gp)
            for gp in g_parts]
    gcum3 = [_dot(lower_ones, jnp.concatenate(
        [jnp.broadcast_to(p, (L, GDN_D)) for p in gp], 1).astype(BF16)) for gp in g_parts]
    gcum = [t[:, :GDN_D] + t[:, GDN_D:2 * GDN_D] + t[:, 2 * GDN_D:] for t in gcum3]
    dec = [jnp.where(incl, jnp.exp(jnp.where(incl, d, 0.0)), 0.0) for d in diff]
    exp_g = [jnp.exp(t) for t in gcum]
    g_end = [t[L - 1:L, :] for t in gcum]
    kb = [t.astype(BF16) for t in k]
    kk = [_dot_nt(t, t) for t in kb]
    aqk = [_dot_nt(qq.astype(BF16), t) * d for qq, t, d in zip(q, kb, dec)]
    a = [jnp.where(strict, d * t, 0.0) * bt for d, t, bt in zip(dec, kk, beta)]

    n = [-t for t in a]
    n_s = [_split(t) for t in n]
    z = [jnp.concatenate([eye + t, _dot_split(ts, ts)], 1) for t, ts in zip(n, n_s)]
    keep_s = lax.broadcasted_iota(jnp.int32, (L, 2 * L), 1) < L
    for _ in range(int(math.log2(L)) - 2):
        pz = [_dot_split(_split(t[:, L:]), _split(t)) for t in z]
        z = [jnp.where(keep_s, t, 0.0) + d for t, d in zip(z, pz)]
    inv = [t[:, :L] + _dot_split(_split(t[:, L:]), _split(t[:, :L])) for t in z]
    inv_s = [_split(t) for t in inv]
    wy = [_dot_split(t, _split(jnp.concatenate([bt * vv, (bt * eg) * kk_], 1)))
          for t, bt, vv, eg, kk_ in zip(inv_s, beta, v, exp_g, k)]
    w = [t[:, :GDN_D] for t in wy]
    y = [t[:, GDN_D:] for t in wy]

    s = [state[b, h] for b, h in chains]
    sb = [t.astype(BF16) for t in s]
    yq = [_dot(jnp.concatenate([yy.astype(BF16), (eg * qq).astype(BF16)], 0), t)
          for yy, eg, qq, t in zip(y, exp_g, q, sb)]
    u = [ww - t[:L] for ww, t in zip(w, yq)]
    ub = [t.astype(BF16) for t in u]
    o = [t[L:] + _dot(aa.astype(BF16), uu) for t, aa, uu in zip(yq, aqk, ub)]
    k_dec = [(jnp.exp(ge - gc) * kk_).astype(BF16) for ge, gc, kk_ in zip(g_end, gcum, k)]
    s_new = [jnp.exp(ge) * t + _dot_tn(kd, uu) for ge, t, kd, uu in zip(g_end, s, k_dec, ub)]

    for (b, h), t, oo in zip(chains, s_new, o):
        state[b, h] = t
        on = oo * lax.rsqrt(jnp.mean(oo * oo, -1, keepdims=True) + NORM_EPS)
        zz = z_ref[b, :, h * GDN_D:(h + 1) * GDN_D]
        o_ref[b, :, h * GDN_D:(h + 1) * GDN_D] = (on * normw_ref[...] * _silu(zz)).astype(BF16)

    xbuf[:, 0:HIST_ROWS, :] = xbuf[:, L:L + HIST_ROWS, :]

    @pl.when(c == pl.num_programs(1) - 1)
    def _():
        sout_ref[...] = state[...]


def _gdn(L, qkv, z, ba, hist8, s0, conv_w, alog_pad, dtb_pad, norm_w):
    b, t, _ = qkv.shape
    nc = t // L
    nbat = GDN_BATCH
    tok = lambda i, c: (i, c, 0)
    per_b3 = lambda i, c: (i, 0, 0)
    per_b4 = lambda i, c: (i, 0, 0, 0)
    const2 = lambda i, c: (0, 0)
    return pl.pallas_call(
        functools.partial(_gdn_kernel, L),
        grid=(b // nbat, nc),
        in_specs=[
            pl.BlockSpec((nbat, L, CONV_DIM), tok),
            pl.BlockSpec((nbat, L, GDN_WIDTH), tok),
            pl.BlockSpec((nbat, L, LANES), tok),
            pl.BlockSpec((nbat, HIST_ROWS, CONV_DIM), per_b3),
            pl.BlockSpec((nbat, GDN_HEADS, GDN_D, GDN_D), per_b4),
            pl.BlockSpec((CONV_WIDTH, CONV_DIM), const2),
            pl.BlockSpec((1, LANES), const2),
            pl.BlockSpec((1, LANES), const2),
            pl.BlockSpec((1, GDN_D), const2),
        ],
        out_specs=(
            pl.BlockSpec((nbat, L, GDN_WIDTH), tok),
            pl.BlockSpec((nbat, GDN_HEADS, GDN_D, GDN_D), per_b4),
        ),
        out_shape=(
            jax.ShapeDtypeStruct((b, t, GDN_WIDTH), BF16),
            jax.ShapeDtypeStruct((b, GDN_HEADS, GDN_D, GDN_D), F32),
        ),
        scratch_shapes=[
            pltpu.VMEM((nbat, HIST_ROWS + L, CONV_DIM), F32),
            pltpu.VMEM((nbat, GDN_HEADS, GDN_D, GDN_D), F32),
        ],
        compiler_params=pltpu.CompilerParams(
            dimension_semantics=("arbitrary", "arbitrary"), vmem_limit_bytes=VMEM_LIMIT),
        name="gdn",
    )(qkv, z, ba, hist8, s0, conv_w, alog_pad, dtb_pad, norm_w)


def _swa_kernel(L, pos0, q_ref, k_ref, v_ref, sink_ref, o_ref):
    c = pl.program_id(1)
    nch = q_ref.shape[1] // L
    span = WINDOW + L
    rows = WINDOW + nch * L
    start = pl.multiple_of(c * (nch * L), L)
    kx = k_ref[0, pl.ds(start, rows), :]
    vx = v_ref[0, pl.ds(start, rows), :]
    kx_sw = pltpu.roll(kx, SWA_HEAD_DIM, 1)
    vx_sw = pltpu.roll(vx, SWA_HEAD_DIM, 1)
    low = lax.broadcasted_iota(jnp.int32, (rows, LANES), 1) < SWA_HEAD_DIM

    def halves(x, x_sw, kh):
        src_lo, src_hi = (x, x_sw) if kh == 0 else (x_sw, x)
        return (jnp.where(low, src_lo, 0.0).astype(BF16), jnp.where(low, 0.0, src_hi).astype(BF16))

    k_halves = [halves(kx, kx_sw, kh) for kh in range(SWA_KV_HEADS)]
    v_halves = [halves(vx, vx_sw, kh) for kh in range(SWA_KV_HEADS)]
    sinks = sink_ref[...]
    top_rows = lax.broadcasted_iota(jnp.int32, (2 * L, 1), 0) < L
    first_pos = pos0 - WINDOW + c * (nch * L)

    chains = [(u, kh, half) for u in range(nch) for kh in range(SWA_KV_HEADS) for half in range(2)]
    valid = [first_pos + u * L + lax.broadcasted_iota(jnp.int32, (1, span), 1) >= 0
             for u in range(nch)]
    q4 = {}
    for u in range(nch):
        for kh in range(SWA_KV_HEADS):
            c0 = kh * 2 * LANES
            q4[u, kh] = jnp.concatenate([q_ref[0, u * L:(u + 1) * L, c0:c0 + LANES],
                                         q_ref[0, u * L:(u + 1) * L, c0 + LANES:c0 + 2 * LANES]], 0)
    scores = [jnp.where(valid[u],
                        _dot_nt(q4[u, kh], k_halves[kh][half][u * L:u * L + span])
                        * (SWA_HEAD_DIM ** -0.5), -jnp.inf)
              for u, kh, half in chains]
    sink_cols = [jnp.where(top_rows, sinks[:, kh * SWA_GROUP + half:kh * SWA_GROUP + half + 1],
                           sinks[:, kh * SWA_GROUP + half + 2:kh * SWA_GROUP + half + 3])
                 for _, kh, half in chains]
    m = [jnp.maximum(jnp.max(s, -1, keepdims=True), sk) for s, sk in zip(scores, sink_cols)]
    p = [jnp.exp(s - mm) for s, mm in zip(scores, m)]
    den = [jnp.sum(pp, -1, keepdims=True) + jnp.exp(sk - mm) for pp, sk, mm in zip(p, sink_cols, m)]
    pv = [_dot(pp.astype(BF16), v_halves[kh][half][u * L:u * L + span]) / dd
          for pp, dd, (u, kh, half) in zip(p, den, chains)]
    for n in range(0, len(chains), 2):
        u, kh, _ = chains[n]
        o = pv[n] + pv[n + 1]
        c0 = kh * 2 * LANES
        o_ref[0, u * L:(u + 1) * L, c0:c0 + LANES] = o[0:L].astype(BF16)
        o_ref[0, u * L:(u + 1) * L, c0 + LANES:c0 + 2 * LANES] = o[L:2 * L].astype(BF16)


def _swa(L, pos0, q, k_ext, v_ext, sinks_pad):
    b, t, _ = q.shape
    ext = k_ext.shape[1]
    nch = SWA_CHUNKS if (t // L) % SWA_CHUNKS == 0 else 1
    return pl.pallas_call(
        functools.partial(_swa_kernel, L, pos0),
        grid=(b, t // (nch * L)),
        in_specs=[
            pl.BlockSpec((1, nch * L, SWA_WIDTH), lambda i, c: (i, c, 0)),
            pl.BlockSpec((1, ext, SWA_KV_WIDTH), lambda i, c: (i, 0, 0)),
            pl.BlockSpec((1, ext, SWA_KV_WIDTH), lambda i, c: (i, 0, 0)),
            pl.BlockSpec((1, LANES), lambda i, c: (0, 0)),
        ],
        out_specs=pl.BlockSpec((1, nch * L, SWA_WIDTH), lambda i, c: (i, c, 0)),
        out_shape=jax.ShapeDtypeStruct((b, t, SWA_WIDTH), BF16),
        compiler_params=pltpu.CompilerParams(
            dimension_semantics=("arbitrary", "arbitrary"), vmem_limit_bytes=VMEM_LIMIT),
        name="swa",
    )(q, k_ext, v_ext, sinks_pad)


def _layer_norm(r, g, b):
    mu = jnp.mean(r, -1, keepdims=True)
    d = r - mu
    var = jnp.mean(d * d, -1, keepdims=True)
    return d * lax.rsqrt(var + LN_EPS) * g + b


def _route(x1b, wrt_ref, rbias_ref, count_ref):
    tm = x1b.shape[0]
    scores = jax.nn.sigmoid(_dot_nt(wrt_ref[...], x1b))
    biased = scores + rbias_ref[...]
    neg = -jnp.inf
    r_g = lax.broadcasted_iota(jnp.int32, (GROUP_SIZE, tm), 0)
    blocks = []
    group_score = []
    for g in range(N_GROUPS):
        blk = biased[g * GROUP_SIZE:(g + 1) * GROUP_SIZE]
        m1 = jnp.max(blk, 0, keepdims=True)
        i1 = jnp.min(jnp.where(blk == m1, r_g, GROUP_SIZE), 0, keepdims=True)
        m2 = jnp.max(jnp.where(r_g == i1, neg, blk), 0, keepdims=True)
        blocks.append(blk)
        group_score.append(m1 + m2)
    masked = []
    for g in range(N_GROUPS):
        rank = jnp.zeros((1, tm), jnp.int32)
        for o in range(N_GROUPS):
            if o == g:
                continue
            ahead = group_score[o] > group_score[g]
            if o < g:
                ahead = ahead | (group_score[o] == group_score[g])
            rank = rank + ahead.astype(jnp.int32)
        masked.append(jnp.where(rank < TOPK_GROUPS, blocks[g], neg))
    masked = jnp.concatenate(masked, 0)
    r_e = lax.broadcasted_iota(jnp.int32, (N_EXPERTS, tm), 0)
    idx_rows = []
    w_rows = []
    hits = []
    for _ in range(TOP_K):
        m = jnp.max(masked, 0, keepdims=True)
        idx = jnp.min(jnp.where(masked == m, r_e, N_EXPERTS), 0, keepdims=True)
        hit = r_e == idx
        w_rows.append(jnp.sum(jnp.where(hit, scores, 0.0), 0, keepdims=True))
        masked = jnp.where(hit, neg, masked)
        idx_rows.append(idx)
        hits.append(hit)
    total = w_rows[0]
    for wr in w_rows[1:]:
        total = total + wr
    top_w = jnp.concatenate(w_rows, 0) / total * ROUTED_SCALE

    chosen = jnp.zeros((N_EXPERTS, tm), F32)
    for hit in hits:
        chosen = chosen + hit.astype(F32)
    t_r = lax.broadcasted_iota(jnp.int32, (tm, tm), 0)
    t_c = lax.broadcasted_iota(jnp.int32, (tm, tm), 1)
    before = (t_r < t_c).astype(BF16)
    prior = _dot(chosen.astype(BF16), before) + count_ref[...]
    rank_rows = [jnp.sum(jnp.where(hit, prior, 0.0), 0, keepdims=True) for hit in hits]
    count_ref[...] = count_ref[...] + jnp.sum(chosen, 1, keepdims=True)
    rank = jnp.concatenate(rank_rows, 0).astype(jnp.int32)
    return jnp.concatenate(idx_rows, 0), top_w, rank


def _mix_kernel(np_tiles, gp_ref, sp_ref, xp_ref, gs_ref, ss_ref, xs_ref, wo_ref, g1_ref, b1_ref,
                wrt_ref, rbias_ref, x1_ref, tope_ref, topw_ref, rank_ref, cnt_ref, count_acc):
    i = pl.program_id(0)

    @pl.when(i == 0)
    def _():
        count_acc[...] = jnp.zeros_like(count_acc)

    is_prompt = i < np_tiles
    gd = jnp.where(is_prompt, gp_ref[...], gs_ref[...])
    sw = jnp.where(is_prompt, sp_ref[...], ss_ref[...])
    x = jnp.where(is_prompt, xp_ref[...], xs_ref[...])
    mix = _dot(gd, wo_ref[0:GDN_WIDTH, :]) + _dot(sw, wo_ref[GDN_WIDTH:, :])
    x1 = _layer_norm(ALPHA * x + mix, g1_ref[...], b1_ref[...])
    _store_token_tiles(x1_ref, x1)
    top_e, top_w, rank = _route(x1.astype(BF16), wrt_ref, rbias_ref, count_acc)
    tope_ref[...] = top_e
    topw_ref[...] = top_w
    rank_ref[...] = rank
    cnt_ref[...] = count_acc[...].astype(jnp.int32)


def _mix(gd_p, sw_p, x_p, gd_s, sw_s, x_s, wo, g1, b1, wrt, rbias):
    n_p, n_s = x_p.shape[0], x_s.shape[0]
    np_tiles = n_p // MIX_TILE
    nt = n_p + n_s
    pm = lambda i: (jnp.minimum(i, np_tiles - 1), 0)
    sm = lambda i: (jnp.maximum(i - np_tiles, 0), 0)
    const = lambda i: (0, 0)
    col = lambda i: (0, i)
    return pl.pallas_call(
        functools.partial(_mix_kernel, np_tiles),
        grid=(nt // MIX_TILE,),
        in_specs=[
            pl.BlockSpec((MIX_TILE, GDN_WIDTH), pm),
            pl.BlockSpec((MIX_TILE, SWA_WIDTH), pm),
            pl.BlockSpec((MIX_TILE, D_MODEL), pm),
            pl.BlockSpec((MIX_TILE, GDN_WIDTH), sm),
            pl.BlockSpec((MIX_TILE, SWA_WIDTH), sm),
            pl.BlockSpec((MIX_TILE, D_MODEL), sm),
            pl.BlockSpec((D_MODEL, D_MODEL), const),
            pl.BlockSpec((1, D_MODEL), const),
            pl.BlockSpec((1, D_MODEL), const),
            pl.BlockSpec((N_EXPERTS, D_MODEL), const),
            pl.BlockSpec((N_EXPERTS, 1), const),
        ],
        out_specs=(
            pl.BlockSpec((MIX_TILE * ROW_TILES, LANES), lambda i: (i, 0)),
            pl.BlockSpec((TOP_K, MIX_TILE), col),
            pl.BlockSpec((TOP_K, MIX_TILE), col),
            pl.BlockSpec((TOP_K, MIX_TILE), col),
            pl.BlockSpec((N_EXPERTS, 1), const),
        ),
        out_shape=(
            jax.ShapeDtypeStruct((nt * ROW_TILES, LANES), F32),
            jax.ShapeDtypeStruct((TOP_K, nt), jnp.int32),
            jax.ShapeDtypeStruct((TOP_K, nt), F32),
            jax.ShapeDtypeStruct((TOP_K, nt), jnp.int32),
            jax.ShapeDtypeStruct((N_EXPERTS, 1), jnp.int32),
        ),
        scratch_shapes=[pltpu.VMEM((N_EXPERTS, 1), F32)],
        compiler_params=pltpu.CompilerParams(
            dimension_semantics=("arbitrary",), vmem_limit_bytes=VMEM_LIMIT),
        name="mix",
    )(gd_p, sw_p, x_p, gd_s, sw_s, x_s, wo, g1, b1, wrt, rbias)


def _dest_kernel(tope_ref, rank_ref, pstart_ref, dest_ref):
    tm = tope_ref.shape[1]
    r_e = lax.broadcasted_iota(jnp.int32, (N_EXPERTS, tm), 0)
    pstart = pstart_ref[...]
    top_e = tope_ref[...]
    rows = [jnp.sum(jnp.where(r_e == top_e[j:j + 1, :], pstart, 0), 0, keepdims=True)
            for j in range(TOP_K)]
    dest_ref[...] = (jnp.concatenate(rows, 0) + rank_ref[...]) * ROW_TILES


def _dest(top_e_t, rank_t, pstart):
    nt = top_e_t.shape[1]
    col = lambda i: (0, i)
    return pl.pallas_call(
        _dest_kernel,
        grid=(nt // DEST_TILE,),
        in_specs=[
            pl.BlockSpec((TOP_K, DEST_TILE), col),
            pl.BlockSpec((TOP_K, DEST_TILE), col),
            pl.BlockSpec((N_EXPERTS, 1), lambda i: (0, 0)),
        ],
        out_specs=pl.BlockSpec((TOP_K, DEST_TILE), col),
        out_shape=jax.ShapeDtypeStruct((TOP_K, nt), jnp.int32),
        compiler_params=pltpu.CompilerParams(dimension_semantics=("arbitrary",)),
        name="dest",
    )(top_e_t, rank_t, pstart)


def _experts_kernel(be_ref, nused_ref, c0_ref, wpar_ref, wnext_ref, stok_ref, x1_ref,
                    wg_ref, wu_ref, wd_ref, y_ref,
                    idx0, idx1, idx2, xbuf0, xbuf1, xbuf2, ybuf0, ybuf1, ybuf2, zbuf,
                    idx_sem, row_sem, out_sem, z_sem, wg_f, wu_f, wd_f, w_sem,
                    wg_b, wu_b, wd_b):
    i = pl.program_id(0)
    n_used = nused_ref[0]
    n_steps = pl.num_programs(0)
    idx = (idx0, idx1, idx2)
    xbuf = (xbuf0, xbuf1, xbuf2)
    ybuf = (ybuf0, ybuf1, ybuf2)
    block_rows = EXPERT_BLOCK * ROW_TILES

    def out_rows(block):
        return y_ref.at[pl.ds(pl.multiple_of(block * block_rows, block_rows), block_rows), :]

    def out_copy(block, s):
        return pltpu.make_async_copy(ybuf[s], out_rows(block), out_sem.at[s])

    def zero_copy(block):
        return pltpu.make_async_copy(zbuf, out_rows(block), z_sem)

    def weight_copies(e, s):
        return (pltpu.make_async_copy(wg_ref.at[e], wg_f.at[s], w_sem.at[s]),
                pltpu.make_async_copy(wu_ref.at[e], wu_f.at[s], w_sem.at[s]),
                pltpu.make_async_copy(wd_ref.at[e], wd_f.at[s], w_sem.at[s]))

    def idx_copy(block, s):
        base = pl.multiple_of(c0_ref[block] // LANES * LANES, LANES)
        return pltpu.make_async_copy(stok_ref.at[pl.ds(base, IDX_WINDOW)], idx[s], idx_sem.at[s])

    def start_rows(block, s):
        off = lax.rem(c0_ref[block], LANES)
        for r in range(EXPERT_BLOCK):
            src = pl.multiple_of(idx[s][off + r], ROW_TILES)
            pltpu.make_async_copy(x1_ref.at[pl.ds(src, ROW_TILES), :],
                                  xbuf[s].at[pl.ds(r * ROW_TILES, ROW_TILES), :],
                                  row_sem.at[s]).start()

    def wait_rows(s):
        pltpu.make_async_copy(x1_ref.at[pl.ds(0, EXPERT_BLOCK * ROW_TILES), :], xbuf[s],
                              row_sem.at[s]).wait()

    @pl.when(i == 0)
    def _():
        for cp in weight_copies(be_ref[0], 0):
            cp.start(priority=1)
        idx_copy(0, 0).start()
        idx_copy(1, 1).start()
        idx_copy(0, 0).wait()
        idx_copy(1, 1).wait()
        start_rows(0, 0)
        start_rows(1, 1)
        idx_copy(2, 2).start()

    prev = be_ref[jnp.maximum(i - 1, 0)]

    @pl.when((i < n_used) & ((i == 0) | (be_ref[i] != prev)))
    def _():
        par = wpar_ref[i]
        for cp in weight_copies(be_ref[i], par):
            cp.wait()

        @pl.when(wnext_ref[i] >= 0)
        def _():
            for cp in weight_copies(wnext_ref[i], 1 - par):
                cp.start(priority=1)

        wg_b[...] = wg_f[par].astype(BF16)
        wu_b[...] = wu_f[par].astype(BF16)
        wd_b[...] = wd_f[par].astype(BF16)

    for slot in range(3):
        ahead = (slot + 2) % 3

        @pl.when((i < n_used) & (i >= 3) & (lax.rem(i, 3) == slot))
        def _():
            out_copy(i - 3, slot).wait()

        @pl.when((i < n_used) & (lax.rem(i, 3) == slot))
        def _():
            idx_copy(i + 2, ahead).wait()
            wait_rows(slot)
            start_rows(i + 2, ahead)
            idx_copy(i + 3, slot).start()
            xb = _load_token_tiles(xbuf[slot], EXPERT_BLOCK).astype(BF16)
            hb = _silu(_dot(xb, wg_b[...])) * _dot(xb, wu_b[...])
            _store_token_tiles(ybuf[slot], _dot(hb.astype(BF16), wd_b[...]))
            out_copy(i, slot).start()

        @pl.when((i == n_used) & (lax.rem(i, 3) == slot))
        def _():
            wait_rows(slot)
            wait_rows((slot + 1) % 3)
            idx_copy(i + 2, ahead).wait()
            for back in range(1, 4):
                @pl.when(i >= back)
                def _():
                    out_copy(i - back, (slot - back) % 3).wait()
            zbuf[...] = jnp.zeros_like(zbuf)

    @pl.when(i >= n_used)
    def _():
        zero_copy(i).start()

    @pl.when(i == n_steps - 1)
    def _():
        def drain(_, carry):
            zero_copy(0).wait()
            return carry
        lax.fori_loop(0, n_steps - n_used, drain, 0)


def _experts(block_e, n_used, block_c0, w_par, w_next, stok, x1, w_gate_e, w_up_e, w_down_e):
    nb = block_e.shape[0]
    grid_spec = pltpu.PrefetchScalarGridSpec(
        num_scalar_prefetch=5,
        grid=(nb,),
        in_specs=[pl.BlockSpec(memory_space=pl.ANY)] * 5,
        out_specs=pl.BlockSpec(memory_space=pl.ANY),
        scratch_shapes=(
            [pltpu.SMEM((IDX_WINDOW,), jnp.int32)] * 3
            + [pltpu.VMEM((EXPERT_BLOCK * ROW_TILES, LANES), F32)] * 7
        ) + [
            pltpu.SemaphoreType.DMA((3,)),
            pltpu.SemaphoreType.DMA((3,)),
            pltpu.SemaphoreType.DMA((3,)),
            pltpu.SemaphoreType.DMA(()),
            pltpu.VMEM((2, D_MODEL, EXPERT_FF), F32),
            pltpu.VMEM((2, D_MODEL, EXPERT_FF), F32),
            pltpu.VMEM((2, EXPERT_FF, D_MODEL), F32),
            pltpu.SemaphoreType.DMA((2,)),
            pltpu.VMEM((D_MODEL, EXPERT_FF), BF16),
            pltpu.VMEM((D_MODEL, EXPERT_FF), BF16),
            pltpu.VMEM((EXPERT_FF, D_MODEL), BF16),
        ],
    )
    return pl.pallas_call(
        _experts_kernel,
        grid_spec=grid_spec,
        out_shape=jax.ShapeDtypeStruct((nb * EXPERT_BLOCK * ROW_TILES, LANES), F32),
        compiler_params=pltpu.CompilerParams(
            dimension_semantics=("arbitrary",), vmem_limit_bytes=VMEM_LIMIT),
        name="experts",
    )(block_e, n_used, block_c0, w_par, w_next, stok, x1, w_gate_e, w_up_e, w_down_e)


def _final_kernel(np_tiles, dest_ref, dest_next_ref, x1_ref, topw_ref, yb_ref, wsg_ref, wsu_ref,
                  wsd_ref, g2_ref, b2_ref, yp_ref, ys_ref, ybuf, sem):
    i = pl.program_id(0)
    slot = lax.rem(i, 2)

    def issue_rows(d_ref, s):
        def body(t, carry):
            dst = pl.multiple_of(t * ROW_TILES, ROW_TILES)
            for j in range(TOP_K):
                src = pl.multiple_of(d_ref[j, t], ROW_TILES)
                pltpu.make_async_copy(yb_ref.at[pl.ds(src, ROW_TILES), :],
                                      ybuf.at[s, j, pl.ds(dst, ROW_TILES), :],
                                      sem.at[s]).start(priority=j % 2)
            return carry
        lax.fori_loop(0, MIX_TILE, body, 0, unroll=4)

    @pl.when(i == 0)
    def _():
        issue_rows(dest_ref, 0)

    @pl.when(i + 1 < pl.num_programs(0))
    def _():
        issue_rows(dest_next_ref, 1 - slot)

    x1 = _load_token_tiles(x1_ref, MIX_TILE)
    xb = x1.astype(BF16)
    hs = _silu(_dot(xb, wsg_ref[...])) * _dot(xb, wsu_ref[...])
    ff = _dot(hs.astype(BF16), wsd_ref[...])
    r = lax.broadcasted_iota(jnp.int32, (MIX_TILE, MIX_TILE), 0)
    c = lax.broadcasted_iota(jnp.int32, (MIX_TILE, MIX_TILE), 1)
    w_cols = _dot_nt((r == c).astype(F32), topw_ref[...], HIGHEST)
    for j in range(TOP_K):
        pltpu.make_async_copy(yb_ref.at[pl.ds(0, MIX_TILE * ROW_TILES), :], ybuf.at[slot, j],
                              sem.at[slot]).wait()
    for j in range(TOP_K):
        ff = ff + w_cols[:, j:j + 1] * _load_token_tiles(ybuf.at[slot, j], MIX_TILE)
    y = _layer_norm(ALPHA * x1 + ff, g2_ref[...], b2_ref[...])

    @pl.when(i < np_tiles)
    def _():
        yp_ref[...] = y

    @pl.when(i >= np_tiles)
    def _():
        ys_ref[...] = y


def _final(n_p, dest, x1, top_w_t, yb, wsg, wsu, wsd, g2, b2):
    nt = x1.shape[0] // ROW_TILES
    np_tiles = n_p // MIX_TILE
    row = lambda i: (i, 0)
    const = lambda i: (0, 0)
    n_tiles = nt // MIX_TILE
    return pl.pallas_call(
        functools.partial(_final_kernel, np_tiles),
        grid=(n_tiles,),
        in_specs=[
            pl.BlockSpec((TOP_K, MIX_TILE), lambda i: (0, i), memory_space=pltpu.SMEM),
            pl.BlockSpec((TOP_K, MIX_TILE), lambda i: (0, jnp.minimum(i + 1, n_tiles - 1)),
                         memory_space=pltpu.SMEM),
            pl.BlockSpec((MIX_TILE * ROW_TILES, LANES), row),
            pl.BlockSpec((TOP_K, MIX_TILE), lambda i: (0, i)),
            pl.BlockSpec(memory_space=pl.ANY),
            pl.BlockSpec((D_MODEL, EXPERT_FF), const),
            pl.BlockSpec((D_MODEL, EXPERT_FF), const),
            pl.BlockSpec((EXPERT_FF, D_MODEL), const),
            pl.BlockSpec((1, D_MODEL), const),
            pl.BlockSpec((1, D_MODEL), const),
        ],
        out_specs=(
            pl.BlockSpec((MIX_TILE, D_MODEL), lambda i: (jnp.minimum(i, np_tiles - 1), 0)),
            pl.BlockSpec((MIX_TILE, D_MODEL), lambda i: (jnp.maximum(i - np_tiles, 0), 0)),
        ),
        out_shape=(
            jax.ShapeDtypeStruct((n_p, D_MODEL), F32),
            jax.ShapeDtypeStruct((nt - n_p, D_MODEL), F32),
        ),
        scratch_shapes=[
            pltpu.VMEM((2, TOP_K, MIX_TILE * ROW_TILES, LANES), F32),
            pltpu.SemaphoreType.DMA((2,)),
        ],
        compiler_params=pltpu.CompilerParams(
            dimension_semantics=("arbitrary",), vmem_limit_bytes=VMEM_LIMIT),
        name="final",
    )(dest, dest, x1, top_w_t, yb, wsg, wsu, wsd, g2, b2)


def _expert_layout(counts, nk):
    nb = -(-nk // EXPERT_BLOCK) + N_EXPERTS
    padded = (counts + EXPERT_BLOCK - 1) // EXPERT_BLOCK * EXPERT_BLOCK
    pend = jnp.cumsum(padded)
    pstart = (pend - padded).astype(jnp.int32)
    block_start = jnp.arange(nb, dtype=jnp.int32) * EXPERT_BLOCK
    block_e = jnp.sum((pend[None, :] <= block_start[:, None]).astype(jnp.int32), 1)
    block_e = jnp.minimum(block_e, N_EXPERTS - 1)
    n_used = (pend[-1] // EXPERT_BLOCK).astype(jnp.int32).reshape(1)
    start = (jnp.cumsum(counts) - counts).astype(jnp.int32)
    block_c0 = start[block_e] + block_start - pstart[block_e]
    block_c0 = jnp.where(block_start < pend[-1], block_c0, 0)
    block_c0 = jnp.concatenate([block_c0, jnp.zeros((2,), jnp.int32)]).astype(jnp.int32)
    ids = jnp.arange(nb, dtype=jnp.int32)
    used = block_start < pend[-1]
    switch = used & ((ids == 0) | (block_e != jnp.roll(block_e, 1)))
    w_par = ((jnp.cumsum(switch.astype(jnp.int32)) - 1) % 2).astype(jnp.int32)
    later_switch = (ids[None, :] > ids[:, None]) & switch[None, :]
    next_pos = jnp.min(jnp.where(later_switch, ids[None, :], nb), axis=1)
    w_next = jnp.where(next_pos < nb, block_e[jnp.minimum(next_pos, nb - 1)], -1).astype(jnp.int32)
    return pstart, block_e, n_used, block_c0, w_par, w_next


def _pad_lanes(v, offset=0):
    out = jnp.zeros((1, LANES), F32)
    return out.at[0, offset:offset + v.shape[0]].set(v.astype(F32))


def kernel(x_prompt, x_sample, state_conv, state_gdn, cache_swa_k, cache_swa_v, w_in, conv_w, a_log, dt_bias, gdn_norm_w, attn_sinks, w_o, ln1_g, ln1_b, w_router, router_bias, w_gate_e, w_up_e, w_down_e, w_shared_gate, w_shared_up, w_shared_down, ln2_g, ln2_b):
    bp, tp, d = x_prompt.shape
    bs, ts, _ = x_sample.shape
    n_p, n_s = bp * tp, bs * ts

    wi = w_in[0]
    o1 = CONV_DIM
    o2 = o1 + GDN_WIDTH
    o4 = o2 + 2 * GDN_HEADS
    o5 = o4 + SWA_WIDTH
    o6 = o5 + SWA_KV_WIDTH
    ba_cols = jnp.zeros((d, LANES), F32).at[:, :2 * GDN_HEADS].set(wi[:, o2:o4])
    w_all = jnp.concatenate([wi[:, :o2], wi[:, o4:o5], wi[:, o5:o6], wi[:, o6:], ba_cols], 1).astype(BF16)
    wo = w_o[0].astype(BF16)
    wrt = w_router[0].T.astype(BF16)
    rbias = router_bias[0].astype(F32).reshape(N_EXPERTS, 1)
    alog_pad = _pad_lanes(a_log[0], GDN_HEADS)
    dtb_pad = _pad_lanes(dt_bias[0], GDN_HEADS)
    sinks_pad = _pad_lanes(attn_sinks[0])
    norm_w = gdn_norm_w[0].reshape(1, GDN_D)
    g1, b1 = ln1_g[0].reshape(1, d), ln1_b[0].reshape(1, d)
    g2, b2 = ln2_g[0].reshape(1, d), ln2_b[0].reshape(1, d)

    def front(x, pos0, L, conv_hist, s0, k_hist, v_hist):
        b, t, _ = x.shape
        period = max(t, IN_TILE)
        pos = pos0 + (jnp.arange(period, dtype=jnp.int32) % t).astype(F32)
        cos, s1, s2 = _rope_tables(pos)
        x2d = x.reshape(b * t, d)
        qkv, z, sq, sk, sv, ba = _in_proj(x2d, w_all, cos, s1, s2)
        qkv = qkv.reshape(b, t, CONV_DIM)
        hist8 = jnp.concatenate(
            [jnp.zeros((b, HIST_ROWS - (CONV_WIDTH - 1), CONV_DIM), F32), conv_hist], 1)
        gd, s_new = _gdn(L, qkv, z.reshape(b, t, GDN_WIDTH), ba.reshape(b, t, LANES), hist8, s0,
                         conv_w[0], alog_pad, dtb_pad, norm_w)
        k_ext = jnp.concatenate([k_hist.reshape(b, WINDOW, SWA_KV_WIDTH),
                                 sk.reshape(b, t, SWA_KV_WIDTH)], 1)
        v_ext = jnp.concatenate([v_hist.reshape(b, WINDOW, SWA_KV_WIDTH),
                                 sv.reshape(b, t, SWA_KV_WIDTH)], 1)
        sw = _swa(L, pos0, sq.reshape(b, t, SWA_WIDTH), k_ext, v_ext, sinks_pad)
        conv_new = jnp.concatenate([conv_hist, qkv], 1)[:, -(CONV_WIDTH - 1):]
        k_new = k_ext[:, -WINDOW:].reshape(b, WINDOW, SWA_KV_HEADS, SWA_HEAD_DIM)
        v_new = v_ext[:, -WINDOW:].reshape(b, WINDOW, SWA_KV_HEADS, SWA_HEAD_DIM)
        return (x2d, gd.reshape(b * t, GDN_WIDTH), sw.reshape(b * t, SWA_WIDTH),
                conv_new, s_new, k_new, v_new)

    zeros = lambda *s: jnp.zeros(s, F32)
    xp2, gd_p, sw_p, conv_p, gdn_p, k_p, v_p = front(
        x_prompt, 0, CHUNK, zeros(bp, CONV_WIDTH - 1, CONV_DIM),
        zeros(bp, GDN_HEADS, GDN_D, GDN_D), zeros(bp, WINDOW, SWA_KV_WIDTH),
        zeros(bp, WINDOW, SWA_KV_WIDTH))
    xs2, gd_s, sw_s, conv_s, gdn_s, k_s, v_s = front(
        x_sample, PAST_LEN, ts, state_conv[0], state_gdn[0], cache_swa_k[0], cache_swa_v[0])

    x1, top_e_t, top_w_t, rank_t, counts = _mix(
        gd_p, sw_p, xp2, gd_s, sw_s, xs2, wo, g1, b1, wrt, rbias)

    pstart, block_e, n_used, block_c0, w_par, w_next = _expert_layout(
        counts[:, 0], TOP_K * (n_p + n_s))
    dest = _dest(top_e_t, rank_t, pstart.reshape(N_EXPERTS, 1))
    order = jnp.argsort(top_e_t.T.reshape(-1))
    stok = jnp.concatenate([(order // TOP_K * ROW_TILES).astype(jnp.int32),
                            jnp.zeros((IDX_WINDOW,), jnp.int32)])
    yb = _experts(block_e, n_used, block_c0, w_par, w_next, stok, x1,
                  w_gate_e[0], w_up_e[0], w_down_e[0])

    y_p, y_s = _final(n_p, dest, x1, top_w_t, yb,
                      w_shared_gate[0].astype(BF16), w_shared_up[0].astype(BF16),
                      w_shared_down[0].astype(BF16), g2, b2)
    return (y_p.reshape(bp, tp, d), y_s.reshape(bs, ts, d),
            conv_p[None], gdn_p[None], k_p[None], v_p[None],
            conv_s[None], gdn_s[None], k_s[None], v_s[None])
```

```python
import functools
import math

import jax
import jax.numpy as jnp
from jax import lax
from jax.experimental import pallas as pl
from jax.experimental.pallas import tpu as pltpu

F32 = jnp.float32
BF16 = jnp.bfloat16
HIGHEST = lax.Precision.HIGHEST

D_MODEL = 1024
CHUNK = 64
GDN_HEADS = 4
GDN_D = 128
GDN_WIDTH = GDN_HEADS * GDN_D
CONV_DIM = 3 * GDN_WIDTH
CONV_WIDTH = 4
SWA_HEAD_DIM = 64
SWA_Q_HEADS = 8
SWA_KV_HEADS = 2
SWA_GROUP = SWA_Q_HEADS // SWA_KV_HEADS
SWA_WIDTH = SWA_Q_HEADS * SWA_HEAD_DIM
SWA_KV_WIDTH = SWA_KV_HEADS * SWA_HEAD_DIM
WINDOW = 128
ROPE_THETA = 500000.0
ROT_DIM = SWA_HEAD_DIM // 4
ROT_HALF = ROT_DIM // 2
N_EXPERTS = 256
TOP_K = 8
N_GROUPS = 8
GROUP_SIZE = N_EXPERTS // N_GROUPS
TOPK_GROUPS = 4
EXPERT_FF = 256
ROUTED_SCALE = 2.5
ALPHA = 2.0 ** 0.25
LN_EPS = 1e-5
NORM_EPS = 1e-6
PAST_LEN = 4096

LANES = 128
HIST_ROWS = 8
ROW_TILES = D_MODEL // LANES

C_QKV = 0
C_Z = C_QKV + CONV_DIM
C_SQ = C_Z + GDN_WIDTH
C_SK = C_SQ + SWA_WIDTH
C_SV = C_SK + SWA_KV_WIDTH
C_BA = C_SV + SWA_KV_WIDTH
C_END = C_BA + LANES

IN_TILE = 512
MIX_TILE = 256
DEST_TILE = 512
EXPERT_BLOCK = 256
IDX_WINDOW = EXPERT_BLOCK + LANES
GDN_BATCH = 8
SWA_CHUNKS = 8
VMEM_LIMIT = 56 * 1024 * 1024


def _dot(a, b, precision=None):
    return jnp.dot(a, b, preferred_element_type=F32, precision=precision)


def _dot_nt(a, b, precision=None):
    return lax.dot_general(a, b, (((1,), (1,)), ((), ())),
                           preferred_element_type=F32, precision=precision)


def _dot_tn(a, b):
    return lax.dot_general(a, b, (((0,), (0,)), ((), ())), preferred_element_type=F32)


def _split(x):
    hi = x.astype(BF16)
    return hi, (x - hi.astype(F32)).astype(BF16)


def _dot_split(a, b):
    return _dot(a[0], b[0]) + (_dot(a[1], b[0]) + _dot(a[0], b[1]))


def _silu(x):
    return x * jax.nn.sigmoid(x)


def _store_token_tiles(ref, x):
    m = x.shape[0]
    for s in range(ROW_TILES):
        ref[pl.ds(s, m, stride=ROW_TILES), :] = x[:, s * LANES:(s + 1) * LANES]


def _load_token_tiles(ref, m):
    return jnp.concatenate([ref[pl.ds(s, m, stride=ROW_TILES), :] for s in range(ROW_TILES)], 1)


def _in_proj_kernel(x_ref, w_ref, cos_ref, s1_ref, s2_ref,
                    qkv_ref, z_ref, sq_ref, sk_ref, sv_ref, ba_ref):
    xb = x_ref[...].astype(BF16)
    qkv_ref[...] = _dot(xb, w_ref[:, C_QKV:C_Z])
    z_ref[...] = _dot(xb, w_ref[:, C_Z:C_SQ])
    sv_ref[...] = _dot(xb, w_ref[:, C_SV:C_BA])
    ba_ref[...] = _dot(xb, w_ref[:, C_BA:C_END])
    cos = cos_ref[...]
    s1 = s1_ref[...]
    s2 = s2_ref[...]

    def rope(t):
        return (t * cos + pltpu.roll(t, LANES - ROT_HALF, 1) * s1
                + pltpu.roll(t, ROT_HALF, 1) * s2)

    sk_ref[...] = rope(_dot(xb, w_ref[:, C_SK:C_SV]))
    for g in range(SWA_WIDTH // LANES):
        lo = C_SQ + g * LANES
        sq_ref[:, g * LANES:(g + 1) * LANES] = rope(_dot(xb, w_ref[:, lo:lo + LANES])).astype(BF16)


def _in_proj(x2d, w_all, cos, s1, s2):
    n = x2d.shape[0]
    period_tiles = cos.shape[0] // IN_TILE
    row = lambda i: (i, 0)
    tab = lambda i: (i % period_tiles, 0)
    out_shapes = (
        jax.ShapeDtypeStruct((n, CONV_DIM), F32),
        jax.ShapeDtypeStruct((n, GDN_WIDTH), F32),
        jax.ShapeDtypeStruct((n, SWA_WIDTH), BF16),
        jax.ShapeDtypeStruct((n, SWA_KV_WIDTH), F32),
        jax.ShapeDtypeStruct((n, SWA_KV_WIDTH), F32),
        jax.ShapeDtypeStruct((n, LANES), F32),
    )
    return pl.pallas_call(
        _in_proj_kernel,
        grid=(n // IN_TILE,),
        in_specs=[
            pl.BlockSpec((IN_TILE, D_MODEL), row),
            pl.BlockSpec((D_MODEL, C_END), lambda i: (0, 0)),
            pl.BlockSpec((IN_TILE, LANES), tab),
            pl.BlockSpec((IN_TILE, LANES), tab),
            pl.BlockSpec((IN_TILE, LANES), tab),
        ],
        out_specs=(
            pl.BlockSpec((IN_TILE, CONV_DIM), row),
            pl.BlockSpec((IN_TILE, GDN_WIDTH), row),
            pl.BlockSpec((IN_TILE, SWA_WIDTH), row),
            pl.BlockSpec((IN_TILE, SWA_KV_WIDTH), row),
            pl.BlockSpec((IN_TILE, SWA_KV_WIDTH), row),
            pl.BlockSpec((IN_TILE, LANES), row),
        ),
        out_shape=out_shapes,
        compiler_params=pltpu.CompilerParams(
            dimension_semantics=("arbitrary",), vmem_limit_bytes=VMEM_LIMIT),
        name="in_proj",
    )(x2d, w_all, cos, s1, s2)


def _rope_tables(pos):
    p = pos.shape[0]
    inv = ROPE_THETA ** (-jnp.arange(0, ROT_DIM, 2, dtype=F32) / ROT_DIM)
    ang = pos[:, None] * inv[None, :]
    cos = jnp.cos(ang)
    sin = jnp.sin(ang)
    rest = SWA_HEAD_DIM - ROT_DIM
    head_c = jnp.concatenate([cos, cos, jnp.ones((p, rest), F32)], 1)
    head_s1 = jnp.concatenate([-sin, jnp.zeros((p, SWA_HEAD_DIM - ROT_HALF), F32)], 1)
    head_s2 = jnp.concatenate([jnp.zeros((p, ROT_HALF), F32), sin, jnp.zeros((p, rest), F32)], 1)
    two = lambda t: jnp.concatenate([t, t], 1)
    return two(head_c), two(head_s1), two(head_s2)


def _gdn_kernel(L, qkv_ref, z_ref, ba_ref, hist_ref, s0_ref, convw_ref, alog_ref, dtb_ref,
                normw_ref, o_ref, sout_ref, xbuf, state):
    c = pl.program_id(1)
    nbat = qkv_ref.shape[0]
    chains = [(b, h) for b in range(nbat) for h in range(GDN_HEADS)]

    @pl.when(c == 0)
    def _():
        xbuf[:, 0:HIST_ROWS, :] = hist_ref[...]
        state[...] = s0_ref[...]

    xbuf[:, HIST_ROWS:HIST_ROWS + L, :] = qkv_ref[...]

    row = lax.broadcasted_iota(jnp.int32, (L, L), 0)
    col = lax.broadcasted_iota(jnp.int32, (L, L), 1)
    incl = row >= col
    strict = row > col
    lower_ones = incl.astype(BF16)
    eye = (row == col).astype(F32)

    beta_all, g_all = [], []
    for b in range(nbat):
        ba = ba_ref[b]
        beta_all.append(jax.nn.sigmoid(ba))
        sp_in = ba + dtb_ref[...]
        softplus = jnp.maximum(sp_in, 0.0) + jnp.log1p(jnp.exp(-jnp.abs(sp_in)))
        g_all.append(-jnp.exp(alog_ref[...]) * softplus)

    def conv_silu(b, c0):
        acc = None
        for j in range(CONV_WIDTH):
            lo = HIST_ROWS - (CONV_WIDTH - 1) + j
            t = xbuf[b, lo:lo + L, c0:c0 + GDN_D] * convw_ref[j:j + 1, c0:c0 + GDN_D]
            acc = t if acc is None else acc + t
        return _silu(acc)

    def l2n(t):
        return t * lax.rsqrt(jnp.sum(t * t, -1, keepdims=True) + NORM_EPS)

    q = [l2n(conv_silu(b, h * GDN_D)) * (GDN_D ** -0.5) for b, h in chains]
    k = [l2n(conv_silu(b, GDN_WIDTH + h * GDN_D)) for b, h in chains]
    v = [conv_silu(b, 2 * GDN_WIDTH + h * GDN_D) for b, h in chains]
    beta = [beta_all[b][:, h:h + 1] for b, h in chains]
    g = [g_all[b][:, GDN_HEADS + h:GDN_HEADS + h + 1] for b, h in chains]

    def pieces(t):
        p1 = t.astype(BF16).astype(F32)
        r1 = t - p1
        p2 = r1.astype(BF16).astype(F32)
        return p1, p2, r1 - p2

    g_parts = [pieces(t) for t in g]
    diff = [sum(_dot(lower_ones, jnp.where(strict, p, 0.0).astype(BF16)) for p in gp)
            for gp in g_parts]
    gcum3 = [_dot(lower_ones, jnp.concatenate(
        [jnp.broadcast_to(p, (L, GDN_D)) for p in gp], 1).astype(BF16)) for gp in g_parts]
    gcum = [t[:, :GDN_D] + t[:, GDN_D:2 * GDN_D] + t[:, 2 * GDN_D:] for t in gcum3]
    dec = [jnp.where(incl, jnp.exp(jnp.where(incl, d, 0.0)), 0.0) for d in diff]
    exp_g = [jnp.exp(t) for t in gcum]
    g_end = [t[L - 1:L, :] for t in gcum]
    kb = [t.astype(BF16) for t in k]
    kk = [_dot_nt(t, t) for t in kb]
    aqk = [_dot_nt(qq.astype(BF16), t) * d for qq, t, d in zip(q, kb, dec)]
    a = [jnp.where(strict, d * t, 0.0) * bt for d, t, bt in zip(dec, kk, beta)]

    n = [-t for t in a]
    n_s = [_split(t) for t in n]
    z = [jnp.concatenate([eye + t, _dot_split(ts, ts)], 1) for t, ts in zip(n, n_s)]
    keep_s = lax.broadcasted_iota(jnp.int32, (L, 2 * L), 1) < L
    for _ in range(int(math.log2(L)) - 2):
        pz = [_dot_split(_split(t[:, L:]), _split(t)) for t in z]
        z = [jnp.where(keep_s, t, 0.0) + d for t, d in zip(z, pz)]
    inv = [t[:, :L] + _dot_split(_split(t[:, L:]), _split(t[:, :L])) for t in z]
    inv_s = [_split(t) for t in inv]
    wy = [_dot_split(t, _split(jnp.concatenate([bt * vv, (bt * eg) * kk_], 1)))
          for t, bt, vv, eg, kk_ in zip(inv_s, beta, v, exp_g, k)]
    w = [t[:, :GDN_D] for t in wy]
    y = [t[:, GDN_D:] for t in wy]

    s = [state[b, h] for b, h in chains]
    sb = [t.astype(BF16) for t in s]
    yq = [_dot(jnp.concatenate([yy.astype(BF16), (eg * qq).astype(BF16)], 0), t)
          for yy, eg, qq, t in zip(y, exp_g, q, sb)]
    u = [ww - t[:L] for ww, t in zip(w, yq)]
    ub = [t.astype(BF16) for t in u]
    o = [t[L:] + _dot(aa.astype(BF16), uu) for t, aa, uu in zip(yq, aqk, ub)]
    k_dec = [(jnp.exp(ge - gc) * kk_).astype(BF16) for ge, gc, kk_ in zip(g_end, gcum, k)]
    s_new = [jnp.exp(ge) * t + _dot_tn(kd, uu) for ge, t, kd, uu in zip(g_end, s, k_dec, ub)]

    for (b, h), t, oo in zip(chains, s_new, o):
        state[b, h] = t
        on = oo * lax.rsqrt(jnp.mean(oo * oo, -1, keepdims=True) + NORM_EPS)
        zz = z_ref[b, :, h * GDN_D:(h + 1) * GDN_D]
        o_ref[b, :, h * GDN_D:(h + 1) * GDN_D] = (on * normw_ref[...] * _silu(zz)).astype(BF16)

    xbuf[:, 0:HIST_ROWS, :] = xbuf[:, L:L + HIST_ROWS, :]

    @pl.when(c == pl.num_programs(1) - 1)
    def _():
        sout_ref[...] = state[...]


def _gdn(L, qkv, z, ba, hist8, s0, conv_w, alog_pad, dtb_pad, norm_w):
    b, t, _ = qkv.shape
    nc = t // L
    nbat = GDN_BATCH
    tok = lambda i, c: (i, c, 0)
    per_b3 = lambda i, c: (i, 0, 0)
    per_b4 = lambda i, c: (i, 0, 0, 0)
    const2 = lambda i, c: (0, 0)
    return pl.pallas_call(
        functools.partial(_gdn_kernel, L),
        grid=(b // nbat, nc),
        in_specs=[
            pl.BlockSpec((nbat, L, CONV_DIM), tok),
            pl.BlockSpec((nbat, L, GDN_WIDTH), tok),
            pl.BlockSpec((nbat, L, LANES), tok),
            pl.BlockSpec((nbat, HIST_ROWS, CONV_DIM), per_b3),
            pl.BlockSpec((nbat, GDN_HEADS, GDN_D, GDN_D), per_b4),
            pl.BlockSpec((CONV_WIDTH, CONV_DIM), const2),
            pl.BlockSpec((1, LANES), const2),
            pl.BlockSpec((1, LANES), const2),
            pl.BlockSpec((1, GDN_D), const2),
        ],
        out_specs=(
            pl.BlockSpec((nbat, L, GDN_WIDTH), tok),
            pl.BlockSpec((nbat, GDN_HEADS, GDN_D, GDN_D), per_b4),
        ),
        out_shape=(
            jax.ShapeDtypeStruct((b, t, GDN_WIDTH), BF16),
            jax.ShapeDtypeStruct((b, GDN_HEADS, GDN_D, GDN_D), F32),
        ),
        scratch_shapes=[
            pltpu.VMEM((nbat, HIST_ROWS + L, CONV_DIM), F32),
            pltpu.VMEM((nbat, GDN_HEADS, GDN_D, GDN_D), F32),
        ],
        compiler_params=pltpu.CompilerParams(
            dimension_semantics=("arbitrary", "arbitrary"), vmem_limit_bytes=VMEM_LIMIT),
        name="gdn",
    )(qkv, z, ba, hist8, s0, conv_w, alog_pad, dtb_pad, norm_w)


def _swa_kernel(L, pos0, q_ref, k_ref, v_ref, sink_ref, o_ref):
    c = pl.program_id(1)
    nch = q_ref.shape[1] // L
    span = WINDOW + L
    rows = WINDOW + nch * L
    start = pl.multiple_of(c * (nch * L), L)
    kx = k_ref[0, pl.ds(start, rows), :]
    vx = v_ref[0, pl.ds(start, rows), :]
    kx_sw = pltpu.roll(kx, SWA_HEAD_DIM, 1)
    vx_sw = pltpu.roll(vx, SWA_HEAD_DIM, 1)
    low = lax.broadcasted_iota(jnp.int32, (rows, LANES), 1) < SWA_HEAD_DIM

    def halves(x, x_sw, kh):
        src_lo, src_hi = (x, x_sw) if kh == 0 else (x_sw, x)
        return (jnp.where(low, src_lo, 0.0).astype(BF16), jnp.where(low, 0.0, src_hi).astype(BF16))

    k_halves = [halves(kx, kx_sw, kh) for kh in range(SWA_KV_HEADS)]
    v_halves = [halves(vx, vx_sw, kh) for kh in range(SWA_KV_HEADS)]
    sinks = sink_ref[...]
    top_rows = lax.broadcasted_iota(jnp.int32, (2 * L, 1), 0) < L
    first_pos = pos0 - WINDOW + c * (nch * L)

    chains = [(u, kh, half) for u in range(nch) for kh in range(SWA_KV_HEADS) for half in range(2)]
    valid = [first_pos + u * L + lax.broadcasted_iota(jnp.int32, (1, span), 1) >= 0
             for u in range(nch)]
    q4 = {}
    for u in range(nch):
        for kh in range(SWA_KV_HEADS):
            c0 = kh * 2 * LANES
            q4[u, kh] = jnp.concatenate([q_ref[0, u * L:(u + 1) * L, c0:c0 + LANES],
                                         q_ref[0, u * L:(u + 1) * L, c0 + LANES:c0 + 2 * LANES]], 0)
    scores = [jnp.where(valid[u],
                        _dot_nt(q4[u, kh], k_halves[kh][half][u * L:u * L + span])
                        * (SWA_HEAD_DIM ** -0.5), -jnp.inf)
              for u, kh, half in chains]
    sink_cols = [jnp.where(top_rows, sinks[:, kh * SWA_GROUP + half:kh * SWA_GROUP + half + 1],
                           sinks[:, kh * SWA_GROUP + half + 2:kh * SWA_GROUP + half + 3])
                 for _, kh, half in chains]
    m = [jnp.maximum(jnp.max(s, -1, keepdims=True), sk) for s, sk in zip(scores, sink_cols)]
    p = [jnp.exp(s - mm) for s, mm in zip(scores, m)]
    den = [jnp.sum(pp, -1, keepdims=True) + jnp.exp(sk - mm) for pp, sk, mm in zip(p, sink_cols, m)]
    pv = [_dot(pp.astype(BF16), v_halves[kh][half][u * L:u * L + span]) / dd
          for pp, dd, (u, kh, half) in zip(p, den, chains)]
    for n in range(0, len(chains), 2):
        u, kh, _ = chains[n]
        o = pv[n] + pv[n + 1]
        c0 = kh * 2 * LANES
        o_ref[0, u * L:(u + 1) * L, c0:c0 + LANES] = o[0:L].astype(BF16)
        o_ref[0, u * L:(u + 1) * L, c0 + LANES:c0 + 2 * LANES] = o[L:2 * L].astype(BF16)


def _swa(L, pos0, q, k_ext, v_ext, sinks_pad):
    b, t, _ = q.shape
    ext = k_ext.shape[1]
    nch = SWA_CHUNKS if (t // L) % SWA_CHUNKS == 0 else 1
    return pl.pallas_call(
        functools.partial(_swa_kernel, L, pos0),
        grid=(b, t // (nch * L)),
        in_specs=[
            pl.BlockSpec((1, nch * L, SWA_WIDTH), lambda i, c: (i, c, 0)),
            pl.BlockSpec((1, ext, SWA_KV_WIDTH), lambda i, c: (i, 0, 0)),
            pl.BlockSpec((1, ext, SWA_KV_WIDTH), lambda i, c: (i, 0, 0)),
            pl.BlockSpec((1, LANES), lambda i, c: (0, 0)),
        ],
        out_specs=pl.BlockSpec((1, nch * L, SWA_WIDTH), lambda i, c: (i, c, 0)),
        out_shape=jax.ShapeDtypeStruct((b, t, SWA_WIDTH), BF16),
        compiler_params=pltpu.CompilerParams(
            dimension_semantics=("arbitrary", "arbitrary"), vmem_limit_bytes=VMEM_LIMIT),
        name="swa",
    )(q, k_ext, v_ext, sinks_pad)


def _layer_norm(r, g, b):
    mu = jnp.mean(r, -1, keepdims=True)
    d = r - mu
    var = jnp.mean(d * d, -1, keepdims=True)
    return d * lax.rsqrt(var + LN_EPS) * g + b


def _route(x1b, wrt_ref, rbias_ref, count_ref):
    tm = x1b.shape[0]
    scores = jax.nn.sigmoid(_dot_nt(wrt_ref[...], x1b))
    biased = scores + rbias_ref[...]
    neg = -jnp.inf
    r_g = lax.broadcasted_iota(jnp.int32, (GROUP_SIZE, tm), 0)
    blocks = []
    group_score = []
    for g in range(N_GROUPS):
        blk = biased[g * GROUP_SIZE:(g + 1) * GROUP_SIZE]
        m1 = jnp.max(blk, 0, keepdims=True)
        i1 = jnp.min(jnp.where(blk == m1, r_g, GROUP_SIZE), 0, keepdims=True)
        m2 = jnp.max(jnp.where(r_g == i1, neg, blk), 0, keepdims=True)
        blocks.append(blk)
        group_score.append(m1 + m2)
    masked = []
    for g in range(N_GROUPS):
        rank = jnp.zeros((1, tm), jnp.int32)
        for o in range(N_GROUPS):
            if o == g:
                continue
            ahead = group_score[o] > group_score[g]
            if o < g:
                ahead = ahead | (group_score[o] == group_score[g])
            rank = rank + ahead.astype(jnp.int32)
        masked.append(jnp.where(rank < TOPK_GROUPS, blocks[g], neg))
    masked = jnp.concatenate(masked, 0)
    r_e = lax.broadcasted_iota(jnp.int32, (N_EXPERTS, tm), 0)
    idx_rows = []
    w_rows = []
    hits = []
    for _ in range(TOP_K):
        m = jnp.max(masked, 0, keepdims=True)
        idx = jnp.min(jnp.where(masked == m, r_e, N_EXPERTS), 0, keepdims=True)
        hit = r_e == idx
        w_rows.append(jnp.sum(jnp.where(hit, scores, 0.0), 0, keepdims=True))
        masked = jnp.where(hit, neg, masked)
        idx_rows.append(idx)
        hits.append(hit)
    total = w_rows[0]
    for wr in w_rows[1:]:
        total = total + wr
    top_w = jnp.concatenate(w_rows, 0) / total * ROUTED_SCALE

    chosen = jnp.zeros((N_EXPERTS, tm), F32)
    for hit in hits:
        chosen = chosen + hit.astype(F32)
    t_r = lax.broadcasted_iota(jnp.int32, (tm, tm), 0)
    t_c = lax.broadcasted_iota(jnp.int32, (tm, tm), 1)
    before = (t_r < t_c).astype(BF16)
    prior = _dot(chosen.astype(BF16), before) + count_ref[...]
    rank_rows = [jnp.sum(jnp.where(hit, prior, 0.0), 0, keepdims=True) for hit in hits]
    count_ref[...] = count_ref[...] + jnp.sum(chosen, 1, keepdims=True)
    rank = jnp.concatenate(rank_rows, 0).astype(jnp.int32)
    return jnp.concatenate(idx_rows, 0), top_w, rank


def _mix_kernel(np_tiles, gp_ref, sp_ref, xp_ref, gs_ref, ss_ref, xs_ref, wo_ref, g1_ref, b1_ref,
                wrt_ref, rbias_ref, x1_ref, tope_ref, topw_ref, rank_ref, cnt_ref, count_acc):
    i = pl.program_id(0)

    @pl.when(i == 0)
    def _():
        count_acc[...] = jnp.zeros_like(count_acc)

    is_prompt = i < np_tiles
    gd = jnp.where(is_prompt, gp_ref[...], gs_ref[...])
    sw = jnp.where(is_prompt, sp_ref[...], ss_ref[...])
    x = jnp.where(is_prompt, xp_ref[...], xs_ref[...])
    mix = _dot(gd, wo_ref[0:GDN_WIDTH, :]) + _dot(sw, wo_ref[GDN_WIDTH:, :])
    x1 = _layer_norm(ALPHA * x + mix, g1_ref[...], b1_ref[...])
    _store_token_tiles(x1_ref, x1)
    top_e, top_w, rank = _route(x1.astype(BF16), wrt_ref, rbias_ref, count_acc)
    tope_ref[...] = top_e
    topw_ref[...] = top_w
    rank_ref[...] = rank
    cnt_ref[...] = count_acc[...].astype(jnp.int32)


def _mix(gd_p, sw_p, x_p, gd_s, sw_s, x_s, wo, g1, b1, wrt, rbias):
    n_p, n_s = x_p.shape[0], x_s.shape[0]
    np_tiles = n_p // MIX_TILE
    nt = n_p + n_s
    pm = lambda i: (jnp.minimum(i, np_tiles - 1), 0)
    sm = lambda i: (jnp.maximum(i - np_tiles, 0), 0)
    const = lambda i: (0, 0)
    col = lambda i: (0, i)
    return pl.pallas_call(
        functools.partial(_mix_kernel, np_tiles),
        grid=(nt // MIX_TILE,),
        in_specs=[
            pl.BlockSpec((MIX_TILE, GDN_WIDTH), pm),
            pl.BlockSpec((MIX_TILE, SWA_WIDTH), pm),
            pl.BlockSpec((MIX_TILE, D_MODEL), pm),
            pl.BlockSpec((MIX_TILE, GDN_WIDTH), sm),
            pl.BlockSpec((MIX_TILE, SWA_WIDTH), sm),
            pl.BlockSpec((MIX_TILE, D_MODEL), sm),
            pl.BlockSpec((D_MODEL, D_MODEL), const),
            pl.BlockSpec((1, D_MODEL), const),
            pl.BlockSpec((1, D_MODEL), const),
            pl.BlockSpec((N_EXPERTS, D_MODEL), const),
            pl.BlockSpec((N_EXPERTS, 1), const),
        ],
        out_specs=(
            pl.BlockSpec((MIX_TILE * ROW_TILES, LANES), lambda i: (i, 0)),
            pl.BlockSpec((TOP_K, MIX_TILE), col),
            pl.BlockSpec((TOP_K, MIX_TILE), col),
            pl.BlockSpec((TOP_K, MIX_TILE), col),
            pl.BlockSpec((N_EXPERTS, 1), const),
        ),
        out_shape=(
            jax.ShapeDtypeStruct((nt * ROW_TILES, LANES), F32),
            jax.ShapeDtypeStruct((TOP_K, nt), jnp.int32),
            jax.ShapeDtypeStruct((TOP_K, nt), F32),
            jax.ShapeDtypeStruct((TOP_K, nt), jnp.int32),
            jax.ShapeDtypeStruct((N_EXPERTS, 1), jnp.int32),
        ),
        scratch_shapes=[pltpu.VMEM((N_EXPERTS, 1), F32)],
        compiler_params=pltpu.CompilerParams(
            dimension_semantics=("arbitrary",), vmem_limit_bytes=VMEM_LIMIT),
        name="mix",
    )(gd_p, sw_p, x_p, gd_s, sw_s, x_s, wo, g1, b1, wrt, rbias)


def _dest_kernel(tope_ref, rank_ref, pstart_ref, dest_ref):
    tm = tope_ref.shape[1]
    r_e = lax.broadcasted_iota(jnp.int32, (N_EXPERTS, tm), 0)
    pstart = pstart_ref[...]
    top_e = tope_ref[...]
    rows = [jnp.sum(jnp.where(r_e == top_e[j:j + 1, :], pstart, 0), 0, keepdims=True)
            for j in range(TOP_K)]
    dest_ref[...] = (jnp.concatenate(rows, 0) + rank_ref[...]) * ROW_TILES


def _dest(top_e_t, rank_t, pstart):
    nt = top_e_t.shape[1]
    col = lambda i: (0, i)
    return pl.pallas_call(
        _dest_kernel,
        grid=(nt // DEST_TILE,),
        in_specs=[
            pl.BlockSpec((TOP_K, DEST_TILE), col),
            pl.BlockSpec((TOP_K, DEST_TILE), col),
            pl.BlockSpec((N_EXPERTS, 1), lambda i: (0, 0)),
        ],
        out_specs=pl.BlockSpec((TOP_K, DEST_TILE), col),
        out_shape=jax.ShapeDtypeStruct((TOP_K, nt), jnp.int32),
        compiler_params=pltpu.CompilerParams(dimension_semantics=("arbitrary",)),
        name="dest",
    )(top_e_t, rank_t, pstart)


def _experts_kernel(be_ref, nused_ref, c0_ref, wpar_ref, wnext_ref, stok_ref, x1_ref,
                    wg_ref, wu_ref, wd_ref, y_ref,
                    idx0, idx1, idx2, xbuf0, xbuf1, xbuf2, ybuf0, ybuf1, ybuf2, zbuf,
                    idx_sem, row_sem, out_sem, z_sem, wg_f, wu_f, wd_f, w_sem,
                    wg_b, wu_b, wd_b):
    i = pl.program_id(0)
    n_used = nused_ref[0]
    n_steps = pl.num_programs(0)
    idx = (idx0, idx1, idx2)
    xbuf = (xbuf0, xbuf1, xbuf2)
    ybuf = (ybuf0, ybuf1, ybuf2)
    block_rows = EXPERT_BLOCK * ROW_TILES

    def out_rows(block):
        return y_ref.at[pl.ds(pl.multiple_of(block * block_rows, block_rows), block_rows), :]

    def out_copy(block, s):
        return pltpu.make_async_copy(ybuf[s], out_rows(block), out_sem.at[s])

    def zero_copy(block):
        return pltpu.make_async_copy(zbuf, out_rows(block), z_sem)

    def weight_copies(e, s):
        return (pltpu.make_async_copy(wg_ref.at[e], wg_f.at[s], w_sem.at[s]),
                pltpu.make_async_copy(wu_ref.at[e], wu_f.at[s], w_sem.at[s]),
                pltpu.make_async_copy(wd_ref.at[e], wd_f.at[s], w_sem.at[s]))

    def idx_copy(block, s):
        base = pl.multiple_of(c0_ref[block] // LANES * LANES, LANES)
        return pltpu.make_async_copy(stok_ref.at[pl.ds(base, IDX_WINDOW)], idx[s], idx_sem.at[s])

    def start_rows(block, s):
        off = lax.rem(c0_ref[block], LANES)
        for r in range(EXPERT_BLOCK):
            src = pl.multiple_of(idx[s][off + r], ROW_TILES)
            pltpu.make_async_copy(x1_ref.at[pl.ds(src, ROW_TILES), :],
                                  xbuf[s].at[pl.ds(r * ROW_TILES, ROW_TILES), :],
                                  row_sem.at[s]).start(priority=r % 2)

    def wait_rows(s):
        pltpu.make_async_copy(x1_ref.at[pl.ds(0, EXPERT_BLOCK * ROW_TILES), :], xbuf[s],
                              row_sem.at[s]).wait()

    @pl.when(i == 0)
    def _():
        for cp in weight_copies(be_ref[0], 0):
            cp.start()
        idx_copy(0, 0).start()
        idx_copy(1, 1).start()
        idx_copy(0, 0).wait()
        idx_copy(1, 1).wait()
        start_rows(0, 0)
        start_rows(1, 1)
        idx_copy(2, 2).start()

    prev = be_ref[jnp.maximum(i - 1, 0)]

    @pl.when((i < n_used) & ((i == 0) | (be_ref[i] != prev)))
    def _():
        par = wpar_ref[i]
        for cp in weight_copies(be_ref[i], par):
            cp.wait()

        @pl.when(wnext_ref[i] >= 0)
        def _():
            for cp in weight_copies(wnext_ref[i], 1 - par):
                cp.start()

        wg_b[...] = wg_f[par].astype(BF16)
        wu_b[...] = wu_f[par].astype(BF16)
        wd_b[...] = wd_f[par].astype(BF16)

    for slot in range(3):
        ahead = (slot + 2) % 3

        @pl.when((i < n_used) & (i >= 3) & (lax.rem(i, 3) == slot))
        def _():
            out_copy(i - 3, slot).wait()

        @pl.when((i < n_used) & (lax.rem(i, 3) == slot))
        def _():
            idx_copy(i + 2, ahead).wait()
            wait_rows(slot)
            start_rows(i + 2, ahead)
            idx_copy(i + 3, slot).start()
            xb = _load_token_tiles(xbuf[slot], EXPERT_BLOCK).astype(BF16)
            hb = _silu(_dot(xb, wg_b[...])) * _dot(xb, wu_b[...])
            _store_token_tiles(ybuf[slot], _dot(hb.astype(BF16), wd_b[...]))
            out_copy(i, slot).start()

        @pl.when((i == n_used) & (lax.rem(i, 3) == slot))
        def _():
            wait_rows(slot)
            wait_rows((slot + 1) % 3)
            idx_copy(i + 2, ahead).wait()
            for back in range(1, 4):
                @pl.when(i >= back)
                def _():
                    out_copy(i - back, (slot - back) % 3).wait()
            zbuf[...] = jnp.zeros_like(zbuf)

    @pl.when(i >= n_used)
    def _():
        zero_copy(i).start()

    @pl.when(i == n_steps - 1)
    def _():
        def drain(_, carry):
            zero_copy(0).wait()
            return carry
        lax.fori_loop(0, n_steps - n_used, drain, 0)


def _experts(block_e, n_used, block_c0, w_par, w_next, stok, x1, w_gate_e, w_up_e, w_down_e):
    nb = block_e.shape[0]
    grid_spec = pltpu.PrefetchScalarGridSpec(
        num_scalar_prefetch=5,
        grid=(nb,),
        in_specs=[pl.BlockSpec(memory_space=pl.ANY)] * 5,
        out_specs=pl.BlockSpec(memory_space=pl.ANY),
        scratch_shapes=(
            [pltpu.SMEM((IDX_WINDOW,), jnp.int32)] * 3
            + [pltpu.VMEM((EXPERT_BLOCK * ROW_TILES, LANES), F32)] * 7
        ) + [
            pltpu.SemaphoreType.DMA((3,)),
            pltpu.SemaphoreType.DMA((3,)),
            pltpu.SemaphoreType.DMA((3,)),
            pltpu.SemaphoreType.DMA(()),
            pltpu.VMEM((2, D_MODEL, EXPERT_FF), F32),
            pltpu.VMEM((2, D_MODEL, EXPERT_FF), F32),
            pltpu.VMEM((2, EXPERT_FF, D_MODEL), F32),
            pltpu.SemaphoreType.DMA((2,)),
            pltpu.VMEM((D_MODEL, EXPERT_FF), BF16),
            pltpu.VMEM((D_MODEL, EXPERT_FF), BF16),
            pltpu.VMEM((EXPERT_FF, D_MODEL), BF16),
        ],
    )
    return pl.pallas_call(
        _experts_kernel,
        grid_spec=grid_spec,
        out_shape=jax.ShapeDtypeStruct((nb * EXPERT_BLOCK * ROW_TILES, LANES), F32),
        compiler_params=pltpu.CompilerParams(
            dimension_semantics=("arbitrary",), vmem_limit_bytes=VMEM_LIMIT),
        name="experts",
    )(block_e, n_used, block_c0, w_par, w_next, stok, x1, w_gate_e, w_up_e, w_down_e)


def _final_kernel(np_tiles, dest_ref, dest_next_ref, x1_ref, topw_ref, yb_ref, wsg_ref, wsu_ref,
                  wsd_ref, g2_ref, b2_ref, yp_ref, ys_ref, ybuf, sem):
    i = pl.program_id(0)
    slot = lax.rem(i, 2)

    def issue_rows(d_ref, s):
        def body(t, carry):
            dst = pl.multiple_of(t * ROW_TILES, ROW_TILES)
            for j in range(TOP_K):
                src = pl.multiple_of(d_ref[j, t], ROW_TILES)
                pltpu.make_async_copy(yb_ref.at[pl.ds(src, ROW_TILES), :],
                                      ybuf.at[s, j, pl.ds(dst, ROW_TILES), :],
                                      sem.at[s]).start(priority=j % 2)
            return carry
        lax.fori_loop(0, MIX_TILE, body, 0, unroll=4)

    @pl.when(i == 0)
    def _():
        issue_rows(dest_ref, 0)

    @pl.when(i + 1 < pl.num_programs(0))
    def _():
        issue_rows(dest_next_ref, 1 - slot)

    x1 = _load_token_tiles(x1_ref, MIX_TILE)
    xb = x1.astype(BF16)
    hs = _silu(_dot(xb, wsg_ref[...])) * _dot(xb, wsu_ref[...])
    ff = _dot(hs.astype(BF16), wsd_ref[...])
    r = lax.broadcasted_iota(jnp.int32, (MIX_TILE, MIX_TILE), 0)
    c = lax.broadcasted_iota(jnp.int32, (MIX_TILE, MIX_TILE), 1)
    w_cols = _dot_nt((r == c).astype(F32), topw_ref[...], HIGHEST)
    for j in range(TOP_K):
        pltpu.make_async_copy(yb_ref.at[pl.ds(0, MIX_TILE * ROW_TILES), :], ybuf.at[slot, j],
                              sem.at[slot]).wait()
    for j in range(TOP_K):
        ff = ff + w_cols[:, j:j + 1] * _load_token_tiles(ybuf.at[slot, j], MIX_TILE)
    y = _layer_norm(ALPHA * x1 + ff, g2_ref[...], b2_ref[...])

    @pl.when(i < np_tiles)
    def _():
        yp_ref[...] = y

    @pl.when(i >= np_tiles)
    def _():
        ys_ref[...] = y


def _final(n_p, dest, x1, top_w_t, yb, wsg, wsu, wsd, g2, b2):
    nt = x1.shape[0] // ROW_TILES
    np_tiles = n_p // MIX_TILE
    row = lambda i: (i, 0)
    const = lambda i: (0, 0)
    n_tiles = nt // MIX_TILE
    return pl.pallas_call(
        functools.partial(_final_kernel, np_tiles),
        grid=(n_tiles,),
        in_specs=[
            pl.BlockSpec((TOP_K, MIX_TILE), lambda i: (0, i), memory_space=pltpu.SMEM),
            pl.BlockSpec((TOP_K, MIX_TILE), lambda i: (0, jnp.minimum(i + 1, n_tiles - 1)),
                         memory_space=pltpu.SMEM),
            pl.BlockSpec((MIX_TILE * ROW_TILES, LANES), row),
            pl.BlockSpec((TOP_K, MIX_TILE), lambda i: (0, i)),
            pl.BlockSpec(memory_space=pl.ANY),
            pl.BlockSpec((D_MODEL, EXPERT_FF), const),
            pl.BlockSpec((D_MODEL, EXPERT_FF), const),
            pl.BlockSpec((EXPERT_FF, D_MODEL), const),
            pl.BlockSpec((1, D_MODEL), const),
            pl.BlockSpec((1, D_MODEL), const),
        ],
        out_specs=(
            pl.BlockSpec((MIX_TILE, D_MODEL), lambda i: (jnp.minimum(i, np_tiles - 1), 0)),
            pl.BlockSpec((MIX_TILE, D_MODEL), lambda i: (jnp.maximum(i - np_tiles, 0), 0)),
        ),
        out_shape=(
            jax.ShapeDtypeStruct((n_p, D_MODEL), F32),
            jax.ShapeDtypeStruct((nt - n_p, D_MODEL), F32),
        ),
        scratch_shapes=[
            pltpu.VMEM((2, TOP_K, MIX_TILE * ROW_TILES, LANES), F32),
            pltpu.SemaphoreType.DMA((2,)),
        ],
        compiler_params=pltpu.CompilerParams(
            dimension_semantics=("arbitrary",), vmem_limit_bytes=VMEM_LIMIT),
        name="final",
    )(dest, dest, x1, top_w_t, yb, wsg, wsu, wsd, g2, b2)


def _expert_layout(counts, nk):
    nb = -(-nk // EXPERT_BLOCK) + N_EXPERTS
    padded = (counts + EXPERT_BLOCK - 1) // EXPERT_BLOCK * EXPERT_BLOCK
    pend = jnp.cumsum(padded)
    pstart = (pend - padded).astype(jnp.int32)
    block_start = jnp.arange(nb, dtype=jnp.int32) * EXPERT_BLOCK
    block_e = jnp.sum((pend[None, :] <= block_start[:, None]).astype(jnp.int32), 1)
    block_e = jnp.minimum(block_e, N_EXPERTS - 1)
    n_used = (pend[-1] // EXPERT_BLOCK).astype(jnp.int32).reshape(1)
    start = (jnp.cumsum(counts) - counts).astype(jnp.int32)
    block_c0 = start[block_e] + block_start - pstart[block_e]
    block_c0 = jnp.where(block_start < pend[-1], block_c0, 0)
    block_c0 = jnp.concatenate([block_c0, jnp.zeros((2,), jnp.int32)]).astype(jnp.int32)
    ids = jnp.arange(nb, dtype=jnp.int32)
    used = block_start < pend[-1]
    switch = used & ((ids == 0) | (block_e != jnp.roll(block_e, 1)))
    w_par = ((jnp.cumsum(switch.astype(jnp.int32)) - 1) % 2).astype(jnp.int32)
    later_switch = (ids[None, :] > ids[:, None]) & switch[None, :]
    next_pos = jnp.min(jnp.where(later_switch, ids[None, :], nb), axis=1)
    w_next = jnp.where(next_pos < nb, block_e[jnp.minimum(next_pos, nb - 1)], -1).astype(jnp.int32)
    return pstart, block_e, n_used, block_c0, w_par, w_next


def _pad_lanes(v, offset=0):
    out = jnp.zeros((1, LANES), F32)
    return out.at[0, offset:offset + v.shape[0]].set(v.astype(F32))


def kernel(x_prompt, x_sample, state_conv, state_gdn, cache_swa_k, cache_swa_v, w_in, conv_w, a_log, dt_bias, gdn_norm_w, attn_sinks, w_o, ln1_g, ln1_b, w_router, router_bias, w_gate_e, w_up_e, w_down_e, w_shared_gate, w_shared_up, w_shared_down, ln2_g, ln2_b):
    bp, tp, d = x_prompt.shape
    bs, ts, _ = x_sample.shape
    n_p, n_s = bp * tp, bs * ts

    wi = w_in[0]
    o1 = CONV_DIM
    o2 = o1 + GDN_WIDTH
    o4 = o2 + 2 * GDN_HEADS
    o5 = o4 + SWA_WIDTH
    o6 = o5 + SWA_KV_WIDTH
    ba_cols = jnp.zeros((d, LANES), F32).at[:, :2 * GDN_HEADS].set(wi[:, o2:o4])
    w_all = jnp.concatenate([wi[:, :o2], wi[:, o4:o5], wi[:, o5:o6], wi[:, o6:], ba_cols], 1).astype(BF16)
    wo = w_o[0].astype(BF16)
    wrt = w_router[0].T.astype(BF16)
    rbias = router_bias[0].astype(F32).reshape(N_EXPERTS, 1)
    alog_pad = _pad_lanes(a_log[0], GDN_HEADS)
    dtb_pad = _pad_lanes(dt_bias[0], GDN_HEADS)
    sinks_pad = _pad_lanes(attn_sinks[0])
    norm_w = gdn_norm_w[0].reshape(1, GDN_D)
    g1, b1 = ln1_g[0].reshape(1, d), ln1_b[0].reshape(1, d)
    g2, b2 = ln2_g[0].reshape(1, d), ln2_b[0].reshape(1, d)

    def front(x, pos0, L, conv_hist, s0, k_hist, v_hist):
        b, t, _ = x.shape
        period = max(t, IN_TILE)
        pos = pos0 + (jnp.arange(period, dtype=jnp.int32) % t).astype(F32)
        cos, s1, s2 = _rope_tables(pos)
        x2d = x.reshape(b * t, d)
        qkv, z, sq, sk, sv, ba = _in_proj(x2d, w_all, cos, s1, s2)
        qkv = qkv.reshape(b, t, CONV_DIM)
        hist8 = jnp.concatenate(
            [jnp.zeros((b, HIST_ROWS - (CONV_WIDTH - 1), CONV_DIM), F32), conv_hist], 1)
        gd, s_new = _gdn(L, qkv, z.reshape(b, t, GDN_WIDTH), ba.reshape(b, t, LANES), hist8, s0,
                         conv_w[0], alog_pad, dtb_pad, norm_w)
        k_ext = jnp.concatenate([k_hist.reshape(b, WINDOW, SWA_KV_WIDTH),
                                 sk.reshape(b, t, SWA_KV_WIDTH)], 1)
        v_ext = jnp.concatenate([v_hist.reshape(b, WINDOW, SWA_KV_WIDTH),
                                 sv.reshape(b, t, SWA_KV_WIDTH)], 1)
        sw = _swa(L, pos0, sq.reshape(b, t, SWA_WIDTH), k_ext, v_ext, sinks_pad)
        conv_new = jnp.concatenate([conv_hist, qkv], 1)[:, -(CONV_WIDTH - 1):]
        k_new = k_ext[:, -WINDOW:].reshape(b, WINDOW, SWA_KV_HEADS, SWA_HEAD_DIM)
        v_new = v_ext[:, -WINDOW:].reshape(b, WINDOW, SWA_KV_HEADS, SWA_HEAD_DIM)
        return (x2d, gd.reshape(b * t, GDN_WIDTH), sw.reshape(b * t, SWA_WIDTH),
                conv_new, s_new, k_new, v_new)

    zeros = lambda *s: jnp.zeros(s, F32)
    xp2, gd_p, sw_p, conv_p, gdn_p, k_p, v_p = front(
        x_prompt, 0, CHUNK, zeros(bp, CONV_WIDTH - 1, CONV_DIM),
        zeros(bp, GDN_HEADS, GDN_D, GDN_D), zeros(bp, WINDOW, SWA_KV_WIDTH),
        zeros(bp, WINDOW, SWA_KV_WIDTH))
    xs2, gd_s, sw_s, conv_s, gdn_s, k_s, v_s = front(
        x_sample, PAST_LEN, ts, state_conv[0], state_gdn[0], cache_swa_k[0], cache_swa_v[0])

    x1, top_e_t, top_w_t, rank_t, counts = _mix(
        gd_p, sw_p, xp2, gd_s, sw_s, xs2, wo, g1, b1, wrt, rbias)

    pstart, block_e, n_used, block_c0, w_par, w_next = _expert_layout(
        counts[:, 0], TOP_K * (n_p + n_s))
    dest = _dest(top_e_t, rank_t, pstart.reshape(N_EXPERTS, 1))
    order = jnp.argsort(top_e_t.T.reshape(-1))
    stok = jnp.concatenate([(order // TOP_K * ROW_TILES).astype(jnp.int32),
                            jnp.zeros((IDX_WINDOW,), jnp.int32)])
    yb = _experts(block_e, n_used, block_c0, w_par, w_next, stok, x1,
                  w_gate_e[0], w_up_e[0], w_down_e[0])

    y_p, y_s = _final(n_p, dest, x1, top_w_t, yb,
                      w_shared_gate[0].astype(BF16), w_shared_up[0].astype(BF16),
                      w_shared_down[0].astype(BF16), g2, b2)
    return (y_p.reshape(bp, tp, d), y_s.reshape(bs, ts, d),
            conv_p[None], gdn_p[None], k_p[None], v_p[None],
            conv_s[None], gdn_s[None], k_s[None], v_s[None])
```
